```python
import jax, jax.numpy as jnp
from jax import lax
import numpy as np

D_MODEL = 1024
BATCH = 8
SEQ = 8192
DEPTH = 4

A_W = D_MODEL // 2
CONV_A = 31
B_W = D_MODEL // 2
CONV_B = 3
C_HEADS = 8
C_HEAD_DIM = 64
C_W = C_HEADS * C_HEAD_DIM
Q_BLOCK = 128
M_HEADS = 4
M_HEAD_DIM = 64
M_W = M_HEADS * M_HEAD_DIM
MEM_LEN = 256
N_BRANCH = 4
ALPHA = (2.0 * DEPTH) ** 0.25
BETA = (8.0 * DEPTH) ** -0.25
LN_EPS = 1e-5

SPLITS = (
    2 * A_W, A_W,
    B_W, B_W, B_W, B_W,
    C_W, C_W, C_W, C_HEADS, C_W,
    M_W, M_W,
    N_BRANCH * D_MODEL,
)
IN_COLS = sum(SPLITS)
SPLIT_CUTS = tuple(int(c) for c in np.cumsum(SPLITS)[:-1])

kernel_name = 'hybrid_gated_conformer_shortconv_fox_memxattn'


def _layer_norm(x, g, b):
    xf = x.astype(jnp.float32)
    mu = jnp.mean(xf, axis=-1, keepdims=True)
    var = jnp.mean(jnp.square(xf - mu), axis=-1, keepdims=True)
    y = ((xf - mu) * lax.rsqrt(var + LN_EPS)).astype(x.dtype)
    return y * g + b


def _causal_dwconv(x, w):
    k_width, ch = w.shape
    return lax.conv_general_dilated(
        x, w[:, None, :].astype(x.dtype), window_strides=(1,),
        padding=[(k_width - 1, 0)],
        dimension_numbers=('NWC', 'WIO', 'NWC'),
        feature_group_count=ch)


def _forgetting_attention(q, k, v, logf):
    bsz, seq, heads, dh = q.shape
    scale = dh ** -0.5
    cum = jnp.cumsum(logf, axis=1).transpose(0, 2, 1)
    outs = []
    for i in range(seq // Q_BLOCK):
        q0, end = i * Q_BLOCK, (i + 1) * Q_BLOCK
        s = jnp.einsum('bqhd,bkhd->bhqk', q[:, q0:end], k[:, :end]).astype(jnp.float32) * scale
        decay = cum[:, :, q0:end, None] - cum[:, :, None, :end]
        q_pos = q0 + jnp.arange(Q_BLOCK)
        k_pos = jnp.arange(end)
        causal = q_pos[:, None] >= k_pos[None, :]
        s = jnp.where(causal, s + decay, -jnp.inf)
        p = jax.nn.softmax(s, axis=-1).astype(v.dtype)
        outs.append(jnp.einsum('bhqk,bkhd->bqhd', p, v[:, :end]))
    return jnp.concatenate(outs, axis=1)


def _cross_attention(q, k, v):
    dh = q.shape[-1]
    s = jnp.einsum('bshd,bmhd->bhsm', q, k).astype(jnp.float32) * (dh ** -0.5)
    p = jax.nn.softmax(s, axis=-1).astype(v.dtype)
    return jnp.einsum('bhsm,bmhd->bshd', p, v)


def _fwd_setup_inputs(seed: int = 0) -> dict:
    key = jax.random.key(seed)
    ks = jax.random.split(key, 20)
    f32 = jnp.float32
    nrm = lambda k, shape, scale: jax.random.normal(k, shape, f32) * scale
    return {
        'x': nrm(ks[0], (BATCH, SEQ, D_MODEL), 1.0),
        'mem': nrm(ks[1], (BATCH, MEM_LEN, D_MODEL), 1.0),
        'w_in': nrm(ks[2], (DEPTH, D_MODEL, IN_COLS), D_MODEL ** -0.5),
        'b_forget': jax.random.uniform(ks[3], (DEPTH, C_HEADS), f32, 1.0, 5.0),
        'conv_a_w': nrm(ks[4], (DEPTH, CONV_A, A_W), CONV_A ** -0.5),
        'conv_a_b': nrm(ks[5], (DEPTH, A_W), 0.02),
        'ln_a_g': 1.0 + nrm(ks[6], (DEPTH, A_W), 0.05),
        'ln_a_b': nrm(ks[7], (DEPTH, A_W), 0.02),
        'conv_b_w': nrm(ks[8], (DEPTH, CONV_B, B_W), CONV_B ** -0.5),
        'w_kv_mem': nrm(ks[9], (DEPTH, D_MODEL, 2 * M_W), D_MODEL ** -0.5),
        'mem_ln_g': 1.0 + nrm(ks[10], (D_MODEL,), 0.05),
        'mem_ln_b': nrm(ks[11], (D_MODEL,), 0.02),
        'p_a': nrm(ks[12], (DEPTH, A_W, D_MODEL), BETA * A_W ** -0.5),
        'p_b': nrm(ks[13], (DEPTH, B_W, D_MODEL), BETA * B_W ** -0.5),
        'p_c': nrm(ks[14], (DEPTH, C_W, D_MODEL), BETA * C_W ** -0.5),
        'p_m': nrm(ks[15], (DEPTH, M_W, D_MODEL), BETA * M_W ** -0.5),
        'w_out': nrm(ks[16], (DEPTH, D_MODEL, D_MODEL), BETA * D_MODEL ** -0.5),
        'ln_g': 1.0 + nrm(ks[17], (DEPTH, D_MODEL), 0.05),
        'ln_b': nrm(ks[18], (DEPTH, D_MODEL), 0.02),
    }


def _fwd_reference(x, mem, w_in, b_forget, conv_a_w, conv_a_b, ln_a_g, ln_a_b, conv_b_w,
              w_kv_mem, mem_ln_g, mem_ln_b, p_a, p_b, p_c, p_m, w_out, ln_g, ln_b):
    bsz, seq, _ = x.shape
    mem_n = _layer_norm(mem, mem_ln_g, mem_ln_b)
    for l in range(DEPTH):
        proj = x @ w_in[l]
        (a_glu, a_gate, b_h, b_b, b_c, b_gate, c_q, c_k, c_v, c_f, c_gate,
         m_q, m_gate, g) = jnp.split(proj, SPLIT_CUTS, axis=-1)

        a_u, a_v = jnp.split(a_glu, 2, axis=-1)
        a = _causal_dwconv(a_u * jax.nn.sigmoid(a_v), conv_a_w[l]) + conv_a_b[l]
        a = jax.nn.silu(_layer_norm(a, ln_a_g[l], ln_a_b[l]))
        y_a = (a * jax.nn.silu(a_gate)) @ p_a[l]

        hb = b_b * _causal_dwconv(b_c * b_h, conv_b_w[l])
        y_b = (hb * jax.nn.silu(b_gate)) @ p_b[l]

        q = c_q.reshape(bsz, seq, C_HEADS, C_HEAD_DIM)
        k = c_k.reshape(bsz, seq, C_HEADS, C_HEAD_DIM)
        v = c_v.reshape(bsz, seq, C_HEADS, C_HEAD_DIM)
        logf = jax.nn.log_sigmoid((c_f + b_forget[l]).astype(jnp.float32))
        o_c = _forgetting_attention(q, k, v, logf).reshape(bsz, seq, C_W)
        y_c = (o_c * jax.nn.silu(c_gate)) @ p_c[l]

        mk, mv = jnp.split(mem_n @ w_kv_mem[l], 2, axis=-1)
        o_m = _cross_attention(
            m_q.reshape(bsz, seq, M_HEADS, M_HEAD_DIM),
            mk.reshape(bsz, MEM_LEN, M_HEADS, M_HEAD_DIM),
            mv.reshape(bsz, MEM_LEN, M_HEADS, M_HEAD_DIM)).reshape(bsz, seq, M_W)
        y_m = (o_m * jax.nn.silu(m_gate)) @ p_m[l]

        gates = jax.nn.sigmoid(g.reshape(bsz, seq, N_BRANCH, D_MODEL))
        merged = (gates[:, :, 0] * y_a + gates[:, :, 1] * y_b
                  + gates[:, :, 2] * y_c + gates[:, :, 3] * y_m)
        out = merged @ w_out[l]

        x = _layer_norm(ALPHA * x + out, ln_g[l], ln_b[l])
    return x


import jax as _jax
import jax.numpy as _jnp

TWIN_FORMAT = 'train_step'
FWD_PARAMS = ['x', 'mem', 'w_in', 'b_forget', 'conv_a_w', 'conv_a_b', 'ln_a_g', 'ln_a_b', 'conv_b_w', 'w_kv_mem', 'mem_ln_g', 'mem_ln_b', 'p_a', 'p_b', 'p_c', 'p_m', 'w_out', 'ln_g', 'ln_b']
TWIN_WEIGHTS = ['w_in', 'b_forget', 'conv_a_w', 'conv_a_b', 'ln_a_g', 'ln_a_b', 'conv_b_w', 'w_kv_mem', 'mem_ln_g', 'mem_ln_b', 'p_a', 'p_b', 'p_c', 'p_m', 'w_out', 'ln_g', 'ln_b']
TWIN_DIFF_INPUT = 'x'
TWIN_INPUTS = ['x', 'mem', 'w_in', 'b_forget', 'conv_a_w', 'conv_a_b', 'ln_a_g', 'ln_a_b', 'conv_b_w', 'w_kv_mem', 'mem_ln_g', 'mem_ln_b', 'p_a', 'p_b', 'p_c', 'p_m', 'w_out', 'ln_g', 'ln_b', 'loss_target', 'm_w_in', 'm_b_forget', 'm_conv_a_w', 'm_conv_a_b', 'm_ln_a_g', 'm_ln_a_b', 'm_conv_b_w', 'm_w_kv_mem', 'm_mem_ln_g', 'm_mem_ln_b', 'm_p_a', 'm_p_b', 'm_p_c', 'm_p_m', 'm_w_out', 'm_ln_g', 'm_ln_b', 'v_w_in', 'v_b_forget', 'v_conv_a_w', 'v_conv_a_b', 'v_ln_a_g', 'v_ln_a_b', 'v_conv_b_w', 'v_w_kv_mem', 'v_mem_ln_g', 'v_mem_ln_b', 'v_p_a', 'v_p_b', 'v_p_c', 'v_p_m', 'v_w_out', 'v_ln_g', 'v_ln_b']
TWIN_OUTPUTS = ['loss', 'grad_x', 'grad_w_in', 'grad_b_forget', 'grad_conv_a_w', 'grad_conv_a_b', 'grad_ln_a_g', 'grad_ln_a_b', 'grad_conv_b_w', 'grad_w_kv_mem', 'grad_mem_ln_g', 'grad_mem_ln_b', 'grad_p_a', 'grad_p_b', 'grad_p_c', 'grad_p_m', 'grad_w_out', 'grad_ln_g', 'grad_ln_b', 'delta_w_in', 'delta_b_forget', 'delta_conv_a_w', 'delta_conv_a_b', 'delta_ln_a_g', 'delta_ln_a_b', 'delta_conv_b_w', 'delta_w_kv_mem', 'delta_mem_ln_g', 'delta_mem_ln_b', 'delta_p_a', 'delta_p_b', 'delta_p_c', 'delta_p_m', 'delta_w_out', 'delta_ln_g', 'delta_ln_b', 'new_m_w_in', 'new_m_b_forget', 'new_m_conv_a_w', 'new_m_conv_a_b', 'new_m_ln_a_g', 'new_m_ln_a_b', 'new_m_conv_b_w', 'new_m_w_kv_mem', 'new_m_mem_ln_g', 'new_m_mem_ln_b', 'new_m_p_a', 'new_m_p_b', 'new_m_p_c', 'new_m_p_m', 'new_m_w_out', 'new_m_ln_g', 'new_m_ln_b', 'new_v_w_in', 'new_v_b_forget', 'new_v_conv_a_w', 'new_v_conv_a_b', 'new_v_ln_a_g', 'new_v_ln_a_b', 'new_v_conv_b_w', 'new_v_w_kv_mem', 'new_v_mem_ln_g', 'new_v_mem_ln_b', 'new_v_p_a', 'new_v_p_b', 'new_v_p_c', 'new_v_p_m', 'new_v_w_out', 'new_v_ln_g', 'new_v_ln_b']
TWIN_LEAF_KINDS = {'loss': 'loss', 'grad_x': 'grad_x', 'grad_w_in': 'grad_w', 'grad_b_forget': 'grad_w', 'grad_conv_a_w': 'grad_w', 'grad_conv_a_b': 'grad_w', 'grad_ln_a_g': 'grad_w', 'grad_ln_a_b': 'grad_w', 'grad_conv_b_w': 'grad_w', 'grad_w_kv_mem': 'grad_w', 'grad_mem_ln_g': 'grad_w', 'grad_mem_ln_b': 'grad_w', 'grad_p_a': 'grad_w', 'grad_p_b': 'grad_w', 'grad_p_c': 'grad_w', 'grad_p_m': 'grad_w', 'grad_w_out': 'grad_w', 'grad_ln_g': 'grad_w', 'grad_ln_b': 'grad_w', 'delta_w_in': 'delta_w', 'delta_b_forget': 'delta_w', 'delta_conv_a_w': 'delta_w', 'delta_conv_a_b': 'delta_w', 'delta_ln_a_g': 'delta_w', 'delta_ln_a_b': 'delta_w', 'delta_conv_b_w': 'delta_w', 'delta_w_kv_mem': 'delta_w', 'delta_mem_ln_g': 'delta_w', 'delta_mem_ln_b': 'delta_w', 'delta_p_a': 'delta_w', 'delta_p_b': 'delta_w', 'delta_p_c': 'delta_w', 'delta_p_m': 'delta_w', 'delta_w_out': 'delta_w', 'delta_ln_g': 'delta_w', 'delta_ln_b': 'delta_w', 'new_m_w_in': 'new_m', 'new_m_b_forget': 'new_m', 'new_m_conv_a_w': 'new_m', 'new_m_conv_a_b': 'new_m', 'new_m_ln_a_g': 'new_m', 'new_m_ln_a_b': 'new_m', 'new_m_conv_b_w': 'new_m', 'new_m_w_kv_mem': 'new_m', 'new_m_mem_ln_g': 'new_m', 'new_m_mem_ln_b': 'new_m', 'new_m_p_a': 'new_m', 'new_m_p_b': 'new_m', 'new_m_p_c': 'new_m', 'new_m_p_m': 'new_m', 'new_m_w_out': 'new_m', 'new_m_ln_g': 'new_m', 'new_m_ln_b': 'new_m', 'new_v_w_in': 'new_v', 'new_v_b_forget': 'new_v', 'new_v_conv_a_w': 'new_v', 'new_v_conv_a_b': 'new_v', 'new_v_ln_a_g': 'new_v', 'new_v_ln_a_b': 'new_v', 'new_v_conv_b_w': 'new_v', 'new_v_w_kv_mem': 'new_v', 'new_v_mem_ln_g': 'new_v', 'new_v_mem_ln_b': 'new_v', 'new_v_p_a': 'new_v', 'new_v_p_b': 'new_v', 'new_v_p_c': 'new_v', 'new_v_p_m': 'new_v', 'new_v_w_out': 'new_v', 'new_v_ln_g': 'new_v', 'new_v_ln_b': 'new_v'}


def _forward(args):
    return _fwd_reference(*[args[k] for k in FWD_PARAMS])


def _output_shape():
    out = _jax.eval_shape(lambda: _forward(_fwd_setup_inputs(0)))
    return out.shape, out.dtype

N_MICROBATCH = 1
ADAM_LR = 0.001
ADAM_B1 = 0.9
ADAM_B2 = 0.999
ADAM_EPS = 1e-08
ADAM_WD = 0.01
ADAM_STEP = 10
PER_EXAMPLE_BATCH_AXIS = {'x': 0, 'mem': 0, 'loss_target': 0}
SHARED_INPUTS = []
_WEIGHT_DTYPES = {'w_in': _jnp.float32, 'b_forget': _jnp.float32, 'conv_a_w': _jnp.float32, 'conv_a_b': _jnp.float32, 'ln_a_g': _jnp.float32, 'ln_a_b': _jnp.float32, 'conv_b_w': _jnp.float32, 'w_kv_mem': _jnp.float32, 'mem_ln_g': _jnp.float32, 'mem_ln_b': _jnp.float32, 'p_a': _jnp.float32, 'p_b': _jnp.float32, 'p_c': _jnp.float32, 'p_m': _jnp.float32, 'w_out': _jnp.float32, 'ln_g': _jnp.float32, 'ln_b': _jnp.float32}
MOMENT_SCALE = {'w_in': 6.357566e-03, 'b_forget': 1.669327e-02, 'conv_a_w': 7.527407e-03, 'conv_a_b': 1.628895e-02, 'ln_a_g': 8.946975e-03, 'ln_a_b': 8.565916e-03, 'conv_b_w': 1.294407e-02, 'w_kv_mem': 1.849802e-03, 'mem_ln_g': 2.705764e-03, 'mem_ln_b': 2.201411e-02, 'p_a': 1.253581e-02, 'p_b': 2.037494e-02, 'p_c': 7.649329e-03, 'p_m': 2.236667e-03, 'w_out': 2.490597e-02, 'ln_g': 3.278477e+01, 'ln_b': 1.063050e+00}


def _to_microbatches(a, axis):
    t = _jnp.moveaxis(a, axis, 0)
    t = t.reshape((N_MICROBATCH, t.shape[0] // N_MICROBATCH) + t.shape[1:])
    return _jnp.moveaxis(t, 1, axis + 1)


def setup_inputs(seed: int = 0) -> dict:
    inp = _fwd_setup_inputs(seed)
    key = _jax.random.fold_in(_jax.random.key(seed), 7919)
    shape, _ = _output_shape()
    out = dict(inp)
    out["loss_target"] = _jax.random.normal(_jax.random.fold_in(key, 0), shape, _jnp.float32)
    for i, name in enumerate(TWIN_WEIGHTS):
        w = inp[name].astype(_jnp.float32)
        if MOMENT_SCALE is None:
            s = _jnp.sqrt(_jnp.mean(_jnp.square(w)) + 1e-30)
        else:
            s = MOMENT_SCALE[name]
        km, kv = _jax.random.split(_jax.random.fold_in(key, i + 1))
        out[name] = w
        out["m_" + name] = s * _jax.random.normal(km, w.shape, _jnp.float32)
        out["v_" + name] = (s * s) * _jax.random.uniform(kv, w.shape, _jnp.float32, 0.5, 1.5)
    if N_MICROBATCH > 1:
        for name, axis in PER_EXAMPLE_BATCH_AXIS.items():
            out[name] = _to_microbatches(out[name], axis)
    return {'x': out['x'], 'mem': out['mem'], 'w_in': out['w_in'], 'b_forget': out['b_forget'], 'conv_a_w': out['conv_a_w'], 'conv_a_b': out['conv_a_b'], 'ln_a_g': out['ln_a_g'], 'ln_a_b': out['ln_a_b'], 'conv_b_w': out['conv_b_w'], 'w_kv_mem': out['w_kv_mem'], 'mem_ln_g': out['mem_ln_g'], 'mem_ln_b': out['mem_ln_b'], 'p_a': out['p_a'], 'p_b': out['p_b'], 'p_c': out['p_c'], 'p_m': out['p_m'], 'w_out': out['w_out'], 'ln_g': out['ln_g'], 'ln_b': out['ln_b'], 'loss_target': out['loss_target'], 'm_w_in': out['m_w_in'], 'm_b_forget': out['m_b_forget'], 'm_conv_a_w': out['m_conv_a_w'], 'm_conv_a_b': out['m_conv_a_b'], 'm_ln_a_g': out['m_ln_a_g'], 'm_ln_a_b': out['m_ln_a_b'], 'm_conv_b_w': out['m_conv_b_w'], 'm_w_kv_mem': out['m_w_kv_mem'], 'm_mem_ln_g': out['m_mem_ln_g'], 'm_mem_ln_b': out['m_mem_ln_b'], 'm_p_a': out['m_p_a'], 'm_p_b': out['m_p_b'], 'm_p_c': out['m_p_c'], 'm_p_m': out['m_p_m'], 'm_w_out': out['m_w_out'], 'm_ln_g': out['m_ln_g'], 'm_ln_b': out['m_ln_b'], 'v_w_in': out['v_w_in'], 'v_b_forget': out['v_b_forget'], 'v_conv_a_w': out['v_conv_a_w'], 'v_conv_a_b': out['v_conv_a_b'], 'v_ln_a_g': out['v_ln_a_g'], 'v_ln_a_b': out['v_ln_a_b'], 'v_conv_b_w': out['v_conv_b_w'], 'v_w_kv_mem': out['v_w_kv_mem'], 'v_mem_ln_g': out['v_mem_ln_g'], 'v_mem_ln_b': out['v_mem_ln_b'], 'v_p_a': out['v_p_a'], 'v_p_b': out['v_p_b'], 'v_p_c': out['v_p_c'], 'v_p_m': out['v_p_m'], 'v_w_out': out['v_w_out'], 'v_ln_g': out['v_ln_g'], 'v_ln_b': out['v_ln_b']}


def _loss(weights, diff, rest, loss_target):
    with _jax.named_scope("forward"):
        args = {**rest, TWIN_DIFF_INPUT: diff, **{k: w.astype(_WEIGHT_DTYPES[k]) for k, w in weights.items()}}
        y = _forward(args)
    with _jax.named_scope("loss_head"):
        err = _jnp.square(y.astype(_jnp.float32) - loss_target)
        return 0.5 * _jnp.sum(_jnp.mean(err, axis=-1)) if err.ndim else 0.5 * err


def _adamw(w, g, m, v):
    m = ADAM_B1 * m + (1.0 - ADAM_B1) * g
    v = ADAM_B2 * v + (1.0 - ADAM_B2) * _jnp.square(g)
    m_hat = m / (1.0 - ADAM_B1 ** ADAM_STEP)
    v_hat = v / (1.0 - ADAM_B2 ** ADAM_STEP)
    delta = -ADAM_LR * (m_hat / (_jnp.sqrt(v_hat) + ADAM_EPS) + ADAM_WD * w)
    return delta, m, v


def reference(x, mem, w_in, b_forget, conv_a_w, conv_a_b, ln_a_g, ln_a_b, conv_b_w, w_kv_mem, mem_ln_g, mem_ln_b, p_a, p_b, p_c, p_m, w_out, ln_g, ln_b, loss_target, m_w_in, m_b_forget, m_conv_a_w, m_conv_a_b, m_ln_a_g, m_ln_a_b, m_conv_b_w, m_w_kv_mem, m_mem_ln_g, m_mem_ln_b, m_p_a, m_p_b, m_p_c, m_p_m, m_w_out, m_ln_g, m_ln_b, v_w_in, v_b_forget, v_conv_a_w, v_conv_a_b, v_ln_a_g, v_ln_a_b, v_conv_b_w, v_w_kv_mem, v_mem_ln_g, v_mem_ln_b, v_p_a, v_p_b, v_p_c, v_p_m, v_w_out, v_ln_g, v_ln_b):
    given = dict(x=x, mem=mem, w_in=w_in, b_forget=b_forget, conv_a_w=conv_a_w, conv_a_b=conv_a_b, ln_a_g=ln_a_g, ln_a_b=ln_a_b, conv_b_w=conv_b_w, w_kv_mem=w_kv_mem, mem_ln_g=mem_ln_g, mem_ln_b=mem_ln_b, p_a=p_a, p_b=p_b, p_c=p_c, p_m=p_m, w_out=w_out, ln_g=ln_g, ln_b=ln_b, loss_target=loss_target, m_w_in=m_w_in, m_b_forget=m_b_forget, m_conv_a_w=m_conv_a_w, m_conv_a_b=m_conv_a_b, m_ln_a_g=m_ln_a_g, m_ln_a_b=m_ln_a_b, m_conv_b_w=m_conv_b_w, m_w_kv_mem=m_w_kv_mem, m_mem_ln_g=m_mem_ln_g, m_mem_ln_b=m_mem_ln_b, m_p_a=m_p_a, m_p_b=m_p_b, m_p_c=m_p_c, m_p_m=m_p_m, m_w_out=m_w_out, m_ln_g=m_ln_g, m_ln_b=m_ln_b, v_w_in=v_w_in, v_b_forget=v_b_forget, v_conv_a_w=v_conv_a_w, v_conv_a_b=v_conv_a_b, v_ln_a_g=v_ln_a_g, v_ln_a_b=v_ln_a_b, v_conv_b_w=v_conv_b_w, v_w_kv_mem=v_w_kv_mem, v_mem_ln_g=v_mem_ln_g, v_mem_ln_b=v_mem_ln_b, v_p_a=v_p_a, v_p_b=v_p_b, v_p_c=v_p_c, v_p_m=v_p_m, v_w_out=v_w_out, v_ln_g=v_ln_g, v_ln_b=v_ln_b)
    weights = {n: given[n] for n in TWIN_WEIGHTS}
    shared = {n: given[n] for n in SHARED_INPUTS}
    per_example = {n: given[n] for n in ['x', 'mem']}
    grad_fn = _jax.value_and_grad(_loss, argnums=(0, 1))

    def one_microbatch(ex, loss_target):
        ex = dict(ex)
        diff = ex.pop(TWIN_DIFF_INPUT)
        return grad_fn(weights, diff, {**shared, **ex}, loss_target)

    if N_MICROBATCH == 1:
        loss, (grad_w, grad_x) = one_microbatch(per_example, given["loss_target"])
    else:
        def body(carry, xs):
            loss_sum, grad_sum = carry
            l_k, (gw_k, gx_k) = one_microbatch(xs[0], xs[1])
            with _jax.named_scope("update"):
                return (loss_sum + l_k, _jax.tree.map(_jnp.add, grad_sum, gw_k)), gx_k

        init = (_jnp.zeros((), _jnp.float32), _jax.tree.map(_jnp.zeros_like, weights))
        (loss, grad_w), grad_x = _jax.lax.scan(body, init, (per_example, given["loss_target"]))
    with _jax.named_scope("update"):
        delta_w, new_m, new_v = {}, {}, {}
        for n in TWIN_WEIGHTS:
            delta_w[n], new_m[n], new_v[n] = _adamw(weights[n], grad_w[n], given["m_" + n], given["v_" + n])
    return (loss, grad_x, *[grad_w[n] for n in TWIN_WEIGHTS], *[delta_w[n] for n in TWIN_WEIGHTS],
            *[new_m[n] for n in TWIN_WEIGHTS], *[new_v[n] for n in TWIN_WEIGHTS])
```

```python
import jax
import jax.numpy as jnp
from jax import lax
from jax.experimental import pallas as pl
from jax.experimental.pallas import tpu as pltpu

F32 = jnp.float32
BF = jnp.bfloat16
HIGHEST = lax.Precision.HIGHEST

D = 1024
NL = 4
NDEV = 8
HC, DH = 8, 64
HM = 4
ML = 256
KA, KB = 31, 3
HALO_A, HALO_B = 32, 8
ALPHA = (2.0 * NL) ** 0.25
EPS = 1e-5
SCALE = DH ** -0.5
NEG = -1e30

ADAM_LR, ADAM_B1, ADAM_B2, ADAM_EPS, ADAM_WD, ADAM_STEP = 0.001, 0.9, 0.999, 1e-08, 0.01, 10

C_G = 0
C_AB = 4096
C_Q = 8192
C_M = 8704
C_KV = 9216
C_F = 10240
NP = 10368
_RUNS = ((6152, 10248), (0, 3584), (5128, 5640), (3584, 4096), (5640, 6152), (4096, 5120), (5120, 5128))
IN_COLS = 10248
SHARD_IN = IN_COLS // NDEV

VMEM_LIMIT = 56 * 1024 * 1024

NT_DIMS = (((1,), (1,)), ((), ()))
TN_DIMS = (((0,), (0,)), ((), ()))


def _cp(sem=None):
    return pltpu.CompilerParams(dimension_semantics=sem, vmem_limit_bytes=VMEM_LIMIT)


def _sig(x):
    return 1.0 / (1.0 + jnp.exp(-x))


def _dsilu(x, s):
    return s * (1.0 + x * (1.0 - s))


def _mean_l(x):
    return jnp.mean(x, axis=-1, keepdims=True)


def _sum_r(x):
    return jnp.sum(x, axis=0, keepdims=True)


def _ln_hat(x):
    mu = _mean_l(x)
    xc = x - mu
    rstd = lax.rsqrt(_mean_l(xc * xc) + EPS)
    return xc * rstd, rstd


def _ln_bwd(dxh, xh, rstd):
    return rstd * (dxh - _mean_l(dxh) - xh * _mean_l(dxh * xh))


def _dot(a, b, dims=None):
    if dims is None:
        return jnp.dot(a, b, preferred_element_type=F32)
    return lax.dot_general(a, b, dims, preferred_element_type=F32)


def _lane_pack(cols, rows):
    lane = lax.broadcasted_iota(jnp.int32, (rows, 128), 1)
    out = jnp.zeros((rows, 128), F32)
    for h, c in enumerate(cols):
        out = jnp.where(lane == h, c, out)
    return out


def _mm(a, b, *, name, nt=False, out_dtype=F32, tm=512, tn=512, tk=512, add=None, add_scale=1.0):
    m, kdim = a.shape
    n = b.shape[0] if nt else b.shape[1]
    tm, tn, tk = min(tm, m), min(tn, n), min(tk, kdim)
    assert m % tm == 0 and n % tn == 0 and kdim % tk == 0, (name, a.shape, b.shape, tm, tn, tk)
    nk = kdim // tk

    def body(*refs):
        if add is None:
            a_ref, b_ref, o_ref, acc_ref = refs
        else:
            a_ref, b_ref, add_ref, o_ref, acc_ref = refs
        k = pl.program_id(2)

        @pl.when(k == 0)
        def _():
            acc_ref[...] = jnp.zeros_like(acc_ref)

        acc_ref[...] += _dot(a_ref[...].astype(BF), b_ref[...].astype(BF), NT_DIMS if nt else None)

        @pl.when(k == nk - 1)
        def _():
            r = acc_ref[...]
            if add is not None:
                r = r + add_scale * add_ref[...]
            o_ref[...] = r.astype(out_dtype)

    in_specs = [pl.BlockSpec((tm, tk), lambda i, j, k: (i, k)),
                pl.BlockSpec((tn, tk), lambda i, j, k: (j, k)) if nt else pl.BlockSpec((tk, tn), lambda i, j, k: (k, j))]
    args = [a, b]
    if add is not None:
        in_specs.append(pl.BlockSpec((tm, tn), lambda i, j, k: (i, j)))
        args.append(add)
    return pl.pallas_call(
        body, name=name, grid=(m // tm, n // tn, nk),
        out_shape=jax.ShapeDtypeStruct((m, n), out_dtype),
        in_specs=in_specs, out_specs=pl.BlockSpec((tm, tn), lambda i, j, k: (i, j)),
        scratch_shapes=[pltpu.VMEM((tm, tn), F32)],
        compiler_params=_cp(("parallel", "parallel", "arbitrary")),
    )(*args)


def _mem_ln_fwd(mem, g, b):
    def body(m_ref, g_ref, b_ref, n_ref, h_ref):
        xh, _ = _ln_hat(m_ref[...])
        h_ref[...] = xh
        n_ref[...] = xh * g_ref[...] + b_ref[...]

    shp = jax.ShapeDtypeStruct(mem.shape, F32)
    return pl.pallas_call(body, name="mem_ln_fwd", out_shape=(shp, shp), compiler_params=_cp())(mem, g, b)


def _mem_ln_bwd(dns, mhat):
    def body(*refs):
        d_refs, h_ref, dg_ref, db_ref = refs[:NL], refs[NL], refs[NL + 1], refs[NL + 2]
        dn = d_refs[0][...]
        for r in d_refs[1:]:
            dn = dn + r[...]
        dg_ref[...] = _sum_r(dn * h_ref[...])
        db_ref[...] = _sum_r(dn)

    shp = jax.ShapeDtypeStruct((1, D), F32)
    return pl.pallas_call(body, name="mem_ln_bwd", out_shape=(shp, shp), compiler_params=_cp())(*dns, mhat)


def _pre_fwd(P, wA, bA, gA, betaA, wB, bfg, *, ts):
    S = P.shape[0]
    nt = S // ts
    cb = C_AB // 512

    def cur(j):
        return pl.BlockSpec((ts, 512), lambda i, j=j: (i, cb + j))

    def halo(j, rows):
        return pl.BlockSpec((rows, 512), lambda i, j=j: (jnp.maximum(i * (ts // rows) - 1, 0), cb + j))

    def full(shape):
        return pl.BlockSpec(shape, lambda i: (0, 0))

    def body(au, av, ag, bh, bb, bc, bg, f_ref, au_h, av_h, bh_h, bc_h, wA_r, bA_r, gA_r, betaA_r, wB_r, bf_r,
             za_o, zb_o, u_o, ca_o, vb_o, cb_o, cum_o, winA, winB, carry):
        i = pl.program_id(0)
        nz = (i > 0).astype(F32)

        u = au[...] * _sig(av[...])
        winA[0:HALO_A, :] = au_h[...] * _sig(av_h[...]) * nz
        winA[HALO_A:, :] = u
        acc = jnp.zeros((ts, 512), F32)
        for k in range(KA):
            acc = acc + winA[pl.ds(HALO_A - KA + 1 + k, ts), :] * wA_r[k:k + 1, :]
        ca = acc + bA_r[...]
        xh, _ = _ln_hat(ca)
        n = xh * gA_r[...] + betaA_r[...]
        a = n * _sig(n)
        agv = ag[...]
        za_o[...] = (a * agv * _sig(agv)).astype(BF)
        u_o[...] = u
        ca_o[...] = ca

        vb = bc[...] * bh[...]
        winB[0:HALO_B, :] = bc_h[...] * bh_h[...] * nz
        winB[HALO_B:, :] = vb
        accb = jnp.zeros((ts, 512), F32)
        for k in range(KB):
            accb = accb + winB[pl.ds(HALO_B - KB + 1 + k, ts), :] * wB_r[k:k + 1, :]
        bgv = bg[...]
        zb_o[...] = (bb[...] * accb * bgv * _sig(bgv)).astype(BF)
        vb_o[...] = vb
        cb_o[...] = accb

        @pl.when(i == 0)
        def _():
            carry[...] = jnp.zeros_like(carry)

        x = f_ref[...] + bf_r[...]
        logf = jnp.minimum(x, 0.0) - jnp.log1p(jnp.exp(-jnp.abs(x)))
        r = lax.broadcasted_iota(jnp.int32, (ts, ts), 0)
        c = lax.broadcasted_iota(jnp.int32, (ts, ts), 1)
        tri = (r >= c).astype(F32)
        cum = jnp.dot(tri, logf, precision=HIGHEST, preferred_element_type=F32) + carry[...]
        cum_o[...] = cum
        carry[...] = cum[ts - 1:ts, :]

    s512 = lambda dt: jax.ShapeDtypeStruct((S, 512), dt)
    o512 = pl.BlockSpec((ts, 512), lambda i: (i, 0))
    return pl.pallas_call(
        body, name="pre_fwd", grid=(nt,),
        out_shape=(s512(BF), s512(BF), s512(F32), s512(F32), s512(F32), s512(F32), jax.ShapeDtypeStruct((S, 128), F32)),
        in_specs=[cur(0), cur(1), cur(2), cur(3), cur(4), cur(5), cur(6),
                  pl.BlockSpec((ts, 128), lambda i: (i, C_F // 128)),
                  halo(0, HALO_A), halo(1, HALO_A), halo(3, HALO_B), halo(5, HALO_B),
                  full((32, 512)), full((1, 512)), full((1, 512)), full((1, 512)), full((8, 512)), full((1, 128))],
        out_specs=(o512, o512, o512, o512, o512, o512, pl.BlockSpec((ts, 128), lambda i: (i, 0))),
        scratch_shapes=[pltpu.VMEM((ts + HALO_A, 512), F32), pltpu.VMEM((ts + HALO_B, 512), F32), pltpu.VMEM((1, 128), F32)],
        compiler_params=_cp(("arbitrary",)),
    )(P, P, P, P, P, P, P, P, P, P, P, P, wA, bA, gA, betaA, wB, bfg)


def _scores(q_ref, k_ref, cq_ref, ck_ref, h, masked, tq):
    sl = slice(DH * h, DH * (h + 1))
    s = _dot(q_ref[:, sl].astype(BF), k_ref[:, sl].astype(BF), NT_DIMS) * SCALE
    s = s + (cq_ref[:, h:h + 1] - ck_ref[h:h + 1, :])
    if masked:
        r = lax.broadcasted_iota(jnp.int32, (tq, tq), 0)
        c = lax.broadcasted_iota(jnp.int32, (tq, tq), 1)
        s = jnp.where(r >= c, s, NEG)
    return s


def _attn_fwd(P, cum, cumT, *, tq):
    S = P.shape[0]
    nq = S // tq

    def body(q_ref, k_ref, v_ref, cg_ref, cq_ref, ck_ref, o_ref, zc_ref, lse_ref, m_s, l_s, acc_s):
        i, j = pl.program_id(0), pl.program_id(1)

        @pl.when(j == 0)
        def _():
            m_s[...] = jnp.full_like(m_s, NEG)
            l_s[...] = jnp.zeros_like(l_s)
            acc_s[...] = jnp.zeros_like(acc_s)

        def step(masked):
            for h in range(HC):
                sl = slice(DH * h, DH * (h + 1))
                s = _scores(q_ref, k_ref, cq_ref, ck_ref, h, masked, tq)
                m_prev = m_s[h]
                m_new = jnp.maximum(m_prev, jnp.max(s, axis=1, keepdims=True))
                p = jnp.exp(s - m_new)
                alpha = jnp.exp(m_prev - m_new)
                l_s[h] = alpha * l_s[h] + jnp.sum(p, axis=1, keepdims=True)
                acc_s[:, sl] = alpha * acc_s[:, sl] + _dot(p.astype(BF), v_ref[:, sl].astype(BF))
                m_s[h] = m_new

        @pl.when(j < i)
        def _():
            step(False)

        @pl.when(j == i)
        def _():
            step(True)
            lses = []
            for h in range(HC):
                sl = slice(DH * h, DH * (h + 1))
                l = l_s[h]
                o_ref[:, sl] = acc_s[:, sl] / l
                lses.append(m_s[h] + jnp.log(l))
            lse_ref[...] = _lane_pack(lses, tq)
            cg = cg_ref[...]
            zc_ref[...] = (o_ref[...] * cg * _sig(cg)).astype(BF)

    def qspec(col):
        return pl.BlockSpec((tq, 512), lambda i, j: (i, col))

    def kspec(col):
        return pl.BlockSpec((tq, 512), lambda i, j: (jnp.minimum(i, j), col))

    return pl.pallas_call(
        body, name="attn_fwd", grid=(nq, nq),
        out_shape=(jax.ShapeDtypeStruct((S, 512), F32), jax.ShapeDtypeStruct((S, 512), BF), jax.ShapeDtypeStruct((S, 128), F32)),
        in_specs=[qspec(C_Q // 512), kspec(C_KV // 512), kspec(C_KV // 512 + 1), qspec(C_AB // 512 + 7),
                  pl.BlockSpec((tq, 128), lambda i, j: (i, 0)),
                  pl.BlockSpec((8, tq), lambda i, j: (0, jnp.minimum(i, j)))],
        out_specs=(pl.BlockSpec((tq, 512), lambda i, j: (i, 0)), pl.BlockSpec((tq, 512), lambda i, j: (i, 0)),
                   pl.BlockSpec((tq, 128), lambda i, j: (i, 0))),
        scratch_shapes=[pltpu.VMEM((HC, tq, 1), F32), pltpu.VMEM((HC, tq, 1), F32), pltpu.VMEM((tq, 512), F32)],
        compiler_params=_cp(("parallel", "arbitrary")),
    )(P, P, P, P, cum, cumT)


def _attn_bwd_dkv(P, do, lse, dlt, cum, cumT, dP, *, tq):
    S = P.shape[0]
    nq = S // tq

    def body(q_ref, k_ref, v_ref, do_ref, lse_ref, dl_ref, cq_ref, ck_ref, dp_in, dkv_o, dck_o, dk_s, dv_s, dck_s):
        del dp_in
        j, i = pl.program_id(0), pl.program_id(1)

        @pl.when(i == 0)
        def _():
            dk_s[...] = jnp.zeros_like(dk_s)
            dv_s[...] = jnp.zeros_like(dv_s)
            dck_s[...] = jnp.zeros_like(dck_s)

        def step(masked):
            for h in range(HC):
                sl = slice(DH * h, DH * (h + 1))
                s = _scores(q_ref, k_ref, cq_ref, ck_ref, h, masked, tq)
                p = jnp.exp(s - lse_ref[:, h:h + 1])
                dob = do_ref[:, sl].astype(BF)
                dpv = _dot(dob, v_ref[:, sl].astype(BF), NT_DIMS)
                ds = p * (dpv - dl_ref[:, h:h + 1])
                dv_s[:, sl] += _dot(p.astype(BF), dob, TN_DIMS)
                dk_s[:, sl] += _dot(ds.astype(BF), q_ref[:, sl].astype(BF), TN_DIMS) * SCALE
                dck_s[h:h + 1, :] += _sum_r(ds)

        @pl.when(i > j)
        def _():
            step(False)

        @pl.when(i == j)
        def _():
            step(True)

        @pl.when(i == nq - 1)
        def _():
            dkv_o[:, 0:512] = dk_s[...].astype(BF)
            dkv_o[:, 512:1024] = dv_s[...].astype(BF)
            dck_o[...] = dck_s[...]

    def qspec(w, col):
        return pl.BlockSpec((tq, w), lambda j, i: (jnp.maximum(i, j), col))

    def kspec(col):
        return pl.BlockSpec((tq, 512), lambda j, i: (j, col))

    return pl.pallas_call(
        body, name="attn_bwd_dkv", grid=(nq, nq),
        out_shape=(jax.ShapeDtypeStruct(dP.shape, BF), jax.ShapeDtypeStruct((8, S), F32)),
        in_specs=[qspec(512, C_Q // 512), kspec(C_KV // 512), kspec(C_KV // 512 + 1), qspec(512, 0),
                  qspec(128, 0), qspec(128, 0), qspec(128, 0),
                  pl.BlockSpec((8, tq), lambda j, i: (0, j)),
                  pl.BlockSpec(memory_space=pl.ANY)],
        out_specs=(pl.BlockSpec((tq, 1024), lambda j, i: (j, C_KV // 1024)), pl.BlockSpec((8, tq), lambda j, i: (0, j))),
        scratch_shapes=[pltpu.VMEM((tq, 512), F32), pltpu.VMEM((tq, 512), F32), pltpu.VMEM((8, tq), F32)],
        input_output_aliases={8: 0},
        compiler_params=_cp(("parallel", "arbitrary")),
    )(P, P, P, do, lse, dlt, cum, cumT, dP)


def _attn_bwd_dq(P, do, lse, dlt, cum, cumT, dP, *, tq):
    S = P.shape[0]
    nq = S // tq

    def body(q_ref, k_ref, v_ref, do_ref, lse_ref, dl_ref, cq_ref, ck_ref, dp_in, dq_o, dcq_o, dq_s, dcq_s):
        del dp_in
        i, j = pl.program_id(0), pl.program_id(1)

        @pl.when(j == 0)
        def _():
            dq_s[...] = jnp.zeros_like(dq_s)
            dcq_s[...] = jnp.zeros_like(dcq_s)

        def step(masked):
            for h in range(HC):
                sl = slice(DH * h, DH * (h + 1))
                s = _scores(q_ref, k_ref, cq_ref, ck_ref, h, masked, tq)
                p = jnp.exp(s - lse_ref[:, h:h + 1])
                dpv = _dot(do_ref[:, sl].astype(BF), v_ref[:, sl].astype(BF), NT_DIMS)
                ds = p * (dpv - dl_ref[:, h:h + 1])
                dq_s[:, sl] += _dot(ds.astype(BF), k_ref[:, sl].astype(BF)) * SCALE
                dcq_s[h] += jnp.sum(ds, axis=1, keepdims=True)

        @pl.when(j < i)
        def _():
            step(False)

        @pl.when(j == i)
        def _():
            step(True)
            dq_o[...] = dq_s[...].astype(BF)
            dcq_o[...] = _lane_pack([dcq_s[h] for h in range(HC)], tq)

    def qspec(w, col):
        return pl.BlockSpec((tq, w), lambda i, j: (i, col))

    def kspec(col):
        return pl.BlockSpec((tq, 512), lambda i, j: (jnp.minimum(i, j), col))

    return pl.pallas_call(
        body, name="attn_bwd_dq", grid=(nq, nq),
        out_shape=(jax.ShapeDtypeStruct(dP.shape, BF), jax.ShapeDtypeStruct((S, 128), F32)),
        in_specs=[qspec(512, C_Q // 512), kspec(C_KV // 512), kspec(C_KV // 512 + 1), qspec(512, 0),
                  qspec(128, 0), qspec(128, 0), qspec(128, 0),
                  pl.BlockSpec((8, tq), lambda i, j: (0, jnp.minimum(i, j))),
                  pl.BlockSpec(memory_space=pl.ANY)],
        out_specs=(pl.BlockSpec((tq, 512), lambda i, j: (i, C_Q // 512)), pl.BlockSpec((tq, 128), lambda i, j: (i, 0))),
        scratch_shapes=[pltpu.VMEM((tq, 512), F32), pltpu.VMEM((HC, tq, 1), F32)],
        input_output_aliases={8: 0},
        compiler_params=_cp(("parallel", "arbitrary")),
    )(P, P, P, do, lse, dlt, cum, cumT, dP)


def _xattn_probs(qm_ref, kv_ref, h):
    sl = slice(DH * h, DH * (h + 1))
    s = _dot(qm_ref[:, sl].astype(BF), kv_ref[:, sl].astype(BF), NT_DIMS) * SCALE
    p = jnp.exp(s - jnp.max(s, axis=1, keepdims=True))
    return p / jnp.sum(p, axis=1, keepdims=True)


def _xattn_fwd(P, kv, *, ts):
    S = P.shape[0]

    def body(qm_ref, kv_ref, zm_o, o_s):
        for h in range(HM):
            sl = slice(DH * h, DH * (h + 1))
            p = _xattn_probs(qm_ref, kv_ref, h)
            o_s[:, sl] = _dot(p.astype(BF), kv_ref[:, ML + DH * h:ML + DH * (h + 1)].astype(BF))
        mg = qm_ref[:, 256:512]
        zm_o[...] = (o_s[...] * mg * _sig(mg)).astype(BF)

    return pl.pallas_call(
        body, name="xattn_fwd", grid=(S // ts,),
        out_shape=jax.ShapeDtypeStruct((S, 256), BF),
        in_specs=[pl.BlockSpec((ts, 512), lambda i: (i, C_M // 512)), pl.BlockSpec((ML, 512), lambda i: (0, 0))],
        out_specs=pl.BlockSpec((ts, 256), lambda i: (i, 0)),
        scratch_shapes=[pltpu.VMEM((ts, 256), F32)],
        compiler_params=_cp(("parallel",)),
    )(P, kv)


def _xattn_bwd(P, kv, dzm, dP, *, ts):
    S = P.shape[0]

    def body(qm_ref, kv_ref, dz_ref, dp_in, dqm_o, dkv_o):
        del dp_in
        i = pl.program_id(0)

        @pl.when(i == 0)
        def _():
            dkv_o[...] = jnp.zeros_like(dkv_o)

        mg = qm_ref[:, 256:512]
        sg = _sig(mg)
        for h in range(HM):
            sl = slice(DH * h, DH * (h + 1))
            vsl = slice(ML + DH * h, ML + DH * (h + 1))
            p = _xattn_probs(qm_ref, kv_ref, h)
            pb = p.astype(BF)
            vh = kv_ref[:, vsl].astype(BF)
            o = _dot(pb, vh)
            dz = dz_ref[:, sl]
            do = dz * mg[:, sl] * sg[:, sl]
            dqm_o[:, 256 + DH * h:256 + DH * (h + 1)] = (dz * o * _dsilu(mg[:, sl], sg[:, sl])).astype(BF)
            dob = do.astype(BF)
            dpv = _dot(dob, vh, NT_DIMS)
            ds = p * (dpv - jnp.sum(do * o, axis=1, keepdims=True))
            dsb = ds.astype(BF)
            dqm_o[:, sl] = (_dot(dsb, kv_ref[:, sl].astype(BF)) * SCALE).astype(BF)
            dkv_o[:, sl] += _dot(dsb, qm_ref[:, sl].astype(BF), TN_DIMS) * SCALE
            dkv_o[:, vsl] += _dot(pb, dob, TN_DIMS)

    return pl.pallas_call(
        body, name="xattn_bwd", grid=(S // ts,),
        out_shape=(jax.ShapeDtypeStruct(dP.shape, BF), jax.ShapeDtypeStruct((ML, 512), F32)),
        in_specs=[pl.BlockSpec((ts, 512), lambda i: (i, C_M // 512)), pl.BlockSpec((ML, 512), lambda i: (0, 0)),
                  pl.BlockSpec((ts, 256), lambda i: (i, 0)), pl.BlockSpec(memory_space=pl.ANY)],
        out_specs=(pl.BlockSpec((ts, 512), lambda i: (i, C_M // 512)), pl.BlockSpec((ML, 512), lambda i: (0, 0))),
        input_output_aliases={3: 0},
        compiler_params=_cp(("arbitrary",)),
    )(P, kv, dzm, dP)


def _merge_fwd(za, zb, zc, zm, P, x, pa, pb, pc, pm, wo, lng, lnb, *, ts):
    S = x.shape[0]

    def body(za_r, zb_r, zc_r, zm_r, g_r, x_r, pa_r, pb_r, pc_r, pm_r, wo_r, lng_r, lnb_r,
             mg_o, ya_o, yb_o, yc_o, ym_o, xn_o, xh_o, rs_o):
        merged = jnp.zeros((ts, D), F32)
        for t, (z_r, p_r, y_o) in enumerate(((za_r, pa_r, ya_o), (zb_r, pb_r, yb_o), (zc_r, pc_r, yc_o), (zm_r, pm_r, ym_o))):
            y = _dot(z_r[...], p_r[...])
            merged = merged + _sig(g_r[:, D * t:D * (t + 1)]) * y
            y_o[...] = y.astype(BF)
        mb = merged.astype(BF)
        mg_o[...] = mb
        r = ALPHA * x_r[...] + _dot(mb, wo_r[...])
        xh, rstd = _ln_hat(r)
        xh_o[...] = xh
        rs_o[...] = rstd
        xn_o[...] = xh * lng_r[...] + lnb_r[...]

    def rows(w):
        return pl.BlockSpec((ts, w), lambda i: (i, 0))

    def full(a):
        return pl.BlockSpec(a.shape, lambda i: (0, 0))

    sd = lambda dt: jax.ShapeDtypeStruct((S, D), dt)
    return pl.pallas_call(
        body, name="merge_fwd", grid=(S // ts,),
        out_shape=(sd(BF), sd(BF), sd(BF), sd(BF), sd(BF), sd(F32), sd(F32), jax.ShapeDtypeStruct((S, 1), F32)),
        in_specs=[rows(512), rows(512), rows(512), rows(256), pl.BlockSpec((ts, 4 * D), lambda i: (i, 0)), rows(D),
                  full(pa), full(pb), full(pc), full(pm), full(wo), full(lng), full(lnb)],
        out_specs=(rows(D),) * 7 + (rows(1),),
        compiler_params=_cp(("parallel",)),
    )(za, zb, zc, zm, P, x, pa, pb, pc, pm, wo, lng, lnb)


def _loss_fwd(y, tgt, *, ts):
    S = y.shape[0]

    def body(y_r, t_r, dy_o, l_o):
        @pl.when(pl.program_id(0) == 0)
        def _():
            l_o[...] = jnp.zeros_like(l_o)

        e = y_r[...] - t_r[...]
        dy_o[...] = e / D
        l_o[...] += 0.5 * jnp.sum(_sum_r(e * e), axis=1, keepdims=True) / D

    rows = pl.BlockSpec((ts, D), lambda i: (i, 0))
    return pl.pallas_call(
        body, name="loss", grid=(S // ts,),
        out_shape=(jax.ShapeDtypeStruct((S, D), F32), jax.ShapeDtypeStruct((1, 1), F32)),
        in_specs=[rows, rows], out_specs=(rows, pl.BlockSpec((1, 1), lambda i: (0, 0))),
        compiler_params=_cp(("arbitrary",)),
    )(y, tgt)


def _out_bwd(dxn, xh, rstd, merged, wo, lng, *, ts):
    S = dxn.shape[0]

    def body(dxn_r, xh_r, rs_r, mg_r, wo_r, lng_r, dr_o, dm_o, dwo_o, dlng_o, dlnb_o):
        @pl.when(pl.program_id(0) == 0)
        def _():
            dwo_o[...] = jnp.zeros_like(dwo_o)
            dlng_o[...] = jnp.zeros_like(dlng_o)
            dlnb_o[...] = jnp.zeros_like(dlnb_o)

        dxn = dxn_r[...]
        xh = xh_r[...]
        dr = _ln_bwd(dxn * lng_r[...], xh, rs_r[...])
        dr_o[...] = dr
        drb = dr.astype(BF)
        dm_o[...] = _dot(drb, wo_r[...], NT_DIMS)
        dwo_o[...] += _dot(mg_r[...], drb, TN_DIMS)
        dlng_o[...] += _sum_r(dxn * xh)
        dlnb_o[...] += _sum_r(dxn)

    rows = pl.BlockSpec((ts, D), lambda i: (i, 0))
    full = lambda shape: pl.BlockSpec(shape, lambda i: (0, 0))
    sd = jax.ShapeDtypeStruct((S, D), F32)
    vec = jax.ShapeDtypeStruct((1, D), F32)
    return pl.pallas_call(
        body, name="out_bwd", grid=(S // ts,),
        out_shape=(sd, sd, jax.ShapeDtypeStruct((D, D), F32), vec, vec),
        in_specs=[rows, rows, pl.BlockSpec((ts, 1), lambda i: (i, 0)), rows, full((D, D)), full((1, D))],
        out_specs=(rows, rows, full((D, D)), full((1, D)), full((1, D))),
        compiler_params=_cp(("arbitrary",)),
    )(dxn, xh, rstd, merged, wo, lng)


def _merge_bwd(dm, P, ya, yb, yc, ym, za, zb, zc, zm, pa, pb, pc, pm, *, ts):
    S = dm.shape[0]

    def body(dm_r, g_r, ya_r, yb_r, yc_r, ym_r, za_r, zb_r, zc_r, zm_r, pa_r, pb_r, pc_r, pm_r,
             dg_o, dza_o, dzb_o, dzc_o, dzm_o, dpa_o, dpb_o, dpc_o, dpm_o):
        @pl.when(pl.program_id(0) == 0)
        def _():
            for o in (dpa_o, dpb_o, dpc_o, dpm_o):
                o[...] = jnp.zeros_like(o)

        dm = dm_r[...]
        for t, (y_r, z_r, p_r, dz_o, dp_o) in enumerate(((ya_r, za_r, pa_r, dza_o, dpa_o), (yb_r, zb_r, pb_r, dzb_o, dpb_o),
                                                        (yc_r, zc_r, pc_r, dzc_o, dpc_o), (ym_r, zm_r, pm_r, dzm_o, dpm_o))):
            gate = _sig(g_r[:, D * t:D * (t + 1)])
            dg_o[:, D * t:D * (t + 1)] = (dm * y_r[...].astype(F32) * gate * (1.0 - gate)).astype(BF)
            dyb = (dm * gate).astype(BF)
            dz_o[...] = _dot(dyb, p_r[...], NT_DIMS)
            dp_o[...] += _dot(z_r[...], dyb, TN_DIMS)

    def rows(w):
        return pl.BlockSpec((ts, w), lambda i: (i, 0))

    def full(a):
        return pl.BlockSpec(a.shape, lambda i: (0, 0))

    return pl.pallas_call(
        body, name="merge_bwd", grid=(S // ts,),
        out_shape=(jax.ShapeDtypeStruct((S, NP), BF),
                   jax.ShapeDtypeStruct((S, 512), F32), jax.ShapeDtypeStruct((S, 512), F32),
                   jax.ShapeDtypeStruct((S, 512), F32), jax.ShapeDtypeStruct((S, 256), F32),
                   jax.ShapeDtypeStruct(pa.shape, F32), jax.ShapeDtypeStruct(pb.shape, F32),
                   jax.ShapeDtypeStruct(pc.shape, F32), jax.ShapeDtypeStruct(pm.shape, F32)),
        in_specs=[rows(D), pl.BlockSpec((ts, 4 * D), lambda i: (i, 0)), rows(D), rows(D), rows(D), rows(D),
                  rows(512), rows(512), rows(512), rows(256), full(pa), full(pb), full(pc), full(pm)],
        out_specs=(pl.BlockSpec((ts, 4 * D), lambda i: (i, 0)), rows(512), rows(512), rows(512), rows(256),
                   full(pa), full(pb), full(pc), full(pm)),
        compiler_params=_cp(("arbitrary",)),
    )(dm, P, ya, yb, yc, ym, za, zb, zc, zm, pa, pb, pc, pm)


def _branch_bwd(P, ca, cb, u, vb, dza, dzb, dzc, oc, wA, gA, betaA, wB, dP, *, ts):
    S = P.shape[0]
    nt = S // ts

    def rev(i):
        return nt - 1 - i

    def rows(w):
        return pl.BlockSpec((ts, w), lambda i: (rev(i), 0))

    def halo(rows_):
        return pl.BlockSpec((rows_, 512), lambda i: (jnp.maximum(rev(i) * (ts // rows_) - 1, 0), 0))

    def full(shape):
        return pl.BlockSpec(shape, lambda i: (0, 0))

    def body(pg, ca_r, cb_r, u_r, vb_r, uh_r, vh_r, dza_r, dzb_r, dzc_r, oc_r, wA_r, gA_r, betaA_r, wB_r, dp_in,
             dpg_o, do_o, dl_o, dwA_o, dbA_o, dgA_o, dbetaA_o, dwB_o, dwinA, uwin, haloA, dwinB, vwin, haloB):
        del dp_in
        i = pl.program_id(0)
        nz = (rev(i) > 0).astype(F32)

        @pl.when(i == 0)
        def _():
            for o in (dwA_o, dbA_o, dgA_o, dbetaA_o, dwB_o, haloA, haloB):
                o[...] = jnp.zeros_like(o)

        def col(j):
            return pg[:, 512 * j:512 * (j + 1)]

        def put(j, val):
            dpg_o[:, 512 * j:512 * (j + 1)] = val.astype(BF)

        a_gate = col(2)
        xh, rstd = _ln_hat(ca_r[...])
        gA_v = gA_r[...]
        n = xh * gA_v + betaA_r[...]
        sn = _sig(n)
        a = n * sn
        sg = _sig(a_gate)
        dza = dza_r[...]
        put(2, dza * a * _dsilu(a_gate, sg))
        dn = dza * a_gate * sg * _dsilu(n, sn)
        dgA_o[...] += _sum_r(dn * xh)
        dbetaA_o[...] += _sum_r(dn)
        dca = _ln_bwd(dn * gA_v, xh, rstd)
        dbA_o[...] += _sum_r(dca)
        dwinA[0:ts, :] = dca
        dwinA[ts:, :] = haloA[...]
        haloA[...] = dca[0:HALO_A, :]
        uwin[0:HALO_A, :] = uh_r[...] * nz
        uwin[HALO_A:, :] = u_r[...]
        du = jnp.zeros((ts, 512), F32)
        for k in range(KA):
            du = du + dwinA[pl.ds(KA - 1 - k, ts), :] * wA_r[k:k + 1, :]
            dwA_o[k:k + 1, :] += _sum_r(dca * uwin[pl.ds(HALO_A - KA + 1 + k, ts), :])
        sv = _sig(col(1))
        put(0, du * sv)
        put(1, du * col(0) * sv * (1.0 - sv))

        b_gate = col(6)
        sgb = _sig(b_gate)
        cbv = cb_r[...]
        b_b = col(4)
        dzb = dzb_r[...]
        put(6, dzb * b_b * cbv * _dsilu(b_gate, sgb))
        dhb = dzb * b_gate * sgb
        put(4, dhb * cbv)
        dcb = dhb * b_b
        dwinB[0:ts, :] = dcb
        dwinB[ts:, :] = haloB[...]
        haloB[...] = dcb[0:HALO_B, :]
        vwin[0:HALO_B, :] = vh_r[...] * nz
        vwin[HALO_B:, :] = vb_r[...]
        dv = jnp.zeros((ts, 512), F32)
        for k in range(KB):
            dv = dv + dwinB[pl.ds(KB - 1 - k, ts), :] * wB_r[k:k + 1, :]
            dwB_o[k:k + 1, :] += _sum_r(dcb * vwin[pl.ds(HALO_B - KB + 1 + k, ts), :])
        put(5, dv * col(3))
        put(3, dv * col(5))

        c_gate = col(7)
        sgc = _sig(c_gate)
        dzc = dzc_r[...]
        ocv = oc_r[...]
        put(7, dzc * ocv * _dsilu(c_gate, sgc))
        do = dzc * c_gate * sgc
        do_o[...] = do
        dd = do * ocv
        dl_o[...] = _lane_pack([jnp.sum(dd[:, DH * h:DH * (h + 1)], axis=1, keepdims=True) for h in range(HC)], ts)

    s512 = jax.ShapeDtypeStruct((S, 512), F32)
    v512 = jax.ShapeDtypeStruct((1, 512), F32)
    return pl.pallas_call(
        body, name="branch_bwd", grid=(nt,),
        out_shape=(jax.ShapeDtypeStruct(dP.shape, BF), s512, jax.ShapeDtypeStruct((S, 128), F32),
                   jax.ShapeDtypeStruct((32, 512), F32), v512, v512, v512, jax.ShapeDtypeStruct((8, 512), F32)),
        in_specs=[pl.BlockSpec((ts, 4096), lambda i: (rev(i), C_AB // 4096)),
                  rows(512), rows(512), rows(512), rows(512), halo(HALO_A), halo(HALO_B),
                  rows(512), rows(512), rows(512), rows(512),
                  full((32, 512)), full((1, 512)), full((1, 512)), full((8, 512)), pl.BlockSpec(memory_space=pl.ANY)],
        out_specs=(pl.BlockSpec((ts, 4096), lambda i: (rev(i), C_AB // 4096)), rows(512), rows(128),
                   full((32, 512)), full((1, 512)), full((1, 512)), full((1, 512)), full((8, 512))),
        scratch_shapes=[pltpu.VMEM((ts + HALO_A, 512), F32), pltpu.VMEM((ts + HALO_A, 512), F32), pltpu.VMEM((HALO_A, 512), F32),
                        pltpu.VMEM((ts + HALO_B, 512), F32), pltpu.VMEM((ts + HALO_B, 512), F32), pltpu.VMEM((HALO_B, 512), F32)],
        input_output_aliases={15: 0},
        compiler_params=_cp(("arbitrary",)),
    )(P, ca, cb, u, vb, u, vb, dza, dzb, dzc, oc, wA, gA, betaA, wB, dP)


def _cum_bwd(P, dcum, bfg, dP, *, ts):
    S = P.shape[0]
    nt = S // ts

    def body(f_ref, dc_ref, bf_r, dp_in, df_o, dbf_o, carry):
        del dp_in
        i = pl.program_id(0)

        @pl.when(i == 0)
        def _():
            carry[...] = jnp.zeros_like(carry)
            dbf_o[...] = jnp.zeros_like(dbf_o)

        r = lax.broadcasted_iota(jnp.int32, (ts, ts), 0)
        c = lax.broadcasted_iota(jnp.int32, (ts, ts), 1)
        tri = (r <= c).astype(F32)
        dlogf = jnp.dot(tri, dc_ref[...], precision=HIGHEST, preferred_element_type=F32) + carry[...]
        carry[...] = dlogf[0:1, :]
        x = f_ref[...] + bf_r[...]
        lane = lax.broadcasted_iota(jnp.int32, (ts, 128), 1)
        df = jnp.where(lane < HC, dlogf * _sig(-x), 0.0)
        df_o[...] = df.astype(BF)
        dbf_o[...] += _sum_r(df)

    blk = pl.BlockSpec((ts, 128), lambda i: (nt - 1 - i, C_F // 128))
    return pl.pallas_call(
        body, name="cum_bwd", grid=(nt,),
        out_shape=(jax.ShapeDtypeStruct(dP.shape, BF), jax.ShapeDtypeStruct((1, 128), F32)),
        in_specs=[blk, pl.BlockSpec((ts, 128), lambda i: (nt - 1 - i, 0)), pl.BlockSpec((1, 128), lambda i: (0, 0)),
                  pl.BlockSpec(memory_space=pl.ANY)],
        out_specs=(blk, pl.BlockSpec((1, 128), lambda i: (0, 0))),
        scratch_shapes=[pltpu.VMEM((1, 128), F32)],
        input_output_aliases={3: 0},
        compiler_params=_cp(("arbitrary",)),
    )(P, dcum, bfg, dP)


def _adamw(w, m, v, gparts, *, name, tr):
    rws, cols = w.shape
    tr = min(tr, rws)
    assert rws % tr == 0 and gparts.shape == (NDEV, rws, cols), (name, w.shape, gparts.shape)
    c1 = 1.0 - ADAM_B1 ** ADAM_STEP
    c2 = 1.0 - ADAM_B2 ** ADAM_STEP

    def body(w_r, m_r, v_r, g_r, g_o, d_o, m_o, v_o):
        g = g_r[0].astype(F32)
        for p in range(1, NDEV):
            g = g + g_r[p].astype(F32)
        mn = ADAM_B1 * m_r[...] + (1.0 - ADAM_B1) * g
        vn = ADAM_B2 * v_r[...] + (1.0 - ADAM_B2) * (g * g)
        g_o[...] = g
        m_o[...] = mn
        v_o[...] = vn
        d_o[...] = -ADAM_LR * ((mn / c1) / (jnp.sqrt(vn / c2) + ADAM_EPS) + ADAM_WD * w_r[...])

    blk = pl.BlockSpec((tr, cols), lambda i: (i, 0))
    shp = jax.ShapeDtypeStruct((rws, cols), F32)
    return pl.pallas_call(
        body, name=name, grid=(rws // tr,), out_shape=(shp,) * 4,
        in_specs=[blk, blk, blk, pl.BlockSpec((NDEV, tr, cols), lambda i: (0, i, 0))],
        out_specs=(blk,) * 4, compiler_params=_cp(("parallel",)),
    )(w, m, v, gparts)


def _slot(p):
    return 4 * p[0] + 2 * p[1] + p[2]


def _all_gather(arrs, *, name):
    na = len(arrs)

    def body(*refs):
        ins, outs = refs[:na], refs[na:2 * na]
        send_sems, recv_sems, local_sems = refs[2 * na:]
        x, y, c = lax.axis_index("x"), lax.axis_index("y"), lax.axis_index("c")
        me, sib = (x, y, c), (x, y, 1 - c)
        chips = [(1 - x, y), (x, 1 - y), (1 - x, 1 - y)]

        def cp(a, k, block, to, src=None):
            dst = outs[a].at[_slot(block)]
            return pltpu.make_async_remote_copy(src_ref=dst if src is None else src, dst_ref=dst,
                                                send_sem=send_sems.at[a, k], recv_sem=recv_sems.at[a, k],
                                                device_id=to, device_id_type=pl.DeviceIdType.MESH)

        mine = [pltpu.make_async_copy(ins[a], outs[a].at[_slot(me)], local_sems.at[a]) for a in range(na)]
        for m in mine:
            m.start()
        first = []
        for a in range(na):
            first.append(cp(a, 0, me, sib, src=ins[a]))
            first += [cp(a, 1 + j, me, (*chip, c), src=ins[a]) for j, chip in enumerate(chips)]
        for f in first:
            f.start()
        passed = []
        for j, chip in enumerate(chips):
            for a in range(na):
                cp(a, 1 + j, (*chip, c), me).wait_recv()
                fwd = cp(a, 4 + j, (*chip, c), sib)
                fwd.start()
                passed.append(fwd)
        for a in range(na):
            cp(a, 0, sib, me).wait_recv()
            for j, chip in enumerate(chips):
                cp(a, 4 + j, (*chip, 1 - c), me).wait_recv()
        for f in first + passed:
            f.wait_send()
        for m in mine:
            m.wait()

    anyspec = pl.BlockSpec(memory_space=pl.ANY)
    return pl.pallas_call(
        body, name=name,
        out_shape=tuple(jax.ShapeDtypeStruct((NDEV,) + a.shape, a.dtype) for a in arrs),
        in_specs=[anyspec] * na, out_specs=(anyspec,) * na,
        scratch_shapes=[pltpu.SemaphoreType.DMA((na, 7)), pltpu.SemaphoreType.DMA((na, 7)), pltpu.SemaphoreType.DMA((na,))],
    )(*arrs)


def _all_to_all(arrs, *, name):
    na = len(arrs)

    def body(*refs):
        ins, outs = refs[:na], refs[na:2 * na]
        send_sems, recv_sems, local_sems = refs[2 * na:]
        x, y, c = lax.axis_index("x"), lax.axis_index("y"), lax.axis_index("c")
        me = (x, y, c)
        peers = [(x ^ ((k >> 2) & 1), y ^ ((k >> 1) & 1), c ^ (k & 1)) for k in range(1, NDEV)]

        def cp(a, k, peer):
            return pltpu.make_async_remote_copy(src_ref=ins[a].at[_slot(peer)], dst_ref=outs[a].at[_slot(me)],
                                                send_sem=send_sems.at[a, k], recv_sem=recv_sems.at[a, k],
                                                device_id=peer, device_id_type=pl.DeviceIdType.MESH)

        def landed(a, k, peer):
            dst = outs[a].at[_slot(peer)]
            return pltpu.make_async_remote_copy(src_ref=dst, dst_ref=dst, send_sem=send_sems.at[a, k], recv_sem=recv_sems.at[a, k],
                                                device_id=peer, device_id_type=pl.DeviceIdType.MESH)

        mine = [pltpu.make_async_copy(ins[a].at[_slot(me)], outs[a].at[_slot(me)], local_sems.at[a]) for a in range(na)]
        for m in mine:
            m.start()
        sends = [cp(a, k, peer) for a in range(na) for k, peer in enumerate(peers)]
        for s in sends:
            s.start()
        for a in range(na):
            for k, peer in enumerate(peers):
                landed(a, k, peer).wait_recv()
        for s in sends:
            s.wait_send()
        for m in mine:
            m.wait()

    anyspec = pl.BlockSpec(memory_space=pl.ANY)
    return pl.pallas_call(
        body, name=name,
        out_shape=tuple(jax.ShapeDtypeStruct(a.shape, a.dtype) for a in arrs),
        in_specs=[anyspec] * na, out_specs=(anyspec,) * na,
        scratch_shapes=[pltpu.SemaphoreType.DMA((na, 7)), pltpu.SemaphoreType.DMA((na, 7)), pltpu.SemaphoreType.DMA((na,))],
    )(*arrs)


def _permute_cols(w):
    parts = [w[..., a:b] for a, b in _RUNS]
    parts.append(jnp.zeros(w.shape[:-1] + (NP - IN_COLS,), w.dtype))
    return jnp.concatenate(parts, axis=-1)


def _unpermute_cols(w):
    off, pieces = 0, []
    for a, b in _RUNS:
        pieces.append((a, w[..., off:off + (b - a)]))
        off += b - a
    return jnp.concatenate([p for _, p in sorted(pieces, key=lambda t: t[0])], axis=-1)


def _tiles(S):
    ts = min(256, S)
    tsb = min(128, S)
    tq = min(512, S)
    return ts, tsb, tq


def _local_step(x, mem, tgt, W, wA, wB, wkv, pa, pb, pc, pm, wo, b_forget, conv_a_b, ln_a_g, ln_a_b, mem_ln_g, mem_ln_b, ln_g, ln_b):
    S = x.shape[0]
    ts, tsb, tq = _tiles(S)
    row = lambda a: a.reshape(1, -1)
    bfp = jnp.pad(b_forget, ((0, 0), (0, 128 - HC)))

    mem_n, mem_hat = _mem_ln_fwd(mem, row(mem_ln_g), row(mem_ln_b))
    saved = []
    for l in range(NL):
        P = _mm(x, W[l], name="proj_fwd", tm=512, tn=1152, tk=D)
        kv = _mm(mem_n, wkv[l], name="kv_fwd", tm=ML, tn=512, tk=D)
        za, zb, u, ca, vb, cb, cum = _pre_fwd(P, wA[l], row(conv_a_b[l]), row(ln_a_g[l]), row(ln_a_b[l]), wB[l], row(bfp[l]), ts=ts)
        cumT = cum[:, :HC].T
        oc, zc, lse = _attn_fwd(P, cum, cumT, tq=tq)
        zm = _xattn_fwd(P, kv, ts=tq)
        merged, ya, yb, yc, ym, xn, xh, rstd = _merge_fwd(za, zb, zc, zm, P, x, pa[l], pb[l], pc[l], pm[l], wo[l],
                                                          row(ln_g[l]), row(ln_b[l]), ts=ts)
        saved.append((x, P, kv, za, zb, zc, zm, u, ca, vb, cb, cum, cumT, oc, lse, merged, ya, yb, yc, ym, xh, rstd))
        x = xn

    dx, loss = _loss_fwd(x, tgt, ts=ts)

    g = {k: [None] * NL for k in ("w_in", "b_forget", "conv_a_w", "conv_a_b", "ln_a_g", "ln_a_b", "conv_b_w", "w_kv_mem",
                                  "p_a", "p_b", "p_c", "p_m", "w_out", "ln_g", "ln_b")}
    dmem_n = [None] * NL
    for l in reversed(range(NL)):
        (xl, P, kv, za, zb, zc, zm, u, ca, vb, cb, cum, cumT, oc, lse, merged, ya, yb, yc, ym, xh, rstd) = saved[l]
        dr, dm, g["w_out"][l], g["ln_g"][l], g["ln_b"][l] = _out_bwd(dx, xh, rstd, merged, wo[l], row(ln_g[l]), ts=ts)
        dP, dza, dzb, dzc, dzm, g["p_a"][l], g["p_b"][l], g["p_c"][l], g["p_m"][l] = _merge_bwd(
            dm, P, ya, yb, yc, ym, za, zb, zc, zm, pa[l], pb[l], pc[l], pm[l], ts=tsb)
        dP, do, dlt, dwA, g["conv_a_b"][l], g["ln_a_g"][l], g["ln_a_b"][l], dwB = _branch_bwd(
            P, ca, cb, u, vb, dza, dzb, dzc, oc, wA[l], row(ln_a_g[l]), row(ln_a_b[l]), wB[l], dP, ts=ts)
        g["conv_a_w"][l], g["conv_b_w"][l] = dwA[:KA], dwB[:KB]
        dP, dckT = _attn_bwd_dkv(P, do, lse, dlt, cum, cumT, dP, tq=tq)
        dP, dcq = _attn_bwd_dq(P, do, lse, dlt, cum, cumT, dP, tq=tq)
        dcum = dcq - jnp.pad(dckT.T, ((0, 0), (0, 128 - HC)))
        dP, dbf = _cum_bwd(P, dcum, row(bfp[l]), dP, ts=ts)
        g["b_forget"][l] = dbf[0, :HC]
        dP, dkv = _xattn_bwd(P, kv, dzm, dP, ts=tq)
        g["w_kv_mem"][l] = _mm(mem_n.T, dkv, name="wkv_bwd", tm=D, tn=512, tk=ML)
        dmem_n[l] = _mm(dkv, wkv[l], name="memn_bwd", nt=True, tm=ML, tn=D, tk=512)
        g["w_in"][l] = _mm(xl.T.astype(BF), dP, name="win_bwd", out_dtype=BF, tm=D, tn=1152, tk=512)
        dx = _mm(dP, W[l], name="x_bwd", nt=True, tm=512, tn=D, tk=1152, add=dr, add_scale=ALPHA)

    g["mem_ln_g"], g["mem_ln_b"] = _mem_ln_bwd(dmem_n, mem_hat)
    out = {k: (jnp.stack(v) if isinstance(v, list) else v) for k, v in g.items()}
    return loss[0, 0], dx, out


_SMALL = (("b_forget", (NL, HC)), ("conv_a_b", (NL, 512)), ("ln_a_g", (NL, 512)), ("ln_a_b", (NL, 512)),
          ("mem_ln_g", (D,)), ("mem_ln_b", (D,)), ("ln_g", (NL, D)), ("ln_b", (NL, D)),
          ("conv_a_w", (NL, KA, 512)), ("conv_b_w", (NL, KB, 512)))


def _pack(parts, rows_mult=8):
    flat = jnp.concatenate([p.reshape(-1).astype(F32) for p in parts])
    n = flat.shape[0]
    rows = -(-n // 128)
    rows = -(-rows // rows_mult) * rows_mult
    return jnp.pad(flat, (0, rows * 128 - n)).reshape(rows, 128)


def _unpack(buf, shapes):
    flat = buf.reshape(-1)
    out, off = [], 0
    for shp in shapes:
        n = 1
        for d in shp:
            n *= d
        out.append(flat[off:off + n].reshape(shp))
        off += n
    return out


def kernel(x, mem, w_in, b_forget, conv_a_w, conv_a_b, ln_a_g, ln_a_b, conv_b_w, w_kv_mem, mem_ln_g, mem_ln_b, p_a, p_b, p_c, p_m, w_out, ln_g, ln_b, loss_target, m_w_in, m_b_forget, m_conv_a_w, m_conv_a_b, m_ln_a_g, m_ln_a_b, m_conv_b_w, m_w_kv_mem, m_mem_ln_g, m_mem_ln_b, m_p_a, m_p_b, m_p_c, m_p_m, m_w_out, m_ln_g, m_ln_b, v_w_in, v_b_forget, v_conv_a_w, v_conv_a_b, v_ln_a_g, v_ln_a_b, v_conv_b_w, v_w_kv_mem, v_mem_ln_g, v_mem_ln_b, v_p_a, v_p_b, v_p_c, v_p_m, v_w_out, v_ln_g, v_ln_b):
    wts = dict(w_in=w_in, b_forget=b_forget, conv_a_w=conv_a_w, conv_a_b=conv_a_b, ln_a_g=ln_a_g, ln_a_b=ln_a_b, conv_b_w=conv_b_w,
               w_kv_mem=w_kv_mem, mem_ln_g=mem_ln_g, mem_ln_b=mem_ln_b, p_a=p_a, p_b=p_b, p_c=p_c, p_m=p_m, w_out=w_out, ln_g=ln_g, ln_b=ln_b)
    mom = dict(w_in=m_w_in, b_forget=m_b_forget, conv_a_w=m_conv_a_w, conv_a_b=m_conv_a_b, ln_a_g=m_ln_a_g, ln_a_b=m_ln_a_b,
               conv_b_w=m_conv_b_w, w_kv_mem=m_w_kv_mem, mem_ln_g=m_mem_ln_g, mem_ln_b=m_mem_ln_b, p_a=m_p_a, p_b=m_p_b, p_c=m_p_c,
               p_m=m_p_m, w_out=m_w_out, ln_g=m_ln_g, ln_b=m_ln_b)
    vel = dict(w_in=v_w_in, b_forget=v_b_forget, conv_a_w=v_conv_a_w, conv_a_b=v_conv_a_b, ln_a_g=v_ln_a_g, ln_a_b=v_ln_a_b,
               conv_b_w=v_conv_b_w, w_kv_mem=v_w_kv_mem, mem_ln_g=v_mem_ln_g, mem_ln_b=v_mem_ln_b, p_a=v_p_a, p_b=v_p_b, p_c=v_p_c,
               p_m=v_p_m, w_out=v_w_out, ln_g=v_ln_g, ln_b=v_ln_b)
    names = ("w_in", "b_forget", "conv_a_w", "conv_a_b", "ln_a_g", "ln_a_b", "conv_b_w", "w_kv_mem", "mem_ln_g", "mem_ln_b",
             "p_a", "p_b", "p_c", "p_m", "w_out", "ln_g", "ln_b")
    mid = ("p_a", "p_b", "p_c", "p_m", "w_out", "w_kv_mem")
    me = 4 * lax.axis_index("x") + 2 * lax.axis_index("y") + lax.axis_index("c")

    def mid_rows(a):
        return a.reshape(-1, 128)

    mid_shapes = [wts[n].shape for n in mid]
    mid_nrows = [mid_rows(wts[n]).shape[0] for n in mid]
    pk16 = jnp.concatenate([mid_rows(wts[n]) for n in mid], axis=0)
    pk32 = jnp.concatenate([conv_a_w, conv_b_w], axis=1).reshape(NL * (KA + KB), 512 // NDEV)
    g_win, g16, g32 = _all_gather([w_in.reshape(NL * D, SHARD_IN).astype(BF), pk16.astype(BF), pk32], name="gather_weights")

    W = _permute_cols(g_win.reshape(NDEV, NL, D, SHARD_IN).transpose(1, 2, 0, 3).reshape(NL, D, IN_COLS))
    full = {}
    off = 0
    for n, shp, nr in zip(mid, mid_shapes, mid_nrows):
        blk = g16[:, off:off + nr].reshape((NDEV,) + shp)
        off += nr
        if n in ("w_out", "w_kv_mem"):
            full[n] = blk.transpose(1, 0, 2, 3).reshape(NL, NDEV * shp[1], shp[2])
        else:
            full[n] = blk.transpose(1, 2, 0, 3).reshape(NL, shp[1], NDEV * shp[2])
    conv = g32.reshape(NDEV, NL, KA + KB, 512 // NDEV).transpose(1, 2, 0, 3).reshape(NL, KA + KB, 512)
    wA = jnp.pad(conv[:, :KA], ((0, 0), (0, 32 - KA), (0, 0)))
    wB = jnp.pad(conv[:, KA:], ((0, 0), (0, 8 - KB), (0, 0)))

    loss, dx, g = _local_step(x[0], mem[0], loss_target[0], W, wA, wB, full["w_kv_mem"], full["p_a"], full["p_b"], full["p_c"],
                              full["p_m"], full["w_out"], b_forget, conv_a_b, ln_a_g, ln_a_b, mem_ln_g, mem_ln_b, ln_g, ln_b)
    loss = lax.psum(loss, ("x", "y", "c"))

    gw = _unpermute_cols(g["w_in"]).reshape(NL, D, NDEV, SHARD_IN).transpose(2, 0, 1, 3).reshape(NDEV, NL * D, SHARD_IN)
    chunks = []
    for n, shp in zip(mid, mid_shapes):
        a = g[n]
        if n in ("w_out", "w_kv_mem"):
            a = a.reshape(NL, NDEV, shp[1], shp[2]).transpose(1, 0, 2, 3)
        else:
            a = a.reshape(NL, shp[1], NDEV, shp[2]).transpose(2, 0, 1, 3)
        chunks.append(a.reshape(NDEV, -1, 128))
    g16s = jnp.concatenate(chunks, axis=1).astype(BF)
    small = _pack([g[n] for n, _ in _SMALL])
    r_win, r16 = _all_to_all([gw, g16s], name="scatter_grads")
    (r_small,) = _all_gather([small], name="gather_small")

    res = {}
    res["w_in"] = [a.reshape(NL, D, SHARD_IN) for a in
                   _adamw(w_in.reshape(NL * D, SHARD_IN), m_w_in.reshape(NL * D, SHARD_IN), v_w_in.reshape(NL * D, SHARD_IN),
                          r_win, name="adamw_w_in", tr=128)]
    pk = lambda d: jnp.concatenate([mid_rows(d[n]) for n in mid], axis=0)
    o16 = _adamw(pk(wts), pk(mom), pk(vel), r16, name="adamw_mid", tr=1024)
    off = 0
    for n, shp, nr in zip(mid, mid_shapes, mid_nrows):
        res[n] = [o[off:off + nr].reshape(shp) for o in o16]
        off += nr

    def small_view(d, n):
        a = d[n]
        if n in ("conv_a_w", "conv_b_w"):
            fullw = jnp.zeros(a.shape[:2] + (512,), F32)
            return lax.dynamic_update_slice(fullw, a, (0, 0, me * (512 // NDEV)))
        return a

    spk = lambda d: _pack([small_view(d, n) for n, _ in _SMALL])
    osm = _adamw(spk(wts), spk(mom), spk(vel), r_small, name="adamw_small", tr=1024)
    osm = [_unpack(o, [s for _, s in _SMALL]) for o in osm]
    for idx, (n, _) in enumerate(_SMALL):
        vals = [o[idx] for o in osm]
        if n in ("conv_a_w", "conv_b_w"):
            vals = [lax.dynamic_slice(a, (0, 0, me * (512 // NDEV)), a.shape[:2] + (512 // NDEV,)) for a in vals]
        res[n] = vals

    outs = [loss, dx[None]]
    for k in range(4):
        outs += [res[n][k] for n in names]
    return tuple(outs)
```

```python
import jax
import jax.numpy as jnp
from jax import lax
from jax.experimental import pallas as pl
from jax.experimental.pallas import tpu as pltpu

F32 = jnp.float32
BF = jnp.bfloat16
HIGHEST = lax.Precision.HIGHEST

D = 1024
NL = 4
NDEV = 8
HC, DH = 8, 64
HM = 4
ML = 256
KA, KB = 31, 3
HALO_A, HALO_B = 32, 8
ALPHA = (2.0 * NL) ** 0.25
EPS = 1e-5
SCALE = DH ** -0.5
NEG = -1e30

ADAM_LR, ADAM_B1, ADAM_B2, ADAM_EPS, ADAM_WD, ADAM_STEP = 0.001, 0.9, 0.999, 1e-08, 0.01, 10

C_G = 0
C_AB = 4096
C_Q = 8192
C_M = 8704
C_KV = 9216
C_F = 10240
NP = 10368
_RUNS = ((6152, 10248), (0, 3584), (5128, 5640), (3584, 4096), (5640, 6152), (4096, 5120), (5120, 5128))
IN_COLS = 10248
SHARD_IN = IN_COLS // NDEV

VMEM_LIMIT = 56 * 1024 * 1024

NT_DIMS = (((1,), (1,)), ((), ()))
TN_DIMS = (((0,), (0,)), ((), ()))


def _cp(sem=None):
    return pltpu.CompilerParams(dimension_semantics=sem, vmem_limit_bytes=VMEM_LIMIT)


def _sig(x):
    return 1.0 / (1.0 + jnp.exp(-x))


def _dsilu(x, s):
    return s * (1.0 + x * (1.0 - s))


def _mean_l(x):
    return jnp.mean(x, axis=-1, keepdims=True)


def _sum_r(x):
    return jnp.sum(x, axis=0, keepdims=True)


def _ln_hat(x):
    mu = _mean_l(x)
    xc = x - mu
    rstd = lax.rsqrt(_mean_l(xc * xc) + EPS)
    return xc * rstd, rstd


def _ln_bwd(dxh, xh, rstd):
    return rstd * (dxh - _mean_l(dxh) - xh * _mean_l(dxh * xh))


def _dot(a, b, dims=None):
    if dims is None:
        return jnp.dot(a, b, preferred_element_type=F32)
    return lax.dot_general(a, b, dims, preferred_element_type=F32)


def _lane_pack(cols, rows):
    lane = lax.broadcasted_iota(jnp.int32, (rows, 128), 1)
    out = jnp.zeros((rows, 128), F32)
    for h, c in enumerate(cols):
        out = jnp.where(lane == h, c, out)
    return out


def _mm(a, b, *, name, nt=False, out_dtype=F32, tm=512, tn=512, tk=512, add=None, add_scale=1.0):
    m, kdim = a.shape
    n = b.shape[0] if nt else b.shape[1]
    tm, tn, tk = min(tm, m), min(tn, n), min(tk, kdim)
    assert m % tm == 0 and n % tn == 0 and kdim % tk == 0, (name, a.shape, b.shape, tm, tn, tk)
    nk = kdim // tk

    def body(*refs):
        if add is None:
            a_ref, b_ref, o_ref, acc_ref = refs
        else:
            a_ref, b_ref, add_ref, o_ref, acc_ref = refs
        k = pl.program_id(2)

        @pl.when(k == 0)
        def _():
            acc_ref[...] = jnp.zeros_like(acc_ref)

        acc_ref[...] += _dot(a_ref[...].astype(BF), b_ref[...].astype(BF), NT_DIMS if nt else None)

        @pl.when(k == nk - 1)
        def _():
            r = acc_ref[...]
            if add is not None:
                r = r + add_scale * add_ref[...]
            o_ref[...] = r.astype(out_dtype)

    in_specs = [pl.BlockSpec((tm, tk), lambda i, j, k: (i, k)),
                pl.BlockSpec((tn, tk), lambda i, j, k: (j, k)) if nt else pl.BlockSpec((tk, tn), lambda i, j, k: (k, j))]
    args = [a, b]
    if add is not None:
        in_specs.append(pl.BlockSpec((tm, tn), lambda i, j, k: (i, j)))
        args.append(add)
    return pl.pallas_call(
        body, name=name, grid=(m // tm, n // tn, nk),
        out_shape=jax.ShapeDtypeStruct((m, n), out_dtype),
        in_specs=in_specs, out_specs=pl.BlockSpec((tm, tn), lambda i, j, k: (i, j)),
        scratch_shapes=[pltpu.VMEM((tm, tn), F32)],
        compiler_params=_cp(("parallel", "parallel", "arbitrary")),
    )(*args)


def _mem_ln_fwd(mem, g, b):
    def body(m_ref, g_ref, b_ref, n_ref, h_ref):
        xh, _ = _ln_hat(m_ref[...])
        h_ref[...] = xh
        n_ref[...] = xh * g_ref[...] + b_ref[...]

    shp = jax.ShapeDtypeStruct(mem.shape, F32)
    return pl.pallas_call(body, name="mem_ln_fwd", out_shape=(shp, shp), compiler_params=_cp())(mem, g, b)


def _mem_ln_bwd(dns, mhat):
    def body(*refs):
        d_refs, h_ref, dg_ref, db_ref = refs[:NL], refs[NL], refs[NL + 1], refs[NL + 2]
        dn = d_refs[0][...]
        for r in d_refs[1:]:
            dn = dn + r[...]
        dg_ref[...] = _sum_r(dn * h_ref[...])
        db_ref[...] = _sum_r(dn)

    shp = jax.ShapeDtypeStruct((1, D), F32)
    return pl.pallas_call(body, name="mem_ln_bwd", out_shape=(shp, shp), compiler_params=_cp())(*dns, mhat)


def _pre_fwd(P, wA, bA, gA, betaA, wB, bfg, *, ts):
    S = P.shape[0]
    nt = S // ts
    cb = C_AB // 512

    def cur(j):
        return pl.BlockSpec((ts, 512), lambda i, j=j: (i, cb + j))

    def halo(j, rows):
        return pl.BlockSpec((rows, 512), lambda i, j=j: (jnp.maximum(i * (ts // rows) - 1, 0), cb + j))

    def full(shape):
        return pl.BlockSpec(shape, lambda i: (0, 0))

    def body(au, av, ag, bh, bb, bc, bg, f_ref, au_h, av_h, bh_h, bc_h, wA_r, bA_r, gA_r, betaA_r, wB_r, bf_r,
             za_o, zb_o, u_o, ca_o, vb_o, cb_o, cum_o, winA, winB, carry):
        i = pl.program_id(0)
        nz = (i > 0).astype(F32)

        u = au[...] * _sig(av[...])
        winA[0:HALO_A, :] = au_h[...] * _sig(av_h[...]) * nz
        winA[HALO_A:, :] = u
        acc = jnp.zeros((ts, 512), F32)
        for k in range(KA):
            acc = acc + winA[pl.ds(HALO_A - KA + 1 + k, ts), :] * wA_r[k:k + 1, :]
        ca = acc + bA_r[...]
        xh, _ = _ln_hat(ca)
        n = xh * gA_r[...] + betaA_r[...]
        a = n * _sig(n)
        agv = ag[...]
        za_o[...] = (a * agv * _sig(agv)).astype(BF)
        u_o[...] = u
        ca_o[...] = ca

        vb = bc[...] * bh[...]
        winB[0:HALO_B, :] = bc_h[...] * bh_h[...] * nz
        winB[HALO_B:, :] = vb
        accb = jnp.zeros((ts, 512), F32)
        for k in range(KB):
            accb = accb + winB[pl.ds(HALO_B - KB + 1 + k, ts), :] * wB_r[k:k + 1, :]
        bgv = bg[...]
        zb_o[...] = (bb[...] * accb * bgv * _sig(bgv)).astype(BF)
        vb_o[...] = vb
        cb_o[...] = accb

        @pl.when(i == 0)
        def _():
            carry[...] = jnp.zeros_like(carry)

        x = f_ref[...] + bf_r[...]
        logf = jnp.minimum(x, 0.0) - jnp.log1p(jnp.exp(-jnp.abs(x)))
        r = lax.broadcasted_iota(jnp.int32, (ts, ts), 0)
        c = lax.broadcasted_iota(jnp.int32, (ts, ts), 1)
        tri = (r >= c).astype(F32)
        cum = jnp.dot(tri, logf, precision=HIGHEST, preferred_element_type=F32) + carry[...]
        cum_o[...] = cum
        carry[...] = cum[ts - 1:ts, :]

    s512 = lambda dt: jax.ShapeDtypeStruct((S, 512), dt)
    o512 = pl.BlockSpec((ts, 512), lambda i: (i, 0))
    return pl.pallas_call(
        body, name="pre_fwd", grid=(nt,),
        out_shape=(s512(BF), s512(BF), s512(F32), s512(F32), s512(F32), s512(F32), jax.ShapeDtypeStruct((S, 128), F32)),
        in_specs=[cur(0), cur(1), cur(2), cur(3), cur(4), cur(5), cur(6),
                  pl.BlockSpec((ts, 128), lambda i: (i, C_F // 128)),
                  halo(0, HALO_A), halo(1, HALO_A), halo(3, HALO_B), halo(5, HALO_B),
                  full((32, 512)), full((1, 512)), full((1, 512)), full((1, 512)), full((8, 512)), full((1, 128))],
        out_specs=(o512, o512, o512, o512, o512, o512, pl.BlockSpec((ts, 128), lambda i: (i, 0))),
        scratch_shapes=[pltpu.VMEM((ts + HALO_A, 512), F32), pltpu.VMEM((ts + HALO_B, 512), F32), pltpu.VMEM((1, 128), F32)],
        compiler_params=_cp(("arbitrary",)),
    )(P, P, P, P, P, P, P, P, P, P, P, P, wA, bA, gA, betaA, wB, bfg)


RC = 32


def _split3(c):
    c1 = c.astype(BF).astype(F32)
    r = c - c1
    c2 = r.astype(BF).astype(F32)
    return c1, c2, r - c2


def _attn_prep(P, cum, *, ts):
    S = P.shape[0]

    def body(q_ref, kv_ref, cum_ref, qa_o, ka_o, v_o):
        lane = lax.broadcasted_iota(jnp.int32, (ts, DH), 1)
        for h in range(HC):
            sl = slice(DH * h, DH * (h + 1))
            c1, c2, c3 = _split3(cum_ref[:, h:h + 1])
            lo = jnp.where(lane == 0, c1, jnp.where(lane == 1, c2, jnp.where(lane == 2, c3, 0.0)))
            hi = jnp.where(lane == 3, c1, jnp.where(lane == 4, c2, jnp.where(lane == 5, c3, 0.0)))
            qa_o[h, :, 0:DH] = (q_ref[:, sl] * SCALE).astype(BF)
            qa_o[h, :, DH:2 * DH] = (lo + jnp.where((lane >= 3) & (lane < 6), 1.0, 0.0)).astype(BF)
            ka_o[h, :, 0:DH] = kv_ref[:, sl].astype(BF)
            ka_o[h, :, DH:2 * DH] = (jnp.where(lane < 3, 1.0, 0.0) - hi).astype(BF)
            v_o[h] = kv_ref[:, 512 + DH * h:512 + DH * (h + 1)].astype(BF)

    aug = jax.ShapeDtypeStruct((HC, S, 2 * DH), BF)
    return pl.pallas_call(
        body, name="attn_prep", grid=(S // ts,),
        out_shape=(aug, aug, jax.ShapeDtypeStruct((HC, S, DH), BF)),
        in_specs=[pl.BlockSpec((ts, 512), lambda i: (i, C_Q // 512)), pl.BlockSpec((ts, 1024), lambda i: (i, C_KV // 1024)),
                  pl.BlockSpec((ts, 128), lambda i: (i, 0))],
        out_specs=(pl.BlockSpec((HC, ts, 2 * DH), lambda i: (0, i, 0)), pl.BlockSpec((HC, ts, 2 * DH), lambda i: (0, i, 0)),
                   pl.BlockSpec((HC, ts, DH), lambda i: (0, i, 0))),
        compiler_params=_cp(("parallel",)),
    )(P, P, cum)


LW = 128


def _lanes(c):
    return slice(LW * c, LW * (c + 1))


def _diag_slices(rc, n, rows_are_queries):
    out = []
    for c in range(n // LW):
        r0, r1, c0, c1 = rc * RC, rc * RC + RC - 1, LW * c, LW * c + LW - 1
        lo, hi = (c1 <= r0, c0 > r1) if rows_are_queries else (r1 <= c0, r0 > c1)
        if lo:
            out.append("all")
        elif hi:
            out.append("none")
        else:
            r = lax.broadcasted_iota(jnp.int32, (RC, LW), 0) + r0
            cc = lax.broadcasted_iota(jnp.int32, (RC, LW), 1) + c0
            out.append((r >= cc) if rows_are_queries else (cc >= r))
    return out


def _pieces(ref2d, rows, rc, n, masked, rows_are_queries):
    kinds = _diag_slices(rc, n, rows_are_queries) if masked else ["all"] * (n // LW)
    out = []
    for c, kind in enumerate(kinds):
        if isinstance(kind, str):
            out.append(ref2d[rows, _lanes(c)] if kind == "all" else None)
        else:
            out.append(jnp.where(kind, ref2d[rows, _lanes(c)], NEG))
    return out


def _attn_fwd(P, Qa, Ka, V, *, tq):
    S = P.shape[0]
    nq = S // tq

    def body(qa_ref, ka_ref, v_ref, cg_ref, o_ref, zc_ref, lse_ref, s_s, p_s, m_s, l_s, acc_s, pm_s, al_s):
        i, j = pl.program_id(0), pl.program_id(1)

        @pl.when(j == 0)
        def _():
            m_s[...] = jnp.full_like(m_s, NEG)
            l_s[...] = jnp.zeros_like(l_s)
            acc_s[...] = jnp.zeros_like(acc_s)

        def step(masked):
            s_s[0] = _dot(qa_ref[0], ka_ref[0], NT_DIMS)
            for h in range(HC):
                b = h % 2
                if h + 1 < HC:
                    s_s[1 - b] = _dot(qa_ref[h + 1], ka_ref[h + 1], NT_DIMS)
                for rc in range(tq // RC):
                    rows = slice(rc * RC, (rc + 1) * RC)
                    pm = None
                    for sc in _pieces(s_s.at[b], rows, rc, tq, masked, True):
                        if sc is not None:
                            pm = sc if pm is None else jnp.maximum(pm, sc)
                    pm_s[rows, :] = pm
                m_prev = m_s[h]
                m_new = jnp.maximum(m_prev, jnp.max(pm_s[...], axis=1, keepdims=True))
                alpha = jnp.exp(m_prev - m_new)
                m_s[h] = m_new
                al_s[...] = alpha
                for rc in range(tq // RC):
                    rows = slice(rc * RC, (rc + 1) * RC)
                    mb = m_s[h, rows]
                    ps = None
                    for c, sc in enumerate(_pieces(s_s.at[b], rows, rc, tq, masked, True)):
                        if sc is None:
                            p_s[b, rows, _lanes(c)] = jnp.zeros((RC, LW), BF)
                            continue
                        p = jnp.exp(sc - mb)
                        ps = p if ps is None else ps + p
                        p_s[b, rows, _lanes(c)] = p.astype(BF)
                    l_s[h, rows] = al_s[rows] * l_s[h, rows] + ps
                acc_s[h] = al_s[:, 0:DH] * acc_s[h] + _dot(p_s[b], v_ref[h])

        @pl.when(j < i)
        def _():
            step(False)

        @pl.when(j == i)
        def _():
            step(True)
            for h in range(HC):
                l = jnp.sum(l_s[h], axis=1, keepdims=True)
                o_ref[:, DH * h:DH * (h + 1)] = acc_s[h] / l
                lse_ref[h] = m_s[h] + jnp.log(l)
            cg = cg_ref[...]
            zc_ref[...] = (o_ref[...] * cg * _sig(cg)).astype(BF)

    stat = pltpu.VMEM((HC, tq, LW), F32)
    return pl.pallas_call(
        body, name="attn_fwd", grid=(nq, nq),
        out_shape=(jax.ShapeDtypeStruct((S, 512), F32), jax.ShapeDtypeStruct((S, 512), BF), jax.ShapeDtypeStruct((HC, S, LW), F32)),
        in_specs=[pl.BlockSpec((HC, tq, 2 * DH), lambda i, j: (0, i, 0)),
                  pl.BlockSpec((HC, tq, 2 * DH), lambda i, j: (0, jnp.minimum(i, j), 0)),
                  pl.BlockSpec((HC, tq, DH), lambda i, j: (0, jnp.minimum(i, j), 0)),
                  pl.BlockSpec((tq, 512), lambda i, j: (i, C_AB // 512 + 7))],
        out_specs=(pl.BlockSpec((tq, 512), lambda i, j: (i, 0)), pl.BlockSpec((tq, 512), lambda i, j: (i, 0)),
                   pl.BlockSpec((HC, tq, LW), lambda i, j: (0, i, 0))),
        scratch_shapes=[pltpu.VMEM((2, tq, tq), F32), pltpu.VMEM((2, tq, tq), BF), stat, stat, pltpu.VMEM((HC, tq, DH), F32),
                        pltpu.VMEM((tq, LW), F32), pltpu.VMEM((tq, LW), F32)],
        compiler_params=_cp(("parallel", "arbitrary")),
    )(Qa, Ka, V, P)


def _attn_bwd_dkv(Qa, Ka, V, dO, lseT, dltT, dP, *, tq):
    S = Qa.shape[1]
    nq = S // tq

    def body(qa_ref, ka_ref, v_ref, do_ref, lse_ref, dl_ref, dp_in, dkv_o, dck_o, s_s, dp_s, p_s, ds_s, dk_s, dv_s, dck_s):
        del dp_in
        j, i = pl.program_id(0), pl.program_id(1)

        @pl.when(i == 0)
        def _():
            dk_s[...] = jnp.zeros_like(dk_s)
            dv_s[...] = jnp.zeros_like(dv_s)
            dck_s[...] = jnp.zeros_like(dck_s)

        def mm(h, b):
            s_s[b] = _dot(ka_ref[h], qa_ref[h], NT_DIMS)
            dp_s[b] = _dot(v_ref[h], do_ref[h], NT_DIMS)

        def step(masked):
            mm(0, 0)
            for h in range(HC):
                b = h % 2
                if h + 1 < HC:
                    mm(h + 1, 1 - b)
                for rc in range(tq // RC):
                    rows = slice(rc * RC, (rc + 1) * RC)
                    acc = None
                    for c, sc in enumerate(_pieces(s_s.at[b], rows, rc, tq, masked, False)):
                        if sc is None:
                            p_s[rows, _lanes(c)] = jnp.zeros((RC, LW), BF)
                            ds_s[rows, _lanes(c)] = jnp.zeros((RC, LW), BF)
                            continue
                        p = jnp.exp(sc - lse_ref[h:h + 1, _lanes(c)])
                        ds = p * (dp_s[b, rows, _lanes(c)] - dl_ref[h:h + 1, _lanes(c)])
                        p_s[rows, _lanes(c)] = p.astype(BF)
                        ds_s[rows, _lanes(c)] = ds.astype(BF)
                        acc = ds if acc is None else acc + ds
                    dck_s[h, rows] += acc
                dv_s[h] += _dot(p_s[...], do_ref[h])
                dk_s[h] += _dot(ds_s[...], qa_ref[h])

        @pl.when(i > j)
        def _():
            step(False)

        @pl.when(i == j)
        def _():
            step(True)

        @pl.when(i == nq - 1)
        def _():
            for h in range(HC):
                dkv_o[:, DH * h:DH * (h + 1)] = dk_s[h][:, 0:DH].astype(BF)
                dkv_o[:, 512 + DH * h:512 + DH * (h + 1)] = dv_s[h].astype(BF)
            dck_o[...] = _lane_pack([jnp.sum(dck_s[h], axis=1, keepdims=True) for h in range(HC)], tq)

    def qspec(w):
        return pl.BlockSpec((HC, tq, w), lambda j, i: (0, jnp.maximum(i, j), 0))

    def kspec(w):
        return pl.BlockSpec((HC, tq, w), lambda j, i: (0, j, 0))

    rowv = pl.BlockSpec((8, tq), lambda j, i: (0, jnp.maximum(i, j)))
    return pl.pallas_call(
        body, name="attn_bwd_dkv", grid=(nq, nq),
        out_shape=(jax.ShapeDtypeStruct(dP.shape, BF), jax.ShapeDtypeStruct((S, 128), F32)),
        in_specs=[qspec(2 * DH), kspec(2 * DH), kspec(DH), qspec(DH), rowv, rowv, pl.BlockSpec(memory_space=pl.ANY)],
        out_specs=(pl.BlockSpec((tq, 1024), lambda j, i: (j, C_KV // 1024)), pl.BlockSpec((tq, 128), lambda j, i: (j, 0))),
        scratch_shapes=[pltpu.VMEM((2, tq, tq), F32), pltpu.VMEM((2, tq, tq), F32), pltpu.VMEM((tq, tq), BF), pltpu.VMEM((tq, tq), BF),
                        pltpu.VMEM((HC, tq, 2 * DH), F32), pltpu.VMEM((HC, tq, DH), F32), pltpu.VMEM((HC, tq, LW), F32)],
        input_output_aliases={6: 0},
        compiler_params=_cp(("parallel", "arbitrary")),
    )(Qa, Ka, V, dO, lseT, dltT, dP)


def _attn_bwd_dq(Qa, Ka, V, dO, lse, dlt, dP, *, tq):
    S = Qa.shape[1]
    nq = S // tq

    def body(qa_ref, ka_ref, v_ref, do_ref, lse_ref, dl_ref, dp_in, dq_o, dcq_o, s_s, dp_s, ds_s, dq_s, dcq_s):
        del dp_in
        i, j = pl.program_id(0), pl.program_id(1)

        @pl.when(j == 0)
        def _():
            dq_s[...] = jnp.zeros_like(dq_s)
            dcq_s[...] = jnp.zeros_like(dcq_s)

        def mm(h, b):
            s_s[b] = _dot(qa_ref[h], ka_ref[h], NT_DIMS)
            dp_s[b] = _dot(do_ref[h], v_ref[h], NT_DIMS)

        def step(masked):
            mm(0, 0)
            for h in range(HC):
                b = h % 2
                if h + 1 < HC:
                    mm(h + 1, 1 - b)
                for rc in range(tq // RC):
                    rows = slice(rc * RC, (rc + 1) * RC)
                    lb = lse_ref[h, rows]
                    db = dl_ref[h, rows]
                    acc = None
                    for c, sc in enumerate(_pieces(s_s.at[b], rows, rc, tq, masked, True)):
                        if sc is None:
                            ds_s[rows, _lanes(c)] = jnp.zeros((RC, LW), BF)
                            continue
                        ds = jnp.exp(sc - lb) * (dp_s[b, rows, _lanes(c)] - db)
                        ds_s[rows, _lanes(c)] = ds.astype(BF)
                        acc = ds if acc is None else acc + ds
                    dcq_s[h, rows] += acc
                dq_s[h] += _dot(ds_s[...], ka_ref[h])

        @pl.when(j < i)
        def _():
            step(False)

        @pl.when(j == i)
        def _():
            step(True)
            for h in range(HC):
                dq_o[:, DH * h:DH * (h + 1)] = (dq_s[h][:, 0:DH] * SCALE).astype(BF)
            dcq_o[...] = _lane_pack([jnp.sum(dcq_s[h], axis=1, keepdims=True) for h in range(HC)], tq)

    def qspec(w):
        return pl.BlockSpec((HC, tq, w), lambda i, j: (0, i, 0))

    def kspec(w):
        return pl.BlockSpec((HC, tq, w), lambda i, j: (0, jnp.minimum(i, j), 0))

    colv = pl.BlockSpec((tq, 128), lambda i, j: (i, 0))
    return pl.pallas_call(
        body, name="attn_bwd_dq", grid=(nq, nq),
        out_shape=(jax.ShapeDtypeStruct(dP.shape, BF), jax.ShapeDtypeStruct((S, 128), F32)),
        in_specs=[qspec(2 * DH), kspec(2 * DH), kspec(DH), qspec(DH), qspec(LW), qspec(LW), pl.BlockSpec(memory_space=pl.ANY)],
        out_specs=(pl.BlockSpec((tq, 512), lambda i, j: (i, C_Q // 512)), colv),
        scratch_shapes=[pltpu.VMEM((2, tq, tq), F32), pltpu.VMEM((2, tq, tq), F32), pltpu.VMEM((tq, tq), BF),
                        pltpu.VMEM((HC, tq, 2 * DH), F32), pltpu.VMEM((HC, tq, LW), F32)],
        input_output_aliases={6: 0},
        compiler_params=_cp(("parallel", "arbitrary")),
    )(Qa, Ka, V, dO, lse, dlt, dP)


def _xattn_probs(qm_ref, kv_ref, h):
    sl = slice(DH * h, DH * (h + 1))
    s = _dot(qm_ref[:, sl].astype(BF), kv_ref[:, sl].astype(BF), NT_DIMS) * SCALE
    p = jnp.exp(s - jnp.max(s, axis=1, keepdims=True))
    return p / jnp.sum(p, axis=1, keepdims=True)


def _xattn_fwd(P, kv, *, ts):
    S = P.shape[0]

    def body(qm_ref, kv_ref, zm_o, o_s):
        for h in range(HM):
            sl = slice(DH * h, DH * (h + 1))
            p = _xattn_probs(qm_ref, kv_ref, h)
            o_s[:, sl] = _dot(p.astype(BF), kv_ref[:, ML + DH * h:ML + DH * (h + 1)].astype(BF))
        mg = qm_ref[:, 256:512]
        zm_o[...] = (o_s[...] * mg * _sig(mg)).astype(BF)

    return pl.pallas_call(
        body, name="xattn_fwd", grid=(S // ts,),
        out_shape=jax.ShapeDtypeStruct((S, 256), BF),
        in_specs=[pl.BlockSpec((ts, 512), lambda i: (i, C_M // 512)), pl.BlockSpec((ML, 512), lambda i: (0, 0))],
        out_specs=pl.BlockSpec((ts, 256), lambda i: (i, 0)),
        scratch_shapes=[pltpu.VMEM((ts, 256), F32)],
        compiler_params=_cp(("parallel",)),
    )(P, kv)


def _xattn_bwd(P, kv, dzm, dP, *, ts):
    S = P.shape[0]

    def body(qm_ref, kv_ref, dz_ref, dp_in, dqm_o, dkv_o):
        del dp_in
        i = pl.program_id(0)

        @pl.when(i == 0)
        def _():
            dkv_o[...] = jnp.zeros_like(dkv_o)

        mg = qm_ref[:, 256:512]
        sg = _sig(mg)
        for h in range(HM):
            sl = slice(DH * h, DH * (h + 1))
            vsl = slice(ML + DH * h, ML + DH * (h + 1))
            p = _xattn_probs(qm_ref, kv_ref, h)
            pb = p.astype(BF)
            vh = kv_ref[:, vsl].astype(BF)
            o = _dot(pb, vh)
            dz = dz_ref[:, sl]
            do = dz * mg[:, sl] * sg[:, sl]
            dqm_o[:, 256 + DH * h:256 + DH * (h + 1)] = (dz * o * _dsilu(mg[:, sl], sg[:, sl])).astype(BF)
            dob = do.astype(BF)
            dpv = _dot(dob, vh, NT_DIMS)
            ds = p * (dpv - jnp.sum(do * o, axis=1, keepdims=True))
            dsb = ds.astype(BF)
            dqm_o[:, sl] = (_dot(dsb, kv_ref[:, sl].astype(BF)) * SCALE).astype(BF)
            dkv_o[:, sl] += _dot(dsb, qm_ref[:, sl].astype(BF), TN_DIMS) * SCALE
            dkv_o[:, vsl] += _dot(pb, dob, TN_DIMS)

    return pl.pallas_call(
        body, name="xattn_bwd", grid=(S // ts,),
        out_shape=(jax.ShapeDtypeStruct(dP.shape, BF), jax.ShapeDtypeStruct((ML, 512), F32)),
        in_specs=[pl.BlockSpec((ts, 512), lambda i: (i, C_M // 512)), pl.BlockSpec((ML, 512), lambda i: (0, 0)),
                  pl.BlockSpec((ts, 256), lambda i: (i, 0)), pl.BlockSpec(memory_space=pl.ANY)],
        out_specs=(pl.BlockSpec((ts, 512), lambda i: (i, C_M // 512)), pl.BlockSpec((ML, 512), lambda i: (0, 0))),
        input_output_aliases={3: 0},
        compiler_params=_cp(("arbitrary",)),
    )(P, kv, dzm, dP)


def _merge_fwd(za, zb, zc, zm, P, x, pa, pb, pc, pm, wo, lng, lnb, *, ts):
    S = x.shape[0]

    def body(za_r, zb_r, zc_r, zm_r, g_r, x_r, pa_r, pb_r, pc_r, pm_r, wo_r, lng_r, lnb_r,
             mg_o, ya_o, yb_o, yc_o, ym_o, xn_o, xh_o, rs_o):
        merged = jnp.zeros((ts, D), F32)
        for t, (z_r, p_r, y_o) in enumerate(((za_r, pa_r, ya_o), (zb_r, pb_r, yb_o), (zc_r, pc_r, yc_o), (zm_r, pm_r, ym_o))):
            y = _dot(z_r[...], p_r[...])
            merged = merged + _sig(g_r[:, D * t:D * (t + 1)]) * y
            y_o[...] = y.astype(BF)
        mb = merged.astype(BF)
        mg_o[...] = mb
        r = ALPHA * x_r[...] + _dot(mb, wo_r[...])
        xh, rstd = _ln_hat(r)
        xh_o[...] = xh
        rs_o[...] = rstd
        xn_o[...] = xh * lng_r[...] + lnb_r[...]

    def rows(w):
        return pl.BlockSpec((ts, w), lambda i: (i, 0))

    def full(a):
        return pl.BlockSpec(a.shape, lambda i: (0, 0))

    sd = lambda dt: jax.ShapeDtypeStruct((S, D), dt)
    return pl.pallas_call(
        body, name="merge_fwd", grid=(S // ts,),
        out_shape=(sd(BF), sd(BF), sd(BF), sd(BF), sd(BF), sd(F32), sd(F32), jax.ShapeDtypeStruct((S, 1), F32)),
        in_specs=[rows(512), rows(512), rows(512), rows(256), pl.BlockSpec((ts, 4 * D), lambda i: (i, 0)), rows(D),
                  full(pa), full(pb), full(pc), full(pm), full(wo), full(lng), full(lnb)],
        out_specs=(rows(D),) * 7 + (rows(1),),
        compiler_params=_cp(("parallel",)),
    )(za, zb, zc, zm, P, x, pa, pb, pc, pm, wo, lng, lnb)


def _loss_fwd(y, tgt, *, ts):
    S = y.shape[0]

    def body(y_r, t_r, dy_o, l_o):
        @pl.when(pl.program_id(0) == 0)
        def _():
            l_o[...] = jnp.zeros_like(l_o)

        e = y_r[...] - t_r[...]
        dy_o[...] = e / D
        l_o[...] += 0.5 * jnp.sum(_sum_r(e * e), axis=1, keepdims=True) / D

    rows = pl.BlockSpec((ts, D), lambda i: (i, 0))
    return pl.pallas_call(
        body, name="loss", grid=(S // ts,),
        out_shape=(jax.ShapeDtypeStruct((S, D), F32), jax.ShapeDtypeStruct((1, 1), F32)),
        in_specs=[rows, rows], out_specs=(rows, pl.BlockSpec((1, 1), lambda i: (0, 0))),
        compiler_params=_cp(("arbitrary",)),
    )(y, tgt)


def _out_bwd(dxn, xh, rstd, merged, wo, lng, *, ts):
    S = dxn.shape[0]

    def body(dxn_r, xh_r, rs_r, mg_r, wo_r, lng_r, dr_o, dm_o, dwo_o, dlng_o, dlnb_o):
        @pl.when(pl.program_id(0) == 0)
        def _():
            dwo_o[...] = jnp.zeros_like(dwo_o)
            dlng_o[...] = jnp.zeros_like(dlng_o)
            dlnb_o[...] = jnp.zeros_like(dlnb_o)

        dxn = dxn_r[...]
        xh = xh_r[...]
        dr = _ln_bwd(dxn * lng_r[...], xh, rs_r[...])
        dr_o[...] = dr
        drb = dr.astype(BF)
        dm_o[...] = _dot(drb, wo_r[...], NT_DIMS)
        dwo_o[...] += _dot(mg_r[...], drb, TN_DIMS)
        dlng_o[...] += _sum_r(dxn * xh)
        dlnb_o[...] += _sum_r(dxn)

    rows = pl.BlockSpec((ts, D), lambda i: (i, 0))
    full = lambda shape: pl.BlockSpec(shape, lambda i: (0, 0))
    sd = jax.ShapeDtypeStruct((S, D), F32)
    vec = jax.ShapeDtypeStruct((1, D), F32)
    return pl.pallas_call(
        body, name="out_bwd", grid=(S // ts,),
        out_shape=(sd, sd, jax.ShapeDtypeStruct((D, D), F32), vec, vec),
        in_specs=[rows, rows, pl.BlockSpec((ts, 1), lambda i: (i, 0)), rows, full((D, D)), full((1, D))],
        out_specs=(rows, rows, full((D, D)), full((1, D)), full((1, D))),
        compiler_params=_cp(("arbitrary",)),
    )(dxn, xh, rstd, merged, wo, lng)


def _merge_bwd(dm, P, ya, yb, yc, ym, za, zb, zc, zm, pa, pb, pc, pm, *, ts):
    S = dm.shape[0]

    def body(dm_r, g_r, ya_r, yb_r, yc_r, ym_r, za_r, zb_r, zc_r, zm_r, pa_r, pb_r, pc_r, pm_r,
             dg_o, dza_o, dzb_o, dzc_o, dzm_o, dpa_o, dpb_o, dpc_o, dpm_o):
        @pl.when(pl.program_id(0) == 0)
        def _():
            for o in (dpa_o, dpb_o, dpc_o, dpm_o):
                o[...] = jnp.zeros_like(o)

        dm = dm_r[...]
        for t, (y_r, z_r, p_r, dz_o, dp_o) in enumerate(((ya_r, za_r, pa_r, dza_o, dpa_o), (yb_r, zb_r, pb_r, dzb_o, dpb_o),
                                                        (yc_r, zc_r, pc_r, dzc_o, dpc_o), (ym_r, zm_r, pm_r, dzm_o, dpm_o))):
            gate = _sig(g_r[:, D * t:D * (t + 1)])
            dg_o[:, D * t:D * (t + 1)] = (dm * y_r[...].astype(F32) * gate * (1.0 - gate)).astype(BF)
            dyb = (dm * gate).astype(BF)
            dz_o[...] = _dot(dyb, p_r[...], NT_DIMS)
            dp_o[...] += _dot(z_r[...], dyb, TN_DIMS)

    def rows(w):
        return pl.BlockSpec((ts, w), lambda i: (i, 0))

    def full(a):
        return pl.BlockSpec(a.shape, lambda i: (0, 0))

    return pl.pallas_call(
        body, name="merge_bwd", grid=(S // ts,),
        out_shape=(jax.ShapeDtypeStruct((S, NP), BF),
                   jax.ShapeDtypeStruct((S, 512), F32), jax.ShapeDtypeStruct((S, 512), F32),
                   jax.ShapeDtypeStruct((S, 512), F32), jax.ShapeDtypeStruct((S, 256), F32),
                   jax.ShapeDtypeStruct(pa.shape, F32), jax.ShapeDtypeStruct(pb.shape, F32),
                   jax.ShapeDtypeStruct(pc.shape, F32), jax.ShapeDtypeStruct(pm.shape, F32)),
        in_specs=[rows(D), pl.BlockSpec((ts, 4 * D), lambda i: (i, 0)), rows(D), rows(D), rows(D), rows(D),
                  rows(512), rows(512), rows(512), rows(256), full(pa), full(pb), full(pc), full(pm)],
        out_specs=(pl.BlockSpec((ts, 4 * D), lambda i: (i, 0)), rows(512), rows(512), rows(512), rows(256),
                   full(pa), full(pb), full(pc), full(pm)),
        compiler_params=_cp(("arbitrary",)),
    )(dm, P, ya, yb, yc, ym, za, zb, zc, zm, pa, pb, pc, pm)


def _branch_bwd(P, ca, cb, u, vb, dza, dzb, dzc, oc, wA, gA, betaA, wB, dP, *, ts):
    S = P.shape[0]
    nt = S // ts

    def rev(i):
        return nt - 1 - i

    def rows(w):
        return pl.BlockSpec((ts, w), lambda i: (rev(i), 0))

    def halo(rows_):
        return pl.BlockSpec((rows_, 512), lambda i: (jnp.maximum(rev(i) * (ts // rows_) - 1, 0), 0))

    def full(shape):
        return pl.BlockSpec(shape, lambda i: (0, 0))

    def body(pg, ca_r, cb_r, u_r, vb_r, uh_r, vh_r, dza_r, dzb_r, dzc_r, oc_r, wA_r, gA_r, betaA_r, wB_r, dp_in,
             dpg_o, do_o, dl_o, dwA_o, dbA_o, dgA_o, dbetaA_o, dwB_o, dwinA, uwin, haloA, dwinB, vwin, haloB):
        del dp_in
        i = pl.program_id(0)
        nz = (rev(i) > 0).astype(F32)

        @pl.when(i == 0)
        def _():
            for o in (dwA_o, dbA_o, dgA_o, dbetaA_o, dwB_o, haloA, haloB):
                o[...] = jnp.zeros_like(o)

        def col(j):
            return pg[:, 512 * j:512 * (j + 1)]

        def put(j, val):
            dpg_o[:, 512 * j:512 * (j + 1)] = val.astype(BF)

        a_gate = col(2)
        xh, rstd = _ln_hat(ca_r[...])
        gA_v = gA_r[...]
        n = xh * gA_v + betaA_r[...]
        sn = _sig(n)
        a = n * sn
        sg = _sig(a_gate)
        dza = dza_r[...]
        put(2, dza * a * _dsilu(a_gate, sg))
        dn = dza * a_gate * sg * _dsilu(n, sn)
        dgA_o[...] += _sum_r(dn * xh)
        dbetaA_o[...] += _sum_r(dn)
        dca = _ln_bwd(dn * gA_v, xh, rstd)
        dbA_o[...] += _sum_r(dca)
        dwinA[0:ts, :] = dca
        dwinA[ts:, :] = haloA[...]
        haloA[...] = dca[0:HALO_A, :]
        uwin[0:HALO_A, :] = uh_r[...] * nz
        uwin[HALO_A:, :] = u_r[...]
        du = jnp.zeros((ts, 512), F32)
        for k in range(KA):
            du = du + dwinA[pl.ds(KA - 1 - k, ts), :] * wA_r[k:k + 1, :]
            dwA_o[k:k + 1, :] += _sum_r(dca * uwin[pl.ds(HALO_A - KA + 1 + k, ts), :])
        sv = _sig(col(1))
        put(0, du * sv)
        put(1, du * col(0) * sv * (1.0 - sv))

        b_gate = col(6)
        sgb = _sig(b_gate)
        cbv = cb_r[...]
        b_b = col(4)
        dzb = dzb_r[...]
        put(6, dzb * b_b * cbv * _dsilu(b_gate, sgb))
        dhb = dzb * b_gate * sgb
        put(4, dhb * cbv)
        dcb = dhb * b_b
        dwinB[0:ts, :] = dcb
        dwinB[ts:, :] = haloB[...]
        haloB[...] = dcb[0:HALO_B, :]
        vwin[0:HALO_B, :] = vh_r[...] * nz
        vwin[HALO_B:, :] = vb_r[...]
        dv = jnp.zeros((ts, 512), F32)
        for k in range(KB):
            dv = dv + dwinB[pl.ds(KB - 1 - k, ts), :] * wB_r[k:k + 1, :]
            dwB_o[k:k + 1, :] += _sum_r(dcb * vwin[pl.ds(HALO_B - KB + 1 + k, ts), :])
        put(5, dv * col(3))
        put(3, dv * col(5))

        c_gate = col(7)
        sgc = _sig(c_gate)
        dzc = dzc_r[...]
        ocv = oc_r[...]
        put(7, dzc * ocv * _dsilu(c_gate, sgc))
        do = dzc * c_gate * sgc
        for h in range(HC):
            do_o[h] = do[:, DH * h:DH * (h + 1)].astype(BF)
        dd = do * ocv
        for h in range(HC):
            dl_o[h] = jnp.broadcast_to(jnp.sum(dd[:, DH * h:DH * (h + 1)], axis=1, keepdims=True), (ts, LW))

    v512 = jax.ShapeDtypeStruct((1, 512), F32)
    return pl.pallas_call(
        body, name="branch_bwd", grid=(nt,),
        out_shape=(jax.ShapeDtypeStruct(dP.shape, BF), jax.ShapeDtypeStruct((HC, S, DH), BF), jax.ShapeDtypeStruct((HC, S, LW), F32),
                   jax.ShapeDtypeStruct((32, 512), F32), v512, v512, v512, jax.ShapeDtypeStruct((8, 512), F32)),
        in_specs=[pl.BlockSpec((ts, 4096), lambda i: (rev(i), C_AB // 4096)),
                  rows(512), rows(512), rows(512), rows(512), halo(HALO_A), halo(HALO_B),
                  rows(512), rows(512), rows(512), rows(512),
                  full((32, 512)), full((1, 512)), full((1, 512)), full((8, 512)), pl.BlockSpec(memory_space=pl.ANY)],
        out_specs=(pl.BlockSpec((ts, 4096), lambda i: (rev(i), C_AB // 4096)),
                   pl.BlockSpec((HC, ts, DH), lambda i: (0, rev(i), 0)), pl.BlockSpec((HC, ts, LW), lambda i: (0, rev(i), 0)),
                   full((32, 512)), full((1, 512)), full((1, 512)), full((1, 512)), full((8, 512))),
        scratch_shapes=[pltpu.VMEM((ts + HALO_A, 512), F32), pltpu.VMEM((ts + HALO_A, 512), F32), pltpu.VMEM((HALO_A, 512), F32),
                        pltpu.VMEM((ts + HALO_B, 512), F32), pltpu.VMEM((ts + HALO_B, 512), F32), pltpu.VMEM((HALO_B, 512), F32)],
        input_output_aliases={15: 0},
        compiler_params=_cp(("arbitrary",)),
    )(P, ca, cb, u, vb, u, vb, dza, dzb, dzc, oc, wA, gA, betaA, wB, dP)


def _cum_bwd(P, dcum, bfg, dP, *, ts):
    S = P.shape[0]
    nt = S // ts

    def body(f_ref, dc_ref, bf_r, dp_in, df_o, dbf_o, carry):
        del dp_in
        i = pl.program_id(0)

        @pl.when(i == 0)
        def _():
            carry[...] = jnp.zeros_like(carry)
            dbf_o[...] = jnp.zeros_like(dbf_o)

        r = lax.broadcasted_iota(jnp.int32, (ts, ts), 0)
        c = lax.broadcasted_iota(jnp.int32, (ts, ts), 1)
        tri = (r <= c).astype(F32)
        dlogf = jnp.dot(tri, dc_ref[...], precision=HIGHEST, preferred_element_type=F32) + carry[...]
        carry[...] = dlogf[0:1, :]
        x = f_ref[...] + bf_r[...]
        lane = lax.broadcasted_iota(jnp.int32, (ts, 128), 1)
        df = jnp.where(lane < HC, dlogf * _sig(-x), 0.0)
        df_o[...] = df.astype(BF)
        dbf_o[...] += _sum_r(df)

    blk = pl.BlockSpec((ts, 128), lambda i: (nt - 1 - i, C_F // 128))
    return pl.pallas_call(
        body, name="cum_bwd", grid=(nt,),
        out_shape=(jax.ShapeDtypeStruct(dP.shape, BF), jax.ShapeDtypeStruct((1, 128), F32)),
        in_specs=[blk, pl.BlockSpec((ts, 128), lambda i: (nt - 1 - i, 0)), pl.BlockSpec((1, 128), lambda i: (0, 0)),
                  pl.BlockSpec(memory_space=pl.ANY)],
        out_specs=(blk, pl.BlockSpec((1, 128), lambda i: (0, 0))),
        scratch_shapes=[pltpu.VMEM((1, 128), F32)],
        input_output_aliases={3: 0},
        compiler_params=_cp(("arbitrary",)),
    )(P, dcum, bfg, dP)


def _adamw(w, m, v, gparts, *, name, tr):
    rws, cols = w.shape
    tr = min(tr, rws)
    assert rws % tr == 0 and gparts.shape == (NDEV, rws, cols), (name, w.shape, gparts.shape)
    c1 = 1.0 - ADAM_B1 ** ADAM_STEP
    c2 = 1.0 - ADAM_B2 ** ADAM_STEP

    def body(w_r, m_r, v_r, g_r, g_o, d_o, m_o, v_o):
        g = g_r[0].astype(F32)
        for p in range(1, NDEV):
            g = g + g_r[p].astype(F32)
        mn = ADAM_B1 * m_r[...] + (1.0 - ADAM_B1) * g
        vn = ADAM_B2 * v_r[...] + (1.0 - ADAM_B2) * (g * g)
        g_o[...] = g
        m_o[...] = mn
        v_o[...] = vn
        d_o[...] = -ADAM_LR * ((mn / c1) / (jnp.sqrt(vn / c2) + ADAM_EPS) + ADAM_WD * w_r[...])

    blk = pl.BlockSpec((tr, cols), lambda i: (i, 0))
    shp = jax.ShapeDtypeStruct((rws, cols), F32)
    return pl.pallas_call(
        body, name=name, grid=(rws // tr,), out_shape=(shp,) * 4,
        in_specs=[blk, blk, blk, pl.BlockSpec((NDEV, tr, cols), lambda i: (0, i, 0))],
        out_specs=(blk,) * 4, compiler_params=_cp(("parallel",)),
    )(w, m, v, gparts)


def _slot(p):
    return 4 * p[0] + 2 * p[1] + p[2]


def _all_gather(arrs, *, name):
    na = len(arrs)

    def body(*refs):
        ins, outs = refs[:na], refs[na:2 * na]
        send_sems, recv_sems, local_sems = refs[2 * na:]
        x, y, c = lax.axis_index("x"), lax.axis_index("y"), lax.axis_index("c")
        me, sib = (x, y, c), (x, y, 1 - c)
        chips = [(1 - x, y), (x, 1 - y), (1 - x, 1 - y)]

        def cp(a, k, block, to, src=None):
            dst = outs[a].at[_slot(block)]
            return pltpu.make_async_remote_copy(src_ref=dst if src is None else src, dst_ref=dst,
                                                send_sem=send_sems.at[a, k], recv_sem=recv_sems.at[a, k],
                                                device_id=to, device_id_type=pl.DeviceIdType.MESH)

        mine = [pltpu.make_async_copy(ins[a], outs[a].at[_slot(me)], local_sems.at[a]) for a in range(na)]
        for m in mine:
            m.start()
        first = []
        for a in range(na):
            first.append(cp(a, 0, me, sib, src=ins[a]))
            first += [cp(a, 1 + j, me, (*chip, c), src=ins[a]) for j, chip in enumerate(chips)]
        for f in first:
            f.start()
        passed = []
        for j, chip in enumerate(chips):
            for a in range(na):
                cp(a, 1 + j, (*chip, c), me).wait_recv()
                fwd = cp(a, 4 + j, (*chip, c), sib)
                fwd.start()
                passed.append(fwd)
        for a in range(na):
            cp(a, 0, sib, me).wait_recv()
            for j, chip in enumerate(chips):
                cp(a, 4 + j, (*chip, 1 - c), me).wait_recv()
        for f in first + passed:
            f.wait_send()
        for m in mine:
            m.wait()

    anyspec = pl.BlockSpec(memory_space=pl.ANY)
    return pl.pallas_call(
        body, name=name,
        out_shape=tuple(jax.ShapeDtypeStruct((NDEV,) + a.shape, a.dtype) for a in arrs),
        in_specs=[anyspec] * na, out_specs=(anyspec,) * na,
        scratch_shapes=[pltpu.SemaphoreType.DMA((na, 7)), pltpu.SemaphoreType.DMA((na, 7)), pltpu.SemaphoreType.DMA((na,))],
    )(*arrs)


def _all_to_all(arrs, *, name):
    na = len(arrs)

    def body(*refs):
        ins, outs = refs[:na], refs[na:2 * na]
        send_sems, recv_sems, local_sems = refs[2 * na:]
        x, y, c = lax.axis_index("x"), lax.axis_index("y"), lax.axis_index("c")
        me = (x, y, c)
        peers = [(x ^ ((k >> 2) & 1), y ^ ((k >> 1) & 1), c ^ (k & 1)) for k in range(1, NDEV)]

        def cp(a, k, peer):
            return pltpu.make_async_remote_copy(src_ref=ins[a].at[_slot(peer)], dst_ref=outs[a].at[_slot(me)],
                                                send_sem=send_sems.at[a, k], recv_sem=recv_sems.at[a, k],
                                                device_id=peer, device_id_type=pl.DeviceIdType.MESH)

        def landed(a, k, peer):
            dst = outs[a].at[_slot(peer)]
            return pltpu.make_async_remote_copy(src_ref=dst, dst_ref=dst, send_sem=send_sems.at[a, k], recv_sem=recv_sems.at[a, k],
                                                device_id=peer, device_id_type=pl.DeviceIdType.MESH)

        mine = [pltpu.make_async_copy(ins[a].at[_slot(me)], outs[a].at[_slot(me)], local_sems.at[a]) for a in range(na)]
        for m in mine:
            m.start()
        sends = [cp(a, k, peer) for a in range(na) for k, peer in enumerate(peers)]
        for s in sends:
            s.start()
        for a in range(na):
            for k, peer in enumerate(peers):
                landed(a, k, peer).wait_recv()
        for s in sends:
            s.wait_send()
        for m in mine:
            m.wait()

    anyspec = pl.BlockSpec(memory_space=pl.ANY)
    return pl.pallas_call(
        body, name=name,
        out_shape=tuple(jax.ShapeDtypeStruct(a.shape, a.dtype) for a in arrs),
        in_specs=[anyspec] * na, out_specs=(anyspec,) * na,
        scratch_shapes=[pltpu.SemaphoreType.DMA((na, 7)), pltpu.SemaphoreType.DMA((na, 7)), pltpu.SemaphoreType.DMA((na,))],
    )(*arrs)


def _permute_cols(w):
    parts = [w[..., a:b] for a, b in _RUNS]
    parts.append(jnp.zeros(w.shape[:-1] + (NP - IN_COLS,), w.dtype))
    return jnp.concatenate(parts, axis=-1)


def _unpermute_cols(w):
    off, pieces = 0, []
    for a, b in _RUNS:
        pieces.append((a, w[..., off:off + (b - a)]))
        off += b - a
    return jnp.concatenate([p for _, p in sorted(pieces, key=lambda t: t[0])], axis=-1)


def _tiles(S):
    ts = min(256, S)
    tsb = min(128, S)
    tq = min(512, S)
    return ts, tsb, tq


def _local_step(x, mem, tgt, W, wA, wB, wkv, pa, pb, pc, pm, wo, b_forget, conv_a_b, ln_a_g, ln_a_b, mem_ln_g, mem_ln_b, ln_g, ln_b):
    S = x.shape[0]
    ts, tsb, tq = _tiles(S)
    row = lambda a: a.reshape(1, -1)
    bfp = jnp.pad(b_forget, ((0, 0), (0, 128 - HC)))

    mem_n, mem_hat = _mem_ln_fwd(mem, row(mem_ln_g), row(mem_ln_b))
    saved = []
    for l in range(NL):
        P = _mm(x, W[l], name="proj_fwd", tm=512, tn=1152, tk=D)
        kv = _mm(mem_n, wkv[l], name="kv_fwd", tm=ML, tn=512, tk=D)
        za, zb, u, ca, vb, cb, cum = _pre_fwd(P, wA[l], row(conv_a_b[l]), row(ln_a_g[l]), row(ln_a_b[l]), wB[l], row(bfp[l]), ts=ts)
        Qa, Ka, V = _attn_prep(P, cum, ts=ts)
        oc, zc, lse = _attn_fwd(P, Qa, Ka, V, tq=tq)
        zm = _xattn_fwd(P, kv, ts=tq)
        merged, ya, yb, yc, ym, xn, xh, rstd = _merge_fwd(za, zb, zc, zm, P, x, pa[l], pb[l], pc[l], pm[l], wo[l],
                                                          row(ln_g[l]), row(ln_b[l]), ts=ts)
        saved.append((x, P, kv, za, zb, zc, zm, u, ca, vb, cb, Qa, Ka, V, oc, lse, merged, ya, yb, yc, ym, xh, rstd))
        x = xn

    dx, loss = _loss_fwd(x, tgt, ts=ts)

    g = {k: [None] * NL for k in ("w_in", "b_forget", "conv_a_w", "conv_a_b", "ln_a_g", "ln_a_b", "conv_b_w", "w_kv_mem",
                                  "p_a", "p_b", "p_c", "p_m", "w_out", "ln_g", "ln_b")}
    dmem_n = [None] * NL
    for l in reversed(range(NL)):
        (xl, P, kv, za, zb, zc, zm, u, ca, vb, cb, Qa, Ka, V, oc, lse, merged, ya, yb, yc, ym, xh, rstd) = saved[l]
        dr, dm, g["w_out"][l], g["ln_g"][l], g["ln_b"][l] = _out_bwd(dx, xh, rstd, merged, wo[l], row(ln_g[l]), ts=ts)
        dP, dza, dzb, dzc, dzm, g["p_a"][l], g["p_b"][l], g["p_c"][l], g["p_m"][l] = _merge_bwd(
            dm, P, ya, yb, yc, ym, za, zb, zc, zm, pa[l], pb[l], pc[l], pm[l], ts=tsb)
        dP, do, dlt, dwA, g["conv_a_b"][l], g["ln_a_g"][l], g["ln_a_b"][l], dwB = _branch_bwd(
            P, ca, cb, u, vb, dza, dzb, dzc, oc, wA[l], row(ln_a_g[l]), row(ln_a_b[l]), wB[l], dP, ts=ts)
        g["conv_a_w"][l], g["conv_b_w"][l] = dwA[:KA], dwB[:KB]
        dP, dck = _attn_bwd_dkv(Qa, Ka, V, do, lse[:, :, 0], dlt[:, :, 0], dP, tq=tq)
        dP, dcq = _attn_bwd_dq(Qa, Ka, V, do, lse, dlt, dP, tq=tq)
        dcum = dcq - dck
        dP, dbf = _cum_bwd(P, dcum, row(bfp[l]), dP, ts=ts)
        g["b_forget"][l] = dbf[0, :HC]
        dP, dkv = _xattn_bwd(P, kv, dzm, dP, ts=tq)
        g["w_kv_mem"][l] = _mm(mem_n.T, dkv, name="wkv_bwd", tm=D, tn=512, tk=ML)
        dmem_n[l] = _mm(dkv, wkv[l], name="memn_bwd", nt=True, tm=ML, tn=D, tk=512)
        g["w_in"][l] = _mm(xl.T.astype(BF), dP, name="win_bwd", out_dtype=BF, tm=D, tn=1152, tk=512)
        dx = _mm(dP, W[l], name="x_bwd", nt=True, tm=512, tn=D, tk=1152, add=dr, add_scale=ALPHA)

    g["mem_ln_g"], g["mem_ln_b"] = _mem_ln_bwd(dmem_n, mem_hat)
    out = {k: (jnp.stack(v) if isinstance(v, list) else v) for k, v in g.items()}
    return loss[0, 0], dx, out


_SMALL = (("b_forget", (NL, HC)), ("conv_a_b", (NL, 512)), ("ln_a_g", (NL, 512)), ("ln_a_b", (NL, 512)),
          ("mem_ln_g", (D,)), ("mem_ln_b", (D,)), ("ln_g", (NL, D)), ("ln_b", (NL, D)),
          ("conv_a_w", (NL, KA, 512)), ("conv_b_w", (NL, KB, 512)))


def _pack(parts, rows_mult=8):
    flat = jnp.concatenate([p.reshape(-1).astype(F32) for p in parts])
    n = flat.shape[0]
    rows = -(-n // 128)
    rows = -(-rows // rows_mult) * rows_mult
    return jnp.pad(flat, (0, rows * 128 - n)).reshape(rows, 128)


def _unpack(buf, shapes):
    flat = buf.reshape(-1)
    out, off = [], 0
    for shp in shapes:
        n = 1
        for d in shp:
            n *= d
        out.append(flat[off:off + n].reshape(shp))
        off += n
    return out


def kernel(x, mem, w_in, b_forget, conv_a_w, conv_a_b, ln_a_g, ln_a_b, conv_b_w, w_kv_mem, mem_ln_g, mem_ln_b, p_a, p_b, p_c, p_m, w_out, ln_g, ln_b, loss_target, m_w_in, m_b_forget, m_conv_a_w, m_conv_a_b, m_ln_a_g, m_ln_a_b, m_conv_b_w, m_w_kv_mem, m_mem_ln_g, m_mem_ln_b, m_p_a, m_p_b, m_p_c, m_p_m, m_w_out, m_ln_g, m_ln_b, v_w_in, v_b_forget, v_conv_a_w, v_conv_a_b, v_ln_a_g, v_ln_a_b, v_conv_b_w, v_w_kv_mem, v_mem_ln_g, v_mem_ln_b, v_p_a, v_p_b, v_p_c, v_p_m, v_w_out, v_ln_g, v_ln_b):
    wts = dict(w_in=w_in, b_forget=b_forget, conv_a_w=conv_a_w, conv_a_b=conv_a_b, ln_a_g=ln_a_g, ln_a_b=ln_a_b, conv_b_w=conv_b_w,
               w_kv_mem=w_kv_mem, mem_ln_g=mem_ln_g, mem_ln_b=mem_ln_b, p_a=p_a, p_b=p_b, p_c=p_c, p_m=p_m, w_out=w_out, ln_g=ln_g, ln_b=ln_b)
    mom = dict(w_in=m_w_in, b_forget=m_b_forget, conv_a_w=m_conv_a_w, conv_a_b=m_conv_a_b, ln_a_g=m_ln_a_g, ln_a_b=m_ln_a_b,
               conv_b_w=m_conv_b_w, w_kv_mem=m_w_kv_mem, mem_ln_g=m_mem_ln_g, mem_ln_b=m_mem_ln_b, p_a=m_p_a, p_b=m_p_b, p_c=m_p_c,
               p_m=m_p_m, w_out=m_w_out, ln_g=m_ln_g, ln_b=m_ln_b)
    vel = dict(w_in=v_w_in, b_forget=v_b_forget, conv_a_w=v_conv_a_w, conv_a_b=v_conv_a_b, ln_a_g=v_ln_a_g, ln_a_b=v_ln_a_b,
               conv_b_w=v_conv_b_w, w_kv_mem=v_w_kv_mem, mem_ln_g=v_mem_ln_g, mem_ln_b=v_mem_ln_b, p_a=v_p_a, p_b=v_p_b, p_c=v_p_c,
               p_m=v_p_m, w_out=v_w_out, ln_g=v_ln_g, ln_b=v_ln_b)
    names = ("w_in", "b_forget", "conv_a_w", "conv_a_b", "ln_a_g", "ln_a_b", "conv_b_w", "w_kv_mem", "mem_ln_g", "mem_ln_b",
             "p_a", "p_b", "p_c", "p_m", "w_out", "ln_g", "ln_b")
    mid = ("p_a", "p_b", "p_c", "p_m", "w_out", "w_kv_mem")
    me = 4 * lax.axis_index("x") + 2 * lax.axis_index("y") + lax.axis_index("c")

    def mid_rows(a):
        return a.reshape(-1, 128)

    mid_shapes = [wts[n].shape for n in mid]
    mid_nrows = [mid_rows(wts[n]).shape[0] for n in mid]
    pk16 = jnp.concatenate([mid_rows(wts[n]) for n in mid], axis=0)
    pk32 = jnp.concatenate([conv_a_w, conv_b_w], axis=1).reshape(NL * (KA + KB), 512 // NDEV)
    g_win, g16, g32 = _all_gather([w_in.reshape(NL * D, SHARD_IN).astype(BF), pk16.astype(BF), pk32], name="gather_weights")

    W = _permute_cols(g_win.reshape(NDEV, NL, D, SHARD_IN).transpose(1, 2, 0, 3).reshape(NL, D, IN_COLS))
    full = {}
    off = 0
    for n, shp, nr in zip(mid, mid_shapes, mid_nrows):
        blk = g16[:, off:off + nr].reshape((NDEV,) + shp)
        off += nr
        if n in ("w_out", "w_kv_mem"):
            full[n] = blk.transpose(1, 0, 2, 3).reshape(NL, NDEV * shp[1], shp[2])
        else:
            full[n] = blk.transpose(1, 2, 0, 3).reshape(NL, shp[1], NDEV * shp[2])
    conv = g32.reshape(NDEV, NL, KA + KB, 512 // NDEV).transpose(1, 2, 0, 3).reshape(NL, KA + KB, 512)
    wA = jnp.pad(conv[:, :KA], ((0, 0), (0, 32 - KA), (0, 0)))
    wB = jnp.pad(conv[:, KA:], ((0, 0), (0, 8 - KB), (0, 0)))

    loss, dx, g = _local_step(x[0], mem[0], loss_target[0], W, wA, wB, full["w_kv_mem"], full["p_a"], full["p_b"], full["p_c"],
                              full["p_m"], full["w_out"], b_forget, conv_a_b, ln_a_g, ln_a_b, mem_ln_g, mem_ln_b, ln_g, ln_b)
    loss = lax.psum(loss, ("x", "y", "c"))

    gw = _unpermute_cols(g["w_in"]).reshape(NL, D, NDEV, SHARD_IN).transpose(2, 0, 1, 3).reshape(NDEV, NL * D, SHARD_IN)
    chunks = []
    for n, shp in zip(mid, mid_shapes):
        a = g[n]
        if n in ("w_out", "w_kv_mem"):
            a = a.reshape(NL, NDEV, shp[1], shp[2]).transpose(1, 0, 2, 3)
        else:
            a = a.reshape(NL, shp[1], NDEV, shp[2]).transpose(2, 0, 1, 3)
        chunks.append(a.reshape(NDEV, -1, 128))
    g16s = jnp.concatenate(chunks, axis=1).astype(BF)
    small = _pack([g[n] for n, _ in _SMALL])
    r_win, r16 = _all_to_all([gw, g16s], name="scatter_grads")
    (r_small,) = _all_gather([small], name="gather_small")

    res = {}
    res["w_in"] = [a.reshape(NL, D, SHARD_IN) for a in
                   _adamw(w_in.reshape(NL * D, SHARD_IN), m_w_in.reshape(NL * D, SHARD_IN), v_w_in.reshape(NL * D, SHARD_IN),
                          r_win, name="adamw_w_in", tr=128)]
    pk = lambda d: jnp.concatenate([mid_rows(d[n]) for n in mid], axis=0)
    o16 = _adamw(pk(wts), pk(mom), pk(vel), r16, name="adamw_mid", tr=1024)
    off = 0
    for n, shp, nr in zip(mid, mid_shapes, mid_nrows):
        res[n] = [o[off:off + nr].reshape(shp) for o in o16]
        off += nr

    def small_view(d, n):
        a = d[n]
        if n in ("conv_a_w", "conv_b_w"):
            fullw = jnp.zeros(a.shape[:2] + (512,), F32)
            return lax.dynamic_update_slice(fullw, a, (0, 0, me * (512 // NDEV)))
        return a

    spk = lambda d: _pack([small_view(d, n) for n, _ in _SMALL])
    osm = _adamw(spk(wts), spk(mom), spk(vel), r_small, name="adamw_small", tr=1024)
    osm = [_unpack(o, [s for _, s in _SMALL]) for o in osm]
    for idx, (n, _) in enumerate(_SMALL):
        vals = [o[idx] for o in osm]
        if n in ("conv_a_w", "conv_b_w"):
            vals = [lax.dynamic_slice(a, (0, 0, me * (512 // NDEV)), a.shape[:2] + (512 // NDEV,)) for a in vals]
        res[n] = vals

    outs = [loss, dx[None]]
    for k in range(4):
        outs += [res[n][k] for n in names]
    return tuple(outs)
```

```python
import jax
import jax.numpy as jnp
from jax import lax
from jax.experimental import pallas as pl
from jax.experimental.pallas import tpu as pltpu

F32 = jnp.float32
BF = jnp.bfloat16
HIGHEST = lax.Precision.HIGHEST

D = 1024
NL = 4
NDEV = 8
HC, DH = 8, 64
HM = 4
ML = 256
KA, KB = 31, 3
HALO_A, HALO_B = 32, 8
ALPHA = (2.0 * NL) ** 0.25
EPS = 1e-5
SCALE = DH ** -0.5
NEG = -1e30

ADAM_LR, ADAM_B1, ADAM_B2, ADAM_EPS, ADAM_WD, ADAM_STEP = 0.001, 0.9, 0.999, 1e-08, 0.01, 10

C_G = 0
C_AB = 4096
C_Q = 8192
C_M = 8704
C_KV = 9216
C_F = 10240
NP = 10368
_RUNS = ((6152, 10248), (0, 3584), (5128, 5640), (3584, 4096), (5640, 6152), (4096, 5120), (5120, 5128))
IN_COLS = 10248
SHARD_IN = IN_COLS // NDEV

VMEM_LIMIT = 56 * 1024 * 1024

NT_DIMS = (((1,), (1,)), ((), ()))
TN_DIMS = (((0,), (0,)), ((), ()))


def _cp(sem=None):
    return pltpu.CompilerParams(dimension_semantics=sem, vmem_limit_bytes=VMEM_LIMIT)


def _sig(x):
    return 1.0 / (1.0 + jnp.exp(-x))


def _dsilu(x, s):
    return s * (1.0 + x * (1.0 - s))


def _mean_l(x):
    return jnp.mean(x, axis=-1, keepdims=True)


def _sum_r(x):
    return jnp.sum(x, axis=0, keepdims=True)


def _ln_hat(x):
    mu = _mean_l(x)
    xc = x - mu
    rstd = lax.rsqrt(_mean_l(xc * xc) + EPS)
    return xc * rstd, rstd


def _ln_bwd(dxh, xh, rstd):
    return rstd * (dxh - _mean_l(dxh) - xh * _mean_l(dxh * xh))


def _dot(a, b, dims=None):
    if dims is None:
        return jnp.dot(a, b, preferred_element_type=F32)
    return lax.dot_general(a, b, dims, preferred_element_type=F32)


def _lane_pack(cols, rows):
    lane = lax.broadcasted_iota(jnp.int32, (rows, 128), 1)
    out = jnp.zeros((rows, 128), F32)
    for h, c in enumerate(cols):
        out = jnp.where(lane == h, c, out)
    return out


def _mm(a, b, *, name, nt=False, out_dtype=F32, tm=512, tn=512, tk=512, add=None, add_scale=1.0):
    m, kdim = a.shape
    n = b.shape[0] if nt else b.shape[1]
    tm, tn, tk = min(tm, m), min(tn, n), min(tk, kdim)
    assert m % tm == 0 and n % tn == 0 and kdim % tk == 0, (name, a.shape, b.shape, tm, tn, tk)
    nk = kdim // tk

    def body(*refs):
        a_ref, b_ref = refs[:2]
        add_ref = None if add is None else refs[2]
        o_ref = refs[2 if add is None else 3]

        def finish(r):
            if add is not None:
                r = r + add_scale * add_ref[...]
            o_ref[...] = r.astype(out_dtype)

        part = _dot(a_ref[...].astype(BF), b_ref[...].astype(BF), NT_DIMS if nt else None)
        if nk == 1:
            finish(part)
            return
        acc_ref = refs[-1]
        k = pl.program_id(2)

        @pl.when(k == 0)
        def _():
            acc_ref[...] = part

        @pl.when(k > 0)
        def _():
            acc_ref[...] += part

        @pl.when(k == nk - 1)
        def _():
            finish(acc_ref[...])

    in_specs = [pl.BlockSpec((tm, tk), lambda i, j, k: (i, k)),
                pl.BlockSpec((tn, tk), lambda i, j, k: (j, k)) if nt else pl.BlockSpec((tk, tn), lambda i, j, k: (k, j))]
    args = [a, b]
    if add is not None:
        in_specs.append(pl.BlockSpec((tm, tn), lambda i, j, k: (i, j)))
        args.append(add)
    return pl.pallas_call(
        body, name=name, grid=(m // tm, n // tn, nk),
        out_shape=jax.ShapeDtypeStruct((m, n), out_dtype),
        in_specs=in_specs, out_specs=pl.BlockSpec((tm, tn), lambda i, j, k: (i, j)),
        scratch_shapes=[pltpu.VMEM((tm, tn), F32)] if nk > 1 else [],
        compiler_params=_cp(("parallel", "parallel", "arbitrary")),
    )(*args)


def _mem_ln_fwd(mem, g, b):
    def body(m_ref, g_ref, b_ref, n_ref, h_ref):
        xh, _ = _ln_hat(m_ref[...])
        h_ref[...] = xh
        n_ref[...] = xh * g_ref[...] + b_ref[...]

    shp = jax.ShapeDtypeStruct(mem.shape, F32)
    return pl.pallas_call(body, name="mem_ln_fwd", out_shape=(shp, shp), compiler_params=_cp())(mem, g, b)


def _mem_ln_bwd(dns, mhat):
    def body(*refs):
        d_refs, h_ref, dg_ref, db_ref = refs[:NL], refs[NL], refs[NL + 1], refs[NL + 2]
        dn = d_refs[0][...]
        for r in d_refs[1:]:
            dn = dn + r[...]
        dg_ref[...] = _sum_r(dn * h_ref[...])
        db_ref[...] = _sum_r(dn)

    shp = jax.ShapeDtypeStruct((1, D), F32)
    return pl.pallas_call(body, name="mem_ln_bwd", out_shape=(shp, shp), compiler_params=_cp())(*dns, mhat)


def _pre_fwd(P, wA, bA, gA, betaA, wB, bfg, *, ts):
    S = P.shape[0]
    nt = S // ts
    cb = C_AB // 512

    def cur(j):
        return pl.BlockSpec((ts, 512), lambda i, j=j: (i, cb + j))

    def halo(j, rows):
        return pl.BlockSpec((rows, 512), lambda i, j=j: (jnp.maximum(i * (ts // rows) - 1, 0), cb + j))

    def full(shape):
        return pl.BlockSpec(shape, lambda i: (0, 0))

    def body(au, av, ag, bh, bb, bc, bg, f_ref, au_h, av_h, bh_h, bc_h, wA_r, bA_r, gA_r, betaA_r, wB_r, bf_r,
             za_o, zb_o, u_o, ca_o, vb_o, cb_o, cum_o, winA, winB, carry):
        i = pl.program_id(0)
        nz = (i > 0).astype(F32)

        u = au[...] * _sig(av[...])
        winA[0:HALO_A, :] = au_h[...] * _sig(av_h[...]) * nz
        winA[HALO_A:, :] = u
        acc = jnp.zeros((ts, 512), F32)
        for k in range(KA):
            acc = acc + winA[pl.ds(HALO_A - KA + 1 + k, ts), :] * wA_r[k:k + 1, :]
        ca = acc + bA_r[...]
        xh, _ = _ln_hat(ca)
        n = xh * gA_r[...] + betaA_r[...]
        a = n * _sig(n)
        agv = ag[...]
        za_o[...] = (a * agv * _sig(agv)).astype(BF)
        u_o[...] = u
        ca_o[...] = ca

        vb = bc[...] * bh[...]
        winB[0:HALO_B, :] = bc_h[...] * bh_h[...] * nz
        winB[HALO_B:, :] = vb
        accb = jnp.zeros((ts, 512), F32)
        for k in range(KB):
            accb = accb + winB[pl.ds(HALO_B - KB + 1 + k, ts), :] * wB_r[k:k + 1, :]
        bgv = bg[...]
        zb_o[...] = (bb[...] * accb * bgv * _sig(bgv)).astype(BF)
        vb_o[...] = vb
        cb_o[...] = accb

        @pl.when(i == 0)
        def _():
            carry[...] = jnp.zeros_like(carry)

        x = f_ref[...] + bf_r[...]
        logf = jnp.minimum(x, 0.0) - jnp.log1p(jnp.exp(-jnp.abs(x)))
        r = lax.broadcasted_iota(jnp.int32, (ts, ts), 0)
        c = lax.broadcasted_iota(jnp.int32, (ts, ts), 1)
        tri = (r >= c).astype(F32)
        cum = jnp.dot(tri, logf, precision=HIGHEST, preferred_element_type=F32) + carry[...]
        cum_o[...] = cum
        carry[...] = cum[ts - 1:ts, :]

    s512 = lambda dt: jax.ShapeDtypeStruct((S, 512), dt)
    o512 = pl.BlockSpec((ts, 512), lambda i: (i, 0))
    return pl.pallas_call(
        body, name="pre_fwd", grid=(nt,),
        out_shape=(s512(BF), s512(BF), s512(F32), s512(F32), s512(F32), s512(F32), jax.ShapeDtypeStruct((S, 128), F32)),
        in_specs=[cur(0), cur(1), cur(2), cur(3), cur(4), cur(5), cur(6),
                  pl.BlockSpec((ts, 128), lambda i: (i, C_F // 128)),
                  halo(0, HALO_A), halo(1, HALO_A), halo(3, HALO_B), halo(5, HALO_B),
                  full((32, 512)), full((1, 512)), full((1, 512)), full((1, 512)), full((8, 512)), full((1, 128))],
        out_specs=(o512, o512, o512, o512, o512, o512, pl.BlockSpec((ts, 128), lambda i: (i, 0))),
        scratch_shapes=[pltpu.VMEM((ts + HALO_A, 512), F32), pltpu.VMEM((ts + HALO_B, 512), F32), pltpu.VMEM((1, 128), F32)],
        compiler_params=_cp(("arbitrary",)),
    )(P, P, P, P, P, P, P, P, P, P, P, P, wA, bA, gA, betaA, wB, bfg)


RC = 32


def _split3(c):
    c1 = c.astype(BF).astype(F32)
    r = c - c1
    c2 = r.astype(BF).astype(F32)
    return c1, c2, r - c2


def _attn_prep(P, cum, *, ts):
    S = P.shape[0]

    def body(q_ref, kv_ref, cum_ref, qa_o, ka_o, v_o):
        lane = lax.broadcasted_iota(jnp.int32, (ts, DH), 1)
        for h in range(HC):
            sl = slice(DH * h, DH * (h + 1))
            c1, c2, c3 = _split3(cum_ref[:, h:h + 1])
            lo = jnp.where(lane == 0, c1, jnp.where(lane == 1, c2, jnp.where(lane == 2, c3, 0.0)))
            hi = jnp.where(lane == 3, c1, jnp.where(lane == 4, c2, jnp.where(lane == 5, c3, 0.0)))
            qa_o[h, :, 0:DH] = (q_ref[:, sl] * SCALE).astype(BF)
            qa_o[h, :, DH:2 * DH] = (lo + jnp.where((lane >= 3) & (lane < 6), 1.0, 0.0)).astype(BF)
            ka_o[h, :, 0:DH] = kv_ref[:, sl].astype(BF)
            ka_o[h, :, DH:2 * DH] = (jnp.where(lane < 3, 1.0, 0.0) - hi).astype(BF)
            v_o[h] = kv_ref[:, 512 + DH * h:512 + DH * (h + 1)].astype(BF)

    aug = jax.ShapeDtypeStruct((HC, S, 2 * DH), BF)
    return pl.pallas_call(
        body, name="attn_prep", grid=(S // ts,),
        out_shape=(aug, aug, jax.ShapeDtypeStruct((HC, S, DH), BF)),
        in_specs=[pl.BlockSpec((ts, 512), lambda i: (i, C_Q // 512)), pl.BlockSpec((ts, 1024), lambda i: (i, C_KV // 1024)),
                  pl.BlockSpec((ts, 128), lambda i: (i, 0))],
        out_specs=(pl.BlockSpec((HC, ts, 2 * DH), lambda i: (0, i, 0)), pl.BlockSpec((HC, ts, 2 * DH), lambda i: (0, i, 0)),
                   pl.BlockSpec((HC, ts, DH), lambda i: (0, i, 0))),
        compiler_params=_cp(("parallel",)),
    )(P, P, cum)


LW = 128


def _lanes(c):
    return slice(LW * c, LW * (c + 1))


def _diag_slices(rc, n, rows_are_queries):
    out = []
    for c in range(n // LW):
        r0, r1, c0, c1 = rc * RC, rc * RC + RC - 1, LW * c, LW * c + LW - 1
        lo, hi = (c1 <= r0, c0 > r1) if rows_are_queries else (r1 <= c0, r0 > c1)
        if lo:
            out.append("all")
        elif hi:
            out.append("none")
        else:
            r = lax.broadcasted_iota(jnp.int32, (RC, LW), 0) + r0
            cc = lax.broadcasted_iota(jnp.int32, (RC, LW), 1) + c0
            out.append((r >= cc) if rows_are_queries else (cc >= r))
    return out


def _pieces(ref2d, rows, rc, n, masked, rows_are_queries):
    kinds = _diag_slices(rc, n, rows_are_queries) if masked else ["all"] * (n // LW)
    out = []
    for c, kind in enumerate(kinds):
        if isinstance(kind, str):
            out.append(ref2d[rows, _lanes(c)] if kind == "all" else None)
        else:
            out.append(jnp.where(kind, ref2d[rows, _lanes(c)], NEG))
    return out


def _attn_fwd(P, Qa, Ka, V, *, tq):
    S = P.shape[0]
    nq = S // tq

    def body(qa_ref, ka_ref, v_ref, cg_ref, o_ref, zc_ref, lse_ref, s_s, p_s, m_s, l_s, acc_s, pm_s, al_s):
        i, j = pl.program_id(0), pl.program_id(1)

        @pl.when(j == 0)
        def _():
            m_s[...] = jnp.full_like(m_s, NEG)
            l_s[...] = jnp.zeros_like(l_s)
            acc_s[...] = jnp.zeros_like(acc_s)

        def step(masked):
            s_s[0] = _dot(qa_ref[0], ka_ref[0], NT_DIMS)
            for h in range(HC):
                b = h % 2
                if h + 1 < HC:
                    s_s[1 - b] = _dot(qa_ref[h + 1], ka_ref[h + 1], NT_DIMS)
                for rc in range(tq // RC):
                    rows = slice(rc * RC, (rc + 1) * RC)
                    pm = None
                    for sc in _pieces(s_s.at[b], rows, rc, tq, masked, True):
                        if sc is not None:
                            pm = sc if pm is None else jnp.maximum(pm, sc)
                    pm_s[rows, :] = pm
                m_prev = m_s[h]
                m_new = jnp.maximum(m_prev, jnp.max(pm_s[...], axis=1, keepdims=True))
                alpha = jnp.exp(m_prev - m_new)
                m_s[h] = m_new
                al_s[...] = alpha
                for rc in range(tq // RC):
                    rows = slice(rc * RC, (rc + 1) * RC)
                    mb = m_s[h, rows]
                    ps = None
                    for c, sc in enumerate(_pieces(s_s.at[b], rows, rc, tq, masked, True)):
                        if sc is None:
                            p_s[b, rows, _lanes(c)] = jnp.zeros((RC, LW), BF)
                            continue
                        p = jnp.exp(sc - mb)
                        ps = p if ps is None else ps + p
                        p_s[b, rows, _lanes(c)] = p.astype(BF)
                    l_s[h, rows] = al_s[rows] * l_s[h, rows] + ps
                acc_s[h] = al_s[:, 0:DH] * acc_s[h] + _dot(p_s[b], v_ref[h])

        @pl.when(j < i)
        def _():
            step(False)

        @pl.when(j == i)
        def _():
            step(True)
            for h in range(HC):
                l = jnp.sum(l_s[h], axis=1, keepdims=True)
                o_ref[:, DH * h:DH * (h + 1)] = acc_s[h] / l
                lse_ref[h] = m_s[h] + jnp.log(l)
            cg = cg_ref[...]
            zc_ref[...] = (o_ref[...] * cg * _sig(cg)).astype(BF)

    stat = pltpu.VMEM((HC, tq, LW), F32)
    return pl.pallas_call(
        body, name="attn_fwd", grid=(nq, nq),
        out_shape=(jax.ShapeDtypeStruct((S, 512), F32), jax.ShapeDtypeStruct((S, 512), BF), jax.ShapeDtypeStruct((HC, S, LW), F32)),
        in_specs=[pl.BlockSpec((HC, tq, 2 * DH), lambda i, j: (0, i, 0)),
                  pl.BlockSpec((HC, tq, 2 * DH), lambda i, j: (0, jnp.minimum(i, j), 0)),
                  pl.BlockSpec((HC, tq, DH), lambda i, j: (0, jnp.minimum(i, j), 0)),
                  pl.BlockSpec((tq, 512), lambda i, j: (i, C_AB // 512 + 7))],
        out_specs=(pl.BlockSpec((tq, 512), lambda i, j: (i, 0)), pl.BlockSpec((tq, 512), lambda i, j: (i, 0)),
                   pl.BlockSpec((HC, tq, LW), lambda i, j: (0, i, 0))),
        scratch_shapes=[pltpu.VMEM((2, tq, tq), F32), pltpu.VMEM((2, tq, tq), BF), stat, stat, pltpu.VMEM((HC, tq, DH), F32),
                        pltpu.VMEM((tq, LW), F32), pltpu.VMEM((tq, LW), F32)],
        compiler_params=_cp(("parallel", "arbitrary")),
    )(Qa, Ka, V, P)


def _attn_bwd_dkv(Qa, Ka, V, dO, lseT, dltT, dP, *, tq):
    S = Qa.shape[1]
    nq = S // tq

    def body(qa_ref, ka_ref, v_ref, do_ref, lse_ref, dl_ref, dp_in, dkv_o, dck_o, s_s, dp_s, p_s, ds_s, dk_s, dv_s, dck_s):
        del dp_in
        j, i = pl.program_id(0), pl.program_id(1)

        @pl.when(i == 0)
        def _():
            dk_s[...] = jnp.zeros_like(dk_s)
            dv_s[...] = jnp.zeros_like(dv_s)
            dck_s[...] = jnp.zeros_like(dck_s)

        def mm(h, b):
            s_s[b] = _dot(ka_ref[h], qa_ref[h], NT_DIMS)
            dp_s[b] = _dot(v_ref[h], do_ref[h], NT_DIMS)

        def step(masked):
            mm(0, 0)
            for h in range(HC):
                b = h % 2
                if h + 1 < HC:
                    mm(h + 1, 1 - b)
                for rc in range(tq // RC):
                    rows = slice(rc * RC, (rc + 1) * RC)
                    acc = None
                    for c, sc in enumerate(_pieces(s_s.at[b], rows, rc, tq, masked, False)):
                        if sc is None:
                            p_s[rows, _lanes(c)] = jnp.zeros((RC, LW), BF)
                            ds_s[rows, _lanes(c)] = jnp.zeros((RC, LW), BF)
                            continue
                        p = jnp.exp(sc - lse_ref[h:h + 1, _lanes(c)])
                        ds = p * (dp_s[b, rows, _lanes(c)] - dl_ref[h:h + 1, _lanes(c)])
                        p_s[rows, _lanes(c)] = p.astype(BF)
                        ds_s[rows, _lanes(c)] = ds.astype(BF)
                        acc = ds if acc is None else acc + ds
                    dck_s[h, rows] += acc
                dv_s[h] += _dot(p_s[...], do_ref[h])
                dk_s[h] += _dot(ds_s[...], qa_ref[h])

        @pl.when(i > j)
        def _():
            step(False)

        @pl.when(i == j)
        def _():
            step(True)

        @pl.when(i == nq - 1)
        def _():
            for h in range(HC):
                dkv_o[:, DH * h:DH * (h + 1)] = dk_s[h][:, 0:DH].astype(BF)
                dkv_o[:, 512 + DH * h:512 + DH * (h + 1)] = dv_s[h].astype(BF)
            dck_o[...] = _lane_pack([jnp.sum(dck_s[h], axis=1, keepdims=True) for h in range(HC)], tq)

    def qspec(w):
        return pl.BlockSpec((HC, tq, w), lambda j, i: (0, jnp.maximum(i, j), 0))

    def kspec(w):
        return pl.BlockSpec((HC, tq, w), lambda j, i: (0, j, 0))

    rowv = pl.BlockSpec((8, tq), lambda j, i: (0, jnp.maximum(i, j)))
    return pl.pallas_call(
        body, name="attn_bwd_dkv", grid=(nq, nq),
        out_shape=(jax.ShapeDtypeStruct(dP.shape, BF), jax.ShapeDtypeStruct((S, 128), F32)),
        in_specs=[qspec(2 * DH), kspec(2 * DH), kspec(DH), qspec(DH), rowv, rowv, pl.BlockSpec(memory_space=pl.ANY)],
        out_specs=(pl.BlockSpec((tq, 1024), lambda j, i: (j, C_KV // 1024)), pl.BlockSpec((tq, 128), lambda j, i: (j, 0))),
        scratch_shapes=[pltpu.VMEM((2, tq, tq), F32), pltpu.VMEM((2, tq, tq), F32), pltpu.VMEM((tq, tq), BF), pltpu.VMEM((tq, tq), BF),
                        pltpu.VMEM((HC, tq, 2 * DH), F32), pltpu.VMEM((HC, tq, DH), F32), pltpu.VMEM((HC, tq, LW), F32)],
        input_output_aliases={6: 0},
        compiler_params=_cp(("parallel", "arbitrary")),
    )(Qa, Ka, V, dO, lseT, dltT, dP)


def _attn_bwd_dq(Qa, Ka, V, dO, lse, dlt, dP, *, tq):
    S = Qa.shape[1]
    nq = S // tq

    def body(qa_ref, ka_ref, v_ref, do_ref, lse_ref, dl_ref, dp_in, dq_o, dcq_o, s_s, dp_s, ds_s, dq_s, dcq_s):
        del dp_in
        i, j = pl.program_id(0), pl.program_id(1)

        @pl.when(j == 0)
        def _():
            dq_s[...] = jnp.zeros_like(dq_s)
            dcq_s[...] = jnp.zeros_like(dcq_s)

        def mm(h, b):
            s_s[b] = _dot(qa_ref[h], ka_ref[h], NT_DIMS)
            dp_s[b] = _dot(do_ref[h], v_ref[h], NT_DIMS)

        def step(masked):
            mm(0, 0)
            for h in range(HC):
                b = h % 2
                if h + 1 < HC:
                    mm(h + 1, 1 - b)
                for rc in range(tq // RC):
                    rows = slice(rc * RC, (rc + 1) * RC)
                    lb = lse_ref[h, rows]
                    db = dl_ref[h, rows]
                    acc = None
                    for c, sc in enumerate(_pieces(s_s.at[b], rows, rc, tq, masked, True)):
                        if sc is None:
                            ds_s[rows, _lanes(c)] = jnp.zeros((RC, LW), BF)
                            continue
                        ds = jnp.exp(sc - lb) * (dp_s[b, rows, _lanes(c)] - db)
                        ds_s[rows, _lanes(c)] = ds.astype(BF)
                        acc = ds if acc is None else acc + ds
                    dcq_s[h, rows] += acc
                dq_s[h] += _dot(ds_s[...], ka_ref[h])

        @pl.when(j < i)
        def _():
            step(False)

        @pl.when(j == i)
        def _():
            step(True)
            for h in range(HC):
                dq_o[:, DH * h:DH * (h + 1)] = (dq_s[h][:, 0:DH] * SCALE).astype(BF)
            dcq_o[...] = _lane_pack([jnp.sum(dcq_s[h], axis=1, keepdims=True) for h in range(HC)], tq)

    def qspec(w):
        return pl.BlockSpec((HC, tq, w), lambda i, j: (0, i, 0))

    def kspec(w):
        return pl.BlockSpec((HC, tq, w), lambda i, j: (0, jnp.minimum(i, j), 0))

    colv = pl.BlockSpec((tq, 128), lambda i, j: (i, 0))
    return pl.pallas_call(
        body, name="attn_bwd_dq", grid=(nq, nq),
        out_shape=(jax.ShapeDtypeStruct(dP.shape, BF), jax.ShapeDtypeStruct((S, 128), F32)),
        in_specs=[qspec(2 * DH), kspec(2 * DH), kspec(DH), qspec(DH), qspec(LW), qspec(LW), pl.BlockSpec(memory_space=pl.ANY)],
        out_specs=(pl.BlockSpec((tq, 512), lambda i, j: (i, C_Q // 512)), colv),
        scratch_shapes=[pltpu.VMEM((2, tq, tq), F32), pltpu.VMEM((2, tq, tq), F32), pltpu.VMEM((tq, tq), BF),
                        pltpu.VMEM((HC, tq, 2 * DH), F32), pltpu.VMEM((HC, tq, LW), F32)],
        input_output_aliases={6: 0},
        compiler_params=_cp(("parallel", "arbitrary")),
    )(Qa, Ka, V, dO, lse, dlt, dP)


def _xattn_probs(qm_ref, kv_ref, h):
    sl = slice(DH * h, DH * (h + 1))
    s = _dot(qm_ref[:, sl].astype(BF), kv_ref[:, sl].astype(BF), NT_DIMS) * SCALE
    p = jnp.exp(s - jnp.max(s, axis=1, keepdims=True))
    return p / jnp.sum(p, axis=1, keepdims=True)


def _xattn_fwd(P, kv, *, ts):
    S = P.shape[0]

    def body(qm_ref, kv_ref, zm_o, o_s):
        for h in range(HM):
            sl = slice(DH * h, DH * (h + 1))
            p = _xattn_probs(qm_ref, kv_ref, h)
            o_s[:, sl] = _dot(p.astype(BF), kv_ref[:, ML + DH * h:ML + DH * (h + 1)].astype(BF))
        mg = qm_ref[:, 256:512]
        zm_o[...] = (o_s[...] * mg * _sig(mg)).astype(BF)

    return pl.pallas_call(
        body, name="xattn_fwd", grid=(S // ts,),
        out_shape=jax.ShapeDtypeStruct((S, 256), BF),
        in_specs=[pl.BlockSpec((ts, 512), lambda i: (i, C_M // 512)), pl.BlockSpec((ML, 512), lambda i: (0, 0))],
        out_specs=pl.BlockSpec((ts, 256), lambda i: (i, 0)),
        scratch_shapes=[pltpu.VMEM((ts, 256), F32)],
        compiler_params=_cp(("parallel",)),
    )(P, kv)


def _xattn_bwd(P, kv, dzm, dP, *, ts):
    S = P.shape[0]

    def body(qm_ref, kv_ref, dz_ref, dp_in, dqm_o, dkv_o):
        del dp_in
        i = pl.program_id(0)

        @pl.when(i == 0)
        def _():
            dkv_o[...] = jnp.zeros_like(dkv_o)

        mg = qm_ref[:, 256:512]
        sg = _sig(mg)
        for h in range(HM):
            sl = slice(DH * h, DH * (h + 1))
            vsl = slice(ML + DH * h, ML + DH * (h + 1))
            p = _xattn_probs(qm_ref, kv_ref, h)
            pb = p.astype(BF)
            vh = kv_ref[:, vsl].astype(BF)
            o = _dot(pb, vh)
            dz = dz_ref[:, sl]
            do = dz * mg[:, sl] * sg[:, sl]
            dqm_o[:, 256 + DH * h:256 + DH * (h + 1)] = (dz * o * _dsilu(mg[:, sl], sg[:, sl])).astype(BF)
            dob = do.astype(BF)
            dpv = _dot(dob, vh, NT_DIMS)
            ds = p * (dpv - jnp.sum(do * o, axis=1, keepdims=True))
            dsb = ds.astype(BF)
            dqm_o[:, sl] = (_dot(dsb, kv_ref[:, sl].astype(BF)) * SCALE).astype(BF)
            dkv_o[:, sl] += _dot(dsb, qm_ref[:, sl].astype(BF), TN_DIMS) * SCALE
            dkv_o[:, vsl] += _dot(pb, dob, TN_DIMS)

    return pl.pallas_call(
        body, name="xattn_bwd", grid=(S // ts,),
        out_shape=(jax.ShapeDtypeStruct(dP.shape, BF), jax.ShapeDtypeStruct((ML, 512), F32)),
        in_specs=[pl.BlockSpec((ts, 512), lambda i: (i, C_M // 512)), pl.BlockSpec((ML, 512), lambda i: (0, 0)),
                  pl.BlockSpec((ts, 256), lambda i: (i, 0)), pl.BlockSpec(memory_space=pl.ANY)],
        out_specs=(pl.BlockSpec((ts, 512), lambda i: (i, C_M // 512)), pl.BlockSpec((ML, 512), lambda i: (0, 0))),
        input_output_aliases={3: 0},
        compiler_params=_cp(("arbitrary",)),
    )(P, kv, dzm, dP)


def _merge_fwd(za, zb, zc, zm, P, x, pa, pb, pc, pm, wo, lng, lnb, *, ts):
    S = x.shape[0]

    def body(za_r, zb_r, zc_r, zm_r, g_r, x_r, pa_r, pb_r, pc_r, pm_r, wo_r, lng_r, lnb_r,
             mg_o, ya_o, yb_o, yc_o, ym_o, xn_o, xh_o, rs_o):
        merged = jnp.zeros((ts, D), F32)
        for t, (z_r, p_r, y_o) in enumerate(((za_r, pa_r, ya_o), (zb_r, pb_r, yb_o), (zc_r, pc_r, yc_o), (zm_r, pm_r, ym_o))):
            y = _dot(z_r[...], p_r[...])
            merged = merged + _sig(g_r[:, D * t:D * (t + 1)]) * y
            y_o[...] = y.astype(BF)
        mb = merged.astype(BF)
        mg_o[...] = mb
        r = ALPHA * x_r[...] + _dot(mb, wo_r[...])
        xh, rstd = _ln_hat(r)
        xh_o[...] = xh
        rs_o[...] = rstd
        xn_o[...] = xh * lng_r[...] + lnb_r[...]

    def rows(w):
        return pl.BlockSpec((ts, w), lambda i: (i, 0))

    def full(a):
        return pl.BlockSpec(a.shape, lambda i: (0, 0))

    sd = lambda dt: jax.ShapeDtypeStruct((S, D), dt)
    return pl.pallas_call(
        body, name="merge_fwd", grid=(S // ts,),
        out_shape=(sd(BF), sd(BF), sd(BF), sd(BF), sd(BF), sd(F32), sd(F32), jax.ShapeDtypeStruct((S, 1), F32)),
        in_specs=[rows(512), rows(512), rows(512), rows(256), pl.BlockSpec((ts, 4 * D), lambda i: (i, 0)), rows(D),
                  full(pa), full(pb), full(pc), full(pm), full(wo), full(lng), full(lnb)],
        out_specs=(rows(D),) * 7 + (rows(1),),
        compiler_params=_cp(("parallel",)),
    )(za, zb, zc, zm, P, x, pa, pb, pc, pm, wo, lng, lnb)


def _loss_fwd(y, tgt, *, ts):
    S = y.shape[0]

    def body(y_r, t_r, dy_o, l_o):
        @pl.when(pl.program_id(0) == 0)
        def _():
            l_o[...] = jnp.zeros_like(l_o)

        e = y_r[...] - t_r[...]
        dy_o[...] = e / D
        l_o[...] += 0.5 * jnp.sum(_sum_r(e * e), axis=1, keepdims=True) / D

    rows = pl.BlockSpec((ts, D), lambda i: (i, 0))
    return pl.pallas_call(
        body, name="loss", grid=(S // ts,),
        out_shape=(jax.ShapeDtypeStruct((S, D), F32), jax.ShapeDtypeStruct((1, 1), F32)),
        in_specs=[rows, rows], out_specs=(rows, pl.BlockSpec((1, 1), lambda i: (0, 0))),
        compiler_params=_cp(("arbitrary",)),
    )(y, tgt)


def _out_bwd(dxn, xh, rstd, merged, wo, lng, *, ts):
    S = dxn.shape[0]

    def body(dxn_r, xh_r, rs_r, mg_r, wo_r, lng_r, dr_o, dm_o, dwo_o, dlng_o, dlnb_o):
        @pl.when(pl.program_id(0) == 0)
        def _():
            dwo_o[...] = jnp.zeros_like(dwo_o)
            dlng_o[...] = jnp.zeros_like(dlng_o)
            dlnb_o[...] = jnp.zeros_like(dlnb_o)

        dxn = dxn_r[...]
        xh = xh_r[...]
        dr = _ln_bwd(dxn * lng_r[...], xh, rs_r[...])
        dr_o[...] = dr
        drb = dr.astype(BF)
        dm_o[...] = _dot(drb, wo_r[...], NT_DIMS)
        dwo_o[...] += _dot(mg_r[...], drb, TN_DIMS)
        dlng_o[...] += _sum_r(dxn * xh)
        dlnb_o[...] += _sum_r(dxn)

    rows = pl.BlockSpec((ts, D), lambda i: (i, 0))
    full = lambda shape: pl.BlockSpec(shape, lambda i: (0, 0))
    sd = jax.ShapeDtypeStruct((S, D), F32)
    vec = jax.ShapeDtypeStruct((1, D), F32)
    return pl.pallas_call(
        body, name="out_bwd", grid=(S // ts,),
        out_shape=(sd, sd, jax.ShapeDtypeStruct((D, D), F32), vec, vec),
        in_specs=[rows, rows, pl.BlockSpec((ts, 1), lambda i: (i, 0)), rows, full((D, D)), full((1, D))],
        out_specs=(rows, rows, full((D, D)), full((1, D)), full((1, D))),
        compiler_params=_cp(("arbitrary",)),
    )(dxn, xh, rstd, merged, wo, lng)


def _merge_bwd(dm, P, ya, yb, yc, ym, za, zb, zc, zm, pa, pb, pc, pm, *, ts):
    S = dm.shape[0]

    def body(dm_r, g_r, ya_r, yb_r, yc_r, ym_r, za_r, zb_r, zc_r, zm_r, pa_r, pb_r, pc_r, pm_r,
             dg_o, dza_o, dzb_o, dzc_o, dzm_o, dpa_o, dpb_o, dpc_o, dpm_o):
        @pl.when(pl.program_id(0) == 0)
        def _():
            for o in (dpa_o, dpb_o, dpc_o, dpm_o):
                o[...] = jnp.zeros_like(o)

        dm = dm_r[...]
        for t, (y_r, z_r, p_r, dz_o, dp_o) in enumerate(((ya_r, za_r, pa_r, dza_o, dpa_o), (yb_r, zb_r, pb_r, dzb_o, dpb_o),
                                                        (yc_r, zc_r, pc_r, dzc_o, dpc_o), (ym_r, zm_r, pm_r, dzm_o, dpm_o))):
            gate = _sig(g_r[:, D * t:D * (t + 1)])
            dg_o[:, D * t:D * (t + 1)] = (dm * y_r[...].astype(F32) * gate * (1.0 - gate)).astype(BF)
            dyb = (dm * gate).astype(BF)
            dz_o[...] = _dot(dyb, p_r[...], NT_DIMS)
            dp_o[...] += _dot(z_r[...], dyb, TN_DIMS)

    def rows(w):
        return pl.BlockSpec((ts, w), lambda i: (i, 0))

    def full(a):
        return pl.BlockSpec(a.shape, lambda i: (0, 0))

    return pl.pallas_call(
        body, name="merge_bwd", grid=(S // ts,),
        out_shape=(jax.ShapeDtypeStruct((S, NP), BF),
                   jax.ShapeDtypeStruct((S, 512), F32), jax.ShapeDtypeStruct((S, 512), F32),
                   jax.ShapeDtypeStruct((S, 512), F32), jax.ShapeDtypeStruct((S, 256), F32),
                   jax.ShapeDtypeStruct(pa.shape, F32), jax.ShapeDtypeStruct(pb.shape, F32),
                   jax.ShapeDtypeStruct(pc.shape, F32), jax.ShapeDtypeStruct(pm.shape, F32)),
        in_specs=[rows(D), pl.BlockSpec((ts, 4 * D), lambda i: (i, 0)), rows(D), rows(D), rows(D), rows(D),
                  rows(512), rows(512), rows(512), rows(256), full(pa), full(pb), full(pc), full(pm)],
        out_specs=(pl.BlockSpec((ts, 4 * D), lambda i: (i, 0)), rows(512), rows(512), rows(512), rows(256),
                   full(pa), full(pb), full(pc), full(pm)),
        compiler_params=_cp(("arbitrary",)),
    )(dm, P, ya, yb, yc, ym, za, zb, zc, zm, pa, pb, pc, pm)


def _branch_bwd(P, ca, cb, u, vb, dza, dzb, dzc, oc, wA, gA, betaA, wB, dP, *, ts):
    S = P.shape[0]
    nt = S // ts

    def rev(i):
        return nt - 1 - i

    def rows(w):
        return pl.BlockSpec((ts, w), lambda i: (rev(i), 0))

    def halo(rows_):
        return pl.BlockSpec((rows_, 512), lambda i: (jnp.maximum(rev(i) * (ts // rows_) - 1, 0), 0))

    def full(shape):
        return pl.BlockSpec(shape, lambda i: (0, 0))

    def body(pg, ca_r, cb_r, u_r, vb_r, uh_r, vh_r, dza_r, dzb_r, dzc_r, oc_r, wA_r, gA_r, betaA_r, wB_r, dp_in,
             dpg_o, do_o, dl_o, dwA_o, dbA_o, dgA_o, dbetaA_o, dwB_o, dwinA, uwin, haloA, dwinB, vwin, haloB):
        del dp_in
        i = pl.program_id(0)
        nz = (rev(i) > 0).astype(F32)

        @pl.when(i == 0)
        def _():
            for o in (dwA_o, dbA_o, dgA_o, dbetaA_o, dwB_o, haloA, haloB):
                o[...] = jnp.zeros_like(o)

        def col(j):
            return pg[:, 512 * j:512 * (j + 1)]

        def put(j, val):
            dpg_o[:, 512 * j:512 * (j + 1)] = val.astype(BF)

        a_gate = col(2)
        xh, rstd = _ln_hat(ca_r[...])
        gA_v = gA_r[...]
        n = xh * gA_v + betaA_r[...]
        sn = _sig(n)
        a = n * sn
        sg = _sig(a_gate)
        dza = dza_r[...]
        put(2, dza * a * _dsilu(a_gate, sg))
        dn = dza * a_gate * sg * _dsilu(n, sn)
        dgA_o[...] += _sum_r(dn * xh)
        dbetaA_o[...] += _sum_r(dn)
        dca = _ln_bwd(dn * gA_v, xh, rstd)
        dbA_o[...] += _sum_r(dca)
        dwinA[0:ts, :] = dca
        dwinA[ts:, :] = haloA[...]
        haloA[...] = dca[0:HALO_A, :]
        uwin[0:HALO_A, :] = uh_r[...] * nz
        uwin[HALO_A:, :] = u_r[...]
        du = jnp.zeros((ts, 512), F32)
        for k in range(KA):
            du = du + dwinA[pl.ds(KA - 1 - k, ts), :] * wA_r[k:k + 1, :]
            dwA_o[k:k + 1, :] += _sum_r(dca * uwin[pl.ds(HALO_A - KA + 1 + k, ts), :])
        sv = _sig(col(1))
        put(0, du * sv)
        put(1, du * col(0) * sv * (1.0 - sv))

        b_gate = col(6)
        sgb = _sig(b_gate)
        cbv = cb_r[...]
        b_b = col(4)
        dzb = dzb_r[...]
        put(6, dzb * b_b * cbv * _dsilu(b_gate, sgb))
        dhb = dzb * b_gate * sgb
        put(4, dhb * cbv)
        dcb = dhb * b_b
        dwinB[0:ts, :] = dcb
        dwinB[ts:, :] = haloB[...]
        haloB[...] = dcb[0:HALO_B, :]
        vwin[0:HALO_B, :] = vh_r[...] * nz
        vwin[HALO_B:, :] = vb_r[...]
        dv = jnp.zeros((ts, 512), F32)
        for k in range(KB):
            dv = dv + dwinB[pl.ds(KB - 1 - k, ts), :] * wB_r[k:k + 1, :]
            dwB_o[k:k + 1, :] += _sum_r(dcb * vwin[pl.ds(HALO_B - KB + 1 + k, ts), :])
        put(5, dv * col(3))
        put(3, dv * col(5))

        c_gate = col(7)
        sgc = _sig(c_gate)
        dzc = dzc_r[...]
        ocv = oc_r[...]
        put(7, dzc * ocv * _dsilu(c_gate, sgc))
        do = dzc * c_gate * sgc
        for h in range(HC):
            do_o[h] = do[:, DH * h:DH * (h + 1)].astype(BF)
        dd = do * ocv
        for h in range(HC):
            dl_o[h] = jnp.broadcast_to(jnp.sum(dd[:, DH * h:DH * (h + 1)], axis=1, keepdims=True), (ts, LW))

    v512 = jax.ShapeDtypeStruct((1, 512), F32)
    return pl.pallas_call(
        body, name="branch_bwd", grid=(nt,),
        out_shape=(jax.ShapeDtypeStruct(dP.shape, BF), jax.ShapeDtypeStruct((HC, S, DH), BF), jax.ShapeDtypeStruct((HC, S, LW), F32),
                   jax.ShapeDtypeStruct((32, 512), F32), v512, v512, v512, jax.ShapeDtypeStruct((8, 512), F32)),
        in_specs=[pl.BlockSpec((ts, 4096), lambda i: (rev(i), C_AB // 4096)),
                  rows(512), rows(512), rows(512), rows(512), halo(HALO_A), halo(HALO_B),
                  rows(512), rows(512), rows(512), rows(512),
                  full((32, 512)), full((1, 512)), full((1, 512)), full((8, 512)), pl.BlockSpec(memory_space=pl.ANY)],
        out_specs=(pl.BlockSpec((ts, 4096), lambda i: (rev(i), C_AB // 4096)),
                   pl.BlockSpec((HC, ts, DH), lambda i: (0, rev(i), 0)), pl.BlockSpec((HC, ts, LW), lambda i: (0, rev(i), 0)),
                   full((32, 512)), full((1, 512)), full((1, 512)), full((1, 512)), full((8, 512))),
        scratch_shapes=[pltpu.VMEM((ts + HALO_A, 512), F32), pltpu.VMEM((ts + HALO_A, 512), F32), pltpu.VMEM((HALO_A, 512), F32),
                        pltpu.VMEM((ts + HALO_B, 512), F32), pltpu.VMEM((ts + HALO_B, 512), F32), pltpu.VMEM((HALO_B, 512), F32)],
        input_output_aliases={15: 0},
        compiler_params=_cp(("arbitrary",)),
    )(P, ca, cb, u, vb, u, vb, dza, dzb, dzc, oc, wA, gA, betaA, wB, dP)


def _cum_bwd(P, dcum, bfg, dP, *, ts):
    S = P.shape[0]
    nt = S // ts

    def body(f_ref, dc_ref, bf_r, dp_in, df_o, dbf_o, carry):
        del dp_in
        i = pl.program_id(0)

        @pl.when(i == 0)
        def _():
            carry[...] = jnp.zeros_like(carry)
            dbf_o[...] = jnp.zeros_like(dbf_o)

        r = lax.broadcasted_iota(jnp.int32, (ts, ts), 0)
        c = lax.broadcasted_iota(jnp.int32, (ts, ts), 1)
        tri = (r <= c).astype(F32)
        dlogf = jnp.dot(tri, dc_ref[...], precision=HIGHEST, preferred_element_type=F32) + carry[...]
        carry[...] = dlogf[0:1, :]
        x = f_ref[...] + bf_r[...]
        lane = lax.broadcasted_iota(jnp.int32, (ts, 128), 1)
        df = jnp.where(lane < HC, dlogf * _sig(-x), 0.0)
        df_o[...] = df.astype(BF)
        dbf_o[...] += _sum_r(df)

    blk = pl.BlockSpec((ts, 128), lambda i: (nt - 1 - i, C_F // 128))
    return pl.pallas_call(
        body, name="cum_bwd", grid=(nt,),
        out_shape=(jax.ShapeDtypeStruct(dP.shape, BF), jax.ShapeDtypeStruct((1, 128), F32)),
        in_specs=[blk, pl.BlockSpec((ts, 128), lambda i: (nt - 1 - i, 0)), pl.BlockSpec((1, 128), lambda i: (0, 0)),
                  pl.BlockSpec(memory_space=pl.ANY)],
        out_specs=(blk, pl.BlockSpec((1, 128), lambda i: (0, 0))),
        scratch_shapes=[pltpu.VMEM((1, 128), F32)],
        input_output_aliases={3: 0},
        compiler_params=_cp(("arbitrary",)),
    )(P, dcum, bfg, dP)


def _adamw(w, m, v, gparts, *, name, tr):
    rws, cols = w.shape
    tr = min(tr, rws)
    assert rws % tr == 0 and gparts.shape == (NDEV, rws, cols), (name, w.shape, gparts.shape)
    c1 = 1.0 - ADAM_B1 ** ADAM_STEP
    c2 = 1.0 - ADAM_B2 ** ADAM_STEP

    def body(w_r, m_r, v_r, g_r, g_o, d_o, m_o, v_o):
        g = g_r[0].astype(F32)
        for p in range(1, NDEV):
            g = g + g_r[p].astype(F32)
        mn = ADAM_B1 * m_r[...] + (1.0 - ADAM_B1) * g
        vn = ADAM_B2 * v_r[...] + (1.0 - ADAM_B2) * (g * g)
        g_o[...] = g
        m_o[...] = mn
        v_o[...] = vn
        d_o[...] = -ADAM_LR * ((mn / c1) / (jnp.sqrt(vn / c2) + ADAM_EPS) + ADAM_WD * w_r[...])

    blk = pl.BlockSpec((tr, cols), lambda i: (i, 0))
    shp = jax.ShapeDtypeStruct((rws, cols), F32)
    return pl.pallas_call(
        body, name=name, grid=(rws // tr,), out_shape=(shp,) * 4,
        in_specs=[blk, blk, blk, pl.BlockSpec((NDEV, tr, cols), lambda i: (0, i, 0))],
        out_specs=(blk,) * 4, compiler_params=_cp(("parallel",)),
    )(w, m, v, gparts)


def _slot(p):
    return 4 * p[0] + 2 * p[1] + p[2]


def _all_gather(arrs, *, name):
    na = len(arrs)

    def body(*refs):
        ins, outs = refs[:na], refs[na:2 * na]
        send_sems, recv_sems, local_sems = refs[2 * na:]
        x, y, c = lax.axis_index("x"), lax.axis_index("y"), lax.axis_index("c")
        me, sib = (x, y, c), (x, y, 1 - c)
        chips = [(1 - x, y), (x, 1 - y), (1 - x, 1 - y)]

        def cp(a, k, block, to, src=None):
            dst = outs[a].at[_slot(block)]
            return pltpu.make_async_remote_copy(src_ref=dst if src is None else src, dst_ref=dst,
                                                send_sem=send_sems.at[a, k], recv_sem=recv_sems.at[a, k],
                                                device_id=to, device_id_type=pl.DeviceIdType.MESH)

        mine = [pltpu.make_async_copy(ins[a], outs[a].at[_slot(me)], local_sems.at[a]) for a in range(na)]
        for m in mine:
            m.start()
        first = []
        for a in range(na):
            first.append(cp(a, 0, me, sib, src=ins[a]))
            first += [cp(a, 1 + j, me, (*chip, c), src=ins[a]) for j, chip in enumerate(chips)]
        for f in first:
            f.start()
        passed = []
        for j, chip in enumerate(chips):
            for a in range(na):
                cp(a, 1 + j, (*chip, c), me).wait_recv()
                fwd = cp(a, 4 + j, (*chip, c), sib)
                fwd.start()
                passed.append(fwd)
        for a in range(na):
            cp(a, 0, sib, me).wait_recv()
            for j, chip in enumerate(chips):
                cp(a, 4 + j, (*chip, 1 - c), me).wait_recv()
        for f in first + passed:
            f.wait_send()
        for m in mine:
            m.wait()

    anyspec = pl.BlockSpec(memory_space=pl.ANY)
    return pl.pallas_call(
        body, name=name,
        out_shape=tuple(jax.ShapeDtypeStruct((NDEV,) + a.shape, a.dtype) for a in arrs),
        in_specs=[anyspec] * na, out_specs=(anyspec,) * na,
        scratch_shapes=[pltpu.SemaphoreType.DMA((na, 7)), pltpu.SemaphoreType.DMA((na, 7)), pltpu.SemaphoreType.DMA((na,))],
    )(*arrs)


def _all_to_all(arrs, *, name):
    na = len(arrs)

    def body(*refs):
        ins, outs = refs[:na], refs[na:2 * na]
        send_sems, recv_sems, local_sems = refs[2 * na:]
        x, y, c = lax.axis_index("x"), lax.axis_index("y"), lax.axis_index("c")
        me = (x, y, c)
        peers = [(x ^ ((k >> 2) & 1), y ^ ((k >> 1) & 1), c ^ (k & 1)) for k in range(1, NDEV)]

        def cp(a, k, peer):
            return pltpu.make_async_remote_copy(src_ref=ins[a].at[_slot(peer)], dst_ref=outs[a].at[_slot(me)],
                                                send_sem=send_sems.at[a, k], recv_sem=recv_sems.at[a, k],
                                                device_id=peer, device_id_type=pl.DeviceIdType.MESH)

        def landed(a, k, peer):
            dst = outs[a].at[_slot(peer)]
            return pltpu.make_async_remote_copy(src_ref=dst, dst_ref=dst, send_sem=send_sems.at[a, k], recv_sem=recv_sems.at[a, k],
                                                device_id=peer, device_id_type=pl.DeviceIdType.MESH)

        mine = [pltpu.make_async_copy(ins[a].at[_slot(me)], outs[a].at[_slot(me)], local_sems.at[a]) for a in range(na)]
        for m in mine:
            m.start()
        sends = [cp(a, k, peer) for a in range(na) for k, peer in enumerate(peers)]
        for s in sends:
            s.start()
        for a in range(na):
            for k, peer in enumerate(peers):
                landed(a, k, peer).wait_recv()
        for s in sends:
            s.wait_send()
        for m in mine:
            m.wait()

    anyspec = pl.BlockSpec(memory_space=pl.ANY)
    return pl.pallas_call(
        body, name=name,
        out_shape=tuple(jax.ShapeDtypeStruct(a.shape, a.dtype) for a in arrs),
        in_specs=[anyspec] * na, out_specs=(anyspec,) * na,
        scratch_shapes=[pltpu.SemaphoreType.DMA((na, 7)), pltpu.SemaphoreType.DMA((na, 7)), pltpu.SemaphoreType.DMA((na,))],
    )(*arrs)


def _gathered_to_layout(g4):
    parts = []
    for a, b in _RUNS:
        for d in range(a // SHARD_IN, (b - 1) // SHARD_IN + 1):
            lo, hi = max(a, d * SHARD_IN), min(b, (d + 1) * SHARD_IN)
            parts.append(g4[d, ..., lo - d * SHARD_IN:hi - d * SHARD_IN])
    parts.append(jnp.zeros(g4.shape[1:-1] + (NP - IN_COLS,), g4.dtype))
    return jnp.concatenate(parts, axis=-1)


def _layout_to_shards(w):
    offs, off = {}, 0
    for a, b in _RUNS:
        offs[a] = (b, off)
        off += b - a
    shards = []
    for d in range(NDEV):
        parts = []
        for a in sorted(offs):
            b, off = offs[a]
            lo, hi = max(a, d * SHARD_IN), min(b, (d + 1) * SHARD_IN)
            if lo < hi:
                parts.append(w[..., off + lo - a:off + hi - a])
        shards.append(jnp.concatenate(parts, axis=-1))
    return jnp.stack(shards)


def _tiles(S):
    ts = min(256, S)
    tsb = min(128, S)
    tq = min(512, S)
    return ts, tsb, tq


def _local_step(x, mem, tgt, W, wA, wB, wkv, pa, pb, pc, pm, wo, b_forget, conv_a_b, ln_a_g, ln_a_b, mem_ln_g, mem_ln_b, ln_g, ln_b):
    S = x.shape[0]
    ts, tsb, tq = _tiles(S)
    row = lambda a: a.reshape(1, -1)
    bfp = jnp.pad(b_forget, ((0, 0), (0, 128 - HC)))

    mem_n, mem_hat = _mem_ln_fwd(mem, row(mem_ln_g), row(mem_ln_b))
    saved = []
    for l in range(NL):
        P = _mm(x, W[l], name="proj_fwd", tm=1024, tn=1152, tk=D)
        kv = _mm(mem_n, wkv[l], name="kv_fwd", tm=ML, tn=512, tk=D)
        za, zb, u, ca, vb, cb, cum = _pre_fwd(P, wA[l], row(conv_a_b[l]), row(ln_a_g[l]), row(ln_a_b[l]), wB[l], row(bfp[l]), ts=ts)
        Qa, Ka, V = _attn_prep(P, cum, ts=ts)
        oc, zc, lse = _attn_fwd(P, Qa, Ka, V, tq=tq)
        zm = _xattn_fwd(P, kv, ts=tq)
        merged, ya, yb, yc, ym, xn, xh, rstd = _merge_fwd(za, zb, zc, zm, P, x, pa[l], pb[l], pc[l], pm[l], wo[l],
                                                          row(ln_g[l]), row(ln_b[l]), ts=ts)
        saved.append((x, P, kv, za, zb, zc, zm, u, ca, vb, cb, Qa, Ka, V, oc, lse, merged, ya, yb, yc, ym, xh, rstd))
        x = xn

    dx, loss = _loss_fwd(x, tgt, ts=ts)

    g = {k: [None] * NL for k in ("w_in", "b_forget", "conv_a_w", "conv_a_b", "ln_a_g", "ln_a_b", "conv_b_w", "w_kv_mem",
                                  "p_a", "p_b", "p_c", "p_m", "w_out", "ln_g", "ln_b")}
    dmem_n = [None] * NL
    for l in reversed(range(NL)):
        (xl, P, kv, za, zb, zc, zm, u, ca, vb, cb, Qa, Ka, V, oc, lse, merged, ya, yb, yc, ym, xh, rstd) = saved[l]
        dr, dm, g["w_out"][l], g["ln_g"][l], g["ln_b"][l] = _out_bwd(dx, xh, rstd, merged, wo[l], row(ln_g[l]), ts=ts)
        dP, dza, dzb, dzc, dzm, g["p_a"][l], g["p_b"][l], g["p_c"][l], g["p_m"][l] = _merge_bwd(
            dm, P, ya, yb, yc, ym, za, zb, zc, zm, pa[l], pb[l], pc[l], pm[l], ts=tsb)
        dP, do, dlt, dwA, g["conv_a_b"][l], g["ln_a_g"][l], g["ln_a_b"][l], dwB = _branch_bwd(
            P, ca, cb, u, vb, dza, dzb, dzc, oc, wA[l], row(ln_a_g[l]), row(ln_a_b[l]), wB[l], dP, ts=ts)
        g["conv_a_w"][l], g["conv_b_w"][l] = dwA[:KA], dwB[:KB]
        dP, dck = _attn_bwd_dkv(Qa, Ka, V, do, lse[:, :, 0], dlt[:, :, 0], dP, tq=tq)
        dP, dcq = _attn_bwd_dq(Qa, Ka, V, do, lse, dlt, dP, tq=tq)
        dcum = dcq - dck
        dP, dbf = _cum_bwd(P, dcum, row(bfp[l]), dP, ts=ts)
        g["b_forget"][l] = dbf[0, :HC]
        dP, dkv = _xattn_bwd(P, kv, dzm, dP, ts=tq)
        g["w_kv_mem"][l] = _mm(mem_n.T, dkv, name="wkv_bwd", tm=D, tn=512, tk=ML)
        dmem_n[l] = _mm(dkv, wkv[l], name="memn_bwd", nt=True, tm=ML, tn=D, tk=512)
        g["w_in"][l] = _mm(xl.T.astype(BF), dP, name="win_bwd", out_dtype=BF, tm=D, tn=1152, tk=1024)
        dx = _mm(dP, W[l], name="x_bwd", nt=True, tm=1024, tn=D, tk=1152, add=dr, add_scale=ALPHA)

    g["mem_ln_g"], g["mem_ln_b"] = _mem_ln_bwd(dmem_n, mem_hat)
    out = {k: (jnp.stack(v) if isinstance(v, list) else v) for k, v in g.items()}
    return loss[0, 0], dx, out


_SMALL = (("b_forget", (NL, HC)), ("conv_a_b", (NL, 512)), ("ln_a_g", (NL, 512)), ("ln_a_b", (NL, 512)),
          ("mem_ln_g", (D,)), ("mem_ln_b", (D,)), ("ln_g", (NL, D)), ("ln_b", (NL, D)),
          ("conv_a_w", (NL, KA, 512)), ("conv_b_w", (NL, KB, 512)))


def _pack(parts, rows_mult=8):
    flat = jnp.concatenate([p.reshape(-1).astype(F32) for p in parts])
    n = flat.shape[0]
    rows = -(-n // 128)
    rows = -(-rows // rows_mult) * rows_mult
    return jnp.pad(flat, (0, rows * 128 - n)).reshape(rows, 128)


def _unpack(buf, shapes):
    flat = buf.reshape(-1)
    out, off = [], 0
    for shp in shapes:
        n = 1
        for d in shp:
            n *= d
        out.append(flat[off:off + n].reshape(shp))
        off += n
    return out


def kernel(x, mem, w_in, b_forget, conv_a_w, conv_a_b, ln_a_g, ln_a_b, conv_b_w, w_kv_mem, mem_ln_g, mem_ln_b, p_a, p_b, p_c, p_m, w_out, ln_g, ln_b, loss_target, m_w_in, m_b_forget, m_conv_a_w, m_conv_a_b, m_ln_a_g, m_ln_a_b, m_conv_b_w, m_w_kv_mem, m_mem_ln_g, m_mem_ln_b, m_p_a, m_p_b, m_p_c, m_p_m, m_w_out, m_ln_g, m_ln_b, v_w_in, v_b_forget, v_conv_a_w, v_conv_a_b, v_ln_a_g, v_ln_a_b, v_conv_b_w, v_w_kv_mem, v_mem_ln_g, v_mem_ln_b, v_p_a, v_p_b, v_p_c, v_p_m, v_w_out, v_ln_g, v_ln_b):
    wts = dict(w_in=w_in, b_forget=b_forget, conv_a_w=conv_a_w, conv_a_b=conv_a_b, ln_a_g=ln_a_g, ln_a_b=ln_a_b, conv_b_w=conv_b_w,
               w_kv_mem=w_kv_mem, mem_ln_g=mem_ln_g, mem_ln_b=mem_ln_b, p_a=p_a, p_b=p_b, p_c=p_c, p_m=p_m, w_out=w_out, ln_g=ln_g, ln_b=ln_b)
    mom = dict(w_in=m_w_in, b_forget=m_b_forget, conv_a_w=m_conv_a_w, conv_a_b=m_conv_a_b, ln_a_g=m_ln_a_g, ln_a_b=m_ln_a_b,
               conv_b_w=m_conv_b_w, w_kv_mem=m_w_kv_mem, mem_ln_g=m_mem_ln_g, mem_ln_b=m_mem_ln_b, p_a=m_p_a, p_b=m_p_b, p_c=m_p_c,
               p_m=m_p_m, w_out=m_w_out, ln_g=m_ln_g, ln_b=m_ln_b)
    vel = dict(w_in=v_w_in, b_forget=v_b_forget, conv_a_w=v_conv_a_w, conv_a_b=v_conv_a_b, ln_a_g=v_ln_a_g, ln_a_b=v_ln_a_b,
               conv_b_w=v_conv_b_w, w_kv_mem=v_w_kv_mem, mem_ln_g=v_mem_ln_g, mem_ln_b=v_mem_ln_b, p_a=v_p_a, p_b=v_p_b, p_c=v_p_c,
               p_m=v_p_m, w_out=v_w_out, ln_g=v_ln_g, ln_b=v_ln_b)
    names = ("w_in", "b_forget", "conv_a_w", "conv_a_b", "ln_a_g", "ln_a_b", "conv_b_w", "w_kv_mem", "mem_ln_g", "mem_ln_b",
             "p_a", "p_b", "p_c", "p_m", "w_out", "ln_g", "ln_b")
    mid = ("p_a", "p_b", "p_c", "p_m", "w_out", "w_kv_mem")
    me = 4 * lax.axis_index("x") + 2 * lax.axis_index("y") + lax.axis_index("c")

    def mid_rows(a):
        return a.reshape(-1, 128)

    mid_shapes = [wts[n].shape for n in mid]
    mid_nrows = [mid_rows(wts[n]).shape[0] for n in mid]
    pk16 = jnp.concatenate([mid_rows(wts[n]) for n in mid], axis=0)
    pk32 = jnp.concatenate([conv_a_w, conv_b_w], axis=1).reshape(NL * (KA + KB), 512 // NDEV)
    g_win, g16, g32 = _all_gather([w_in.reshape(NL * D, SHARD_IN).astype(BF), pk16.astype(BF), pk32], name="gather_weights")

    W = _gathered_to_layout(g_win.reshape(NDEV, NL, D, SHARD_IN))
    full = {}
    off = 0
    for n, shp, nr in zip(mid, mid_shapes, mid_nrows):
        blk = g16[:, off:off + nr].reshape((NDEV,) + shp)
        off += nr
        if n in ("w_out", "w_kv_mem"):
            full[n] = blk.transpose(1, 0, 2, 3).reshape(NL, NDEV * shp[1], shp[2])
        else:
            full[n] = blk.transpose(1, 2, 0, 3).reshape(NL, shp[1], NDEV * shp[2])
    conv = g32.reshape(NDEV, NL, KA + KB, 512 // NDEV).transpose(1, 2, 0, 3).reshape(NL, KA + KB, 512)
    wA = jnp.pad(conv[:, :KA], ((0, 0), (0, 32 - KA), (0, 0)))
    wB = jnp.pad(conv[:, KA:], ((0, 0), (0, 8 - KB), (0, 0)))

    loss, dx, g = _local_step(x[0], mem[0], loss_target[0], W, wA, wB, full["w_kv_mem"], full["p_a"], full["p_b"], full["p_c"],
                              full["p_m"], full["w_out"], b_forget, conv_a_b, ln_a_g, ln_a_b, mem_ln_g, mem_ln_b, ln_g, ln_b)
    loss = lax.psum(loss, ("x", "y", "c"))

    gw = _layout_to_shards(g["w_in"]).reshape(NDEV, NL * D, SHARD_IN)
    chunks = []
    for n, shp in zip(mid, mid_shapes):
        a = g[n]
        if n in ("w_out", "w_kv_mem"):
            a = a.reshape(NL, NDEV, shp[1], shp[2]).transpose(1, 0, 2, 3)
        else:
            a = a.reshape(NL, shp[1], NDEV, shp[2]).transpose(2, 0, 1, 3)
        chunks.append(a.reshape(NDEV, -1, 128))
    g16s = jnp.concatenate(chunks, axis=1).astype(BF)
    small = _pack([g[n] for n, _ in _SMALL])
    r_win, r16 = _all_to_all([gw, g16s], name="scatter_grads")
    (r_small,) = _all_gather([small], name="gather_small")

    res = {}
    res["w_in"] = [a.reshape(NL, D, SHARD_IN) for a in
                   _adamw(w_in.reshape(NL * D, SHARD_IN), m_w_in.reshape(NL * D, SHARD_IN), v_w_in.reshape(NL * D, SHARD_IN),
                          r_win, name="adamw_w_in", tr=128)]
    pk = lambda d: jnp.concatenate([mid_rows(d[n]) for n in mid], axis=0)
    o16 = _adamw(pk(wts), pk(mom), pk(vel), r16, name="adamw_mid", tr=1024)
    off = 0
    for n, shp, nr in zip(mid, mid_shapes, mid_nrows):
        res[n] = [o[off:off + nr].reshape(shp) for o in o16]
        off += nr

    def small_view(d, n):
        a = d[n]
        if n in ("conv_a_w", "conv_b_w"):
            fullw = jnp.zeros(a.shape[:2] + (512,), F32)
            return lax.dynamic_update_slice(fullw, a, (0, 0, me * (512 // NDEV)))
        return a

    spk = lambda d: _pack([small_view(d, n) for n, _ in _SMALL])
    osm = _adamw(spk(wts), spk(mom), spk(vel), r_small, name="adamw_small", tr=1024)
    osm = [_unpack(o, [s for _, s in _SMALL]) for o in osm]
    for idx, (n, _) in enumerate(_SMALL):
        vals = [o[idx] for o in osm]
        if n in ("conv_a_w", "conv_b_w"):
            vals = [lax.dynamic_slice(a, (0, 0, me * (512 // NDEV)), a.shape[:2] + (512 // NDEV,)) for a in vals]
        res[n] = vals

    outs = [loss, dx[None]]
    for k in range(4):
        outs += [res[n][k] for n in names]
    return tuple(outs)
```

```python
import jax
import jax.numpy as jnp
from jax import lax
from jax.experimental import pallas as pl
from jax.experimental.pallas import tpu as pltpu

F32 = jnp.float32
BF = jnp.bfloat16
HIGHEST = lax.Precision.HIGHEST

D = 1024
NL = 4
NDEV = 8
HC, DH = 8, 64
HM = 4
ML = 256
KA, KB = 31, 3
HALO_A, HALO_B = 32, 8
ALPHA = (2.0 * NL) ** 0.25
EPS = 1e-5
SCALE = DH ** -0.5
NEG = -1e30

ADAM_LR, ADAM_B1, ADAM_B2, ADAM_EPS, ADAM_WD, ADAM_STEP = 0.001, 0.9, 0.999, 1e-08, 0.01, 10

C_G = 0
C_AB = 4096
C_Q = 8192
C_M = 8704
C_KV = 9216
C_F = 10240
NP = 10368
_RUNS = ((6152, 10248), (0, 3584), (5128, 5640), (3584, 4096), (5640, 6152), (4096, 5120), (5120, 5128))
IN_COLS = 10248
SHARD_IN = IN_COLS // NDEV

VMEM_LIMIT = 56 * 1024 * 1024

NT_DIMS = (((1,), (1,)), ((), ()))
TN_DIMS = (((0,), (0,)), ((), ()))


def _cp(sem=None):
    return pltpu.CompilerParams(dimension_semantics=sem, vmem_limit_bytes=VMEM_LIMIT)


def _sig(x):
    return 1.0 / (1.0 + jnp.exp(-x))


def _dsilu(x, s):
    return s * (1.0 + x * (1.0 - s))


def _mean_l(x):
    return jnp.mean(x, axis=-1, keepdims=True)


def _sum_r(x):
    return jnp.sum(x, axis=0, keepdims=True)


def _ln_hat(x):
    mu = _mean_l(x)
    xc = x - mu
    rstd = lax.rsqrt(_mean_l(xc * xc) + EPS)
    return xc * rstd, rstd


def _ln_bwd(dxh, xh, rstd):
    return rstd * (dxh - _mean_l(dxh) - xh * _mean_l(dxh * xh))


def _dot(a, b, dims=None):
    if dims is None:
        return jnp.dot(a, b, preferred_element_type=F32)
    return lax.dot_general(a, b, dims, preferred_element_type=F32)


def _lane_pack(cols, rows):
    lane = lax.broadcasted_iota(jnp.int32, (rows, 128), 1)
    out = jnp.zeros((rows, 128), F32)
    for h, c in enumerate(cols):
        out = jnp.where(lane == h, c, out)
    return out


def _mm(a, b, *, name, nt=False, out_dtype=F32, tm=512, tn=512, tk=512, add=None, add_scale=1.0):
    m, kdim = a.shape
    n = b.shape[0] if nt else b.shape[1]
    tm, tn, tk = min(tm, m), min(tn, n), min(tk, kdim)
    assert m % tm == 0 and n % tn == 0 and kdim % tk == 0, (name, a.shape, b.shape, tm, tn, tk)
    nk = kdim // tk

    def body(*refs):
        a_ref, b_ref = refs[:2]
        add_ref = None if add is None else refs[2]
        o_ref = refs[2 if add is None else 3]

        def finish(r):
            if add is not None:
                r = r + add_scale * add_ref[...]
            o_ref[...] = r.astype(out_dtype)

        part = _dot(a_ref[...].astype(BF), b_ref[...].astype(BF), NT_DIMS if nt else None)
        if nk == 1:
            finish(part)
            return
        acc_ref = refs[-1]
        k = pl.program_id(2)

        @pl.when(k == 0)
        def _():
            acc_ref[...] = part

        @pl.when(k > 0)
        def _():
            acc_ref[...] += part

        @pl.when(k == nk - 1)
        def _():
            finish(acc_ref[...])

    in_specs = [pl.BlockSpec((tm, tk), lambda i, j, k: (i, k)),
                pl.BlockSpec((tn, tk), lambda i, j, k: (j, k)) if nt else pl.BlockSpec((tk, tn), lambda i, j, k: (k, j))]
    args = [a, b]
    if add is not None:
        in_specs.append(pl.BlockSpec((tm, tn), lambda i, j, k: (i, j)))
        args.append(add)
    return pl.pallas_call(
        body, name=name, grid=(m // tm, n // tn, nk),
        out_shape=jax.ShapeDtypeStruct((m, n), out_dtype),
        in_specs=in_specs, out_specs=pl.BlockSpec((tm, tn), lambda i, j, k: (i, j)),
        scratch_shapes=[pltpu.VMEM((tm, tn), F32)] if nk > 1 else [],
        compiler_params=_cp(("parallel", "parallel", "arbitrary")),
    )(*args)


def _mem_ln_fwd(mem, g, b):
    def body(m_ref, g_ref, b_ref, n_ref, h_ref):
        xh, _ = _ln_hat(m_ref[...])
        h_ref[...] = xh
        n_ref[...] = xh * g_ref[...] + b_ref[...]

    shp = jax.ShapeDtypeStruct(mem.shape, F32)
    return pl.pallas_call(body, name="mem_ln_fwd", out_shape=(shp, shp), compiler_params=_cp())(mem, g, b)


def _mem_ln_bwd(dns, mhat):
    def body(*refs):
        d_refs, h_ref, dg_ref, db_ref = refs[:NL], refs[NL], refs[NL + 1], refs[NL + 2]
        dn = d_refs[0][...]
        for r in d_refs[1:]:
            dn = dn + r[...]
        dg_ref[...] = _sum_r(dn * h_ref[...])
        db_ref[...] = _sum_r(dn)

    shp = jax.ShapeDtypeStruct((1, D), F32)
    return pl.pallas_call(body, name="mem_ln_bwd", out_shape=(shp, shp), compiler_params=_cp())(*dns, mhat)


def _pre_fwd(P, wA, bA, gA, betaA, wB, bfg, *, ts):
    S = P.shape[0]
    nt = S // ts
    cb = C_AB // 512

    def cur(j):
        return pl.BlockSpec((ts, 512), lambda i, j=j: (i, cb + j))

    def halo(j, rows):
        return pl.BlockSpec((rows, 512), lambda i, j=j: (jnp.maximum(i * (ts // rows) - 1, 0), cb + j))

    def full(shape):
        return pl.BlockSpec(shape, lambda i: (0, 0))

    def body(au, av, ag, bh, bb, bc, bg, f_ref, au_h, av_h, bh_h, bc_h, wA_r, bA_r, gA_r, betaA_r, wB_r, bf_r,
             za_o, zb_o, u_o, ca_o, vb_o, cb_o, cum_o, winA, winB, carry):
        i = pl.program_id(0)
        nz = (i > 0).astype(F32)

        u = au[...] * _sig(av[...])
        winA[0:HALO_A, :] = au_h[...] * _sig(av_h[...]) * nz
        winA[HALO_A:, :] = u
        acc = jnp.zeros((ts, 512), F32)
        for k in range(KA):
            acc = acc + winA[pl.ds(HALO_A - KA + 1 + k, ts), :] * wA_r[k:k + 1, :]
        ca = acc + bA_r[...]
        xh, _ = _ln_hat(ca)
        n = xh * gA_r[...] + betaA_r[...]
        a = n * _sig(n)
        agv = ag[...]
        za_o[...] = (a * agv * _sig(agv)).astype(BF)
        u_o[...] = u
        ca_o[...] = ca

        vb = bc[...] * bh[...]
        winB[0:HALO_B, :] = bc_h[...] * bh_h[...] * nz
        winB[HALO_B:, :] = vb
        accb = jnp.zeros((ts, 512), F32)
        for k in range(KB):
            accb = accb + winB[pl.ds(HALO_B - KB + 1 + k, ts), :] * wB_r[k:k + 1, :]
        bgv = bg[...]
        zb_o[...] = (bb[...] * accb * bgv * _sig(bgv)).astype(BF)
        vb_o[...] = vb
        cb_o[...] = accb

        @pl.when(i == 0)
        def _():
            carry[...] = jnp.zeros_like(carry)

        x = f_ref[...] + bf_r[...]
        logf = jnp.minimum(x, 0.0) - jnp.log1p(jnp.exp(-jnp.abs(x)))
        r = lax.broadcasted_iota(jnp.int32, (ts, ts), 0)
        c = lax.broadcasted_iota(jnp.int32, (ts, ts), 1)
        tri = (r >= c).astype(F32)
        cum = jnp.dot(tri, logf, precision=HIGHEST, preferred_element_type=F32) + carry[...]
        cum_o[...] = cum
        carry[...] = cum[ts - 1:ts, :]

    s512 = lambda dt: jax.ShapeDtypeStruct((S, 512), dt)
    o512 = pl.BlockSpec((ts, 512), lambda i: (i, 0))
    return pl.pallas_call(
        body, name="pre_fwd", grid=(nt,),
        out_shape=(s512(BF), s512(BF), s512(F32), s512(F32), s512(F32), s512(F32), jax.ShapeDtypeStruct((S, 128), F32)),
        in_specs=[cur(0), cur(1), cur(2), cur(3), cur(4), cur(5), cur(6),
                  pl.BlockSpec((ts, 128), lambda i: (i, C_F // 128)),
                  halo(0, HALO_A), halo(1, HALO_A), halo(3, HALO_B), halo(5, HALO_B),
                  full((32, 512)), full((1, 512)), full((1, 512)), full((1, 512)), full((8, 512)), full((1, 128))],
        out_specs=(o512, o512, o512, o512, o512, o512, pl.BlockSpec((ts, 128), lambda i: (i, 0))),
        scratch_shapes=[pltpu.VMEM((ts + HALO_A, 512), F32), pltpu.VMEM((ts + HALO_B, 512), F32), pltpu.VMEM((1, 128), F32)],
        compiler_params=_cp(("arbitrary",)),
    )(P, P, P, P, P, P, P, P, P, P, P, P, wA, bA, gA, betaA, wB, bfg)


RC = 32


def _split3(c):
    c1 = c.astype(BF).astype(F32)
    r = c - c1
    c2 = r.astype(BF).astype(F32)
    return c1, c2, r - c2


def _attn_prep(P, cum, *, ts):
    S = P.shape[0]

    def body(q_ref, kv_ref, cum_ref, qa_o, ka_o, v_o):
        lane = lax.broadcasted_iota(jnp.int32, (ts, DH), 1)
        for h in range(HC):
            sl = slice(DH * h, DH * (h + 1))
            c1, c2, c3 = _split3(cum_ref[:, h:h + 1])
            lo = jnp.where(lane == 0, c1, jnp.where(lane == 1, c2, jnp.where(lane == 2, c3, 0.0)))
            hi = jnp.where(lane == 3, c1, jnp.where(lane == 4, c2, jnp.where(lane == 5, c3, 0.0)))
            qa_o[h, :, 0:DH] = (q_ref[:, sl] * SCALE).astype(BF)
            qa_o[h, :, DH:2 * DH] = (lo + jnp.where((lane >= 3) & (lane < 6), 1.0, 0.0)).astype(BF)
            ka_o[h, :, 0:DH] = kv_ref[:, sl].astype(BF)
            ka_o[h, :, DH:2 * DH] = (jnp.where(lane < 3, 1.0, 0.0) - hi).astype(BF)
            v_o[h] = kv_ref[:, 512 + DH * h:512 + DH * (h + 1)].astype(BF)

    aug = jax.ShapeDtypeStruct((HC, S, 2 * DH), BF)
    return pl.pallas_call(
        body, name="attn_prep", grid=(S // ts,),
        out_shape=(aug, aug, jax.ShapeDtypeStruct((HC, S, DH), BF)),
        in_specs=[pl.BlockSpec((ts, 512), lambda i: (i, C_Q // 512)), pl.BlockSpec((ts, 1024), lambda i: (i, C_KV // 1024)),
                  pl.BlockSpec((ts, 128), lambda i: (i, 0))],
        out_specs=(pl.BlockSpec((HC, ts, 2 * DH), lambda i: (0, i, 0)), pl.BlockSpec((HC, ts, 2 * DH), lambda i: (0, i, 0)),
                   pl.BlockSpec((HC, ts, DH), lambda i: (0, i, 0))),
        compiler_params=_cp(("parallel",)),
    )(P, P, cum)


LW = 128


def _lanes(c):
    return slice(LW * c, LW * (c + 1))


def _diag_slices(rc, n, rows_are_queries):
    out = []
    for c in range(n // LW):
        r0, r1, c0, c1 = rc * RC, rc * RC + RC - 1, LW * c, LW * c + LW - 1
        lo, hi = (c1 <= r0, c0 > r1) if rows_are_queries else (r1 <= c0, r0 > c1)
        if lo:
            out.append("all")
        elif hi:
            out.append("none")
        else:
            r = lax.broadcasted_iota(jnp.int32, (RC, LW), 0) + r0
            cc = lax.broadcasted_iota(jnp.int32, (RC, LW), 1) + c0
            out.append((r >= cc) if rows_are_queries else (cc >= r))
    return out


def _pieces(ref2d, rows, rc, n, masked, rows_are_queries):
    kinds = _diag_slices(rc, n, rows_are_queries) if masked else ["all"] * (n // LW)
    out = []
    for c, kind in enumerate(kinds):
        if isinstance(kind, str):
            out.append(ref2d[rows, _lanes(c)] if kind == "all" else None)
        else:
            out.append(jnp.where(kind, ref2d[rows, _lanes(c)], NEG))
    return out


def _attn_fwd(P, Qa, Ka, V, *, tq, gather=()):
    S = P.shape[0]
    nq = S // tq
    ng = len(gather)

    def body(*refs):
        qa_ref, ka_ref, v_ref, cg_ref = refs[:4]
        o_ref, zc_ref, lse_ref = refs[4 + ng:7 + ng]
        s_s, p_s, m_s, l_s, acc_s, pm_s, al_s = refs[7 + 2 * ng:14 + 2 * ng]
        i, j = pl.program_id(0), pl.program_id(1)
        if ng:
            start, finish = _gather_copies(refs[4:4 + ng], refs[7 + ng:7 + 2 * ng], *refs[14 + 2 * ng:])
            pl.when((i == 0) & (j == 0))(start)

        @pl.when(j == 0)
        def _():
            m_s[...] = jnp.full_like(m_s, NEG)
            l_s[...] = jnp.zeros_like(l_s)
            acc_s[...] = jnp.zeros_like(acc_s)

        def step(masked):
            s_s[0] = _dot(qa_ref[0], ka_ref[0], NT_DIMS)
            for h in range(HC):
                b = h % 2
                if h + 1 < HC:
                    s_s[1 - b] = _dot(qa_ref[h + 1], ka_ref[h + 1], NT_DIMS)
                for rc in range(tq // RC):
                    rows = slice(rc * RC, (rc + 1) * RC)
                    pm = None
                    for sc in _pieces(s_s.at[b], rows, rc, tq, masked, True):
                        if sc is not None:
                            pm = sc if pm is None else jnp.maximum(pm, sc)
                    pm_s[rows, :] = pm
                m_prev = m_s[h]
                m_new = jnp.maximum(m_prev, jnp.max(pm_s[...], axis=1, keepdims=True))
                alpha = jnp.exp(m_prev - m_new)
                m_s[h] = m_new
                al_s[...] = alpha
                for rc in range(tq // RC):
                    rows = slice(rc * RC, (rc + 1) * RC)
                    mb = m_s[h, rows]
                    ps = None
                    for c, sc in enumerate(_pieces(s_s.at[b], rows, rc, tq, masked, True)):
                        if sc is None:
                            p_s[b, rows, _lanes(c)] = jnp.zeros((RC, LW), BF)
                            continue
                        p = jnp.exp(sc - mb)
                        ps = p if ps is None else ps + p
                        p_s[b, rows, _lanes(c)] = p.astype(BF)
                    l_s[h, rows] = al_s[rows] * l_s[h, rows] + ps
                acc_s[h] = al_s[:, 0:DH] * acc_s[h] + _dot(p_s[b], v_ref[h])

        @pl.when(j < i)
        def _():
            step(False)

        @pl.when(j == i)
        def _():
            step(True)
            for h in range(HC):
                l = jnp.sum(l_s[h], axis=1, keepdims=True)
                o_ref[:, DH * h:DH * (h + 1)] = acc_s[h] / l
                lse_ref[h] = m_s[h] + jnp.log(l)
            cg = cg_ref[...]
            zc_ref[...] = (o_ref[...] * cg * _sig(cg)).astype(BF)

        if ng:
            pl.when((i == nq - 1) & (j == nq - 1))(finish)

    stat = pltpu.VMEM((HC, tq, LW), F32)
    anyspec = pl.BlockSpec(memory_space=pl.ANY)
    res = pl.pallas_call(
        body, name="attn_fwd_gather" if ng else "attn_fwd", grid=(nq, nq),
        out_shape=(jax.ShapeDtypeStruct((S, 512), F32), jax.ShapeDtypeStruct((S, 512), BF), jax.ShapeDtypeStruct((HC, S, LW), F32))
        + _gathered_shapes(gather),
        in_specs=[pl.BlockSpec((HC, tq, 2 * DH), lambda i, j: (0, i, 0)),
                  pl.BlockSpec((HC, tq, 2 * DH), lambda i, j: (0, jnp.minimum(i, j), 0)),
                  pl.BlockSpec((HC, tq, DH), lambda i, j: (0, jnp.minimum(i, j), 0)),
                  pl.BlockSpec((tq, 512), lambda i, j: (i, C_AB // 512 + 7))] + [anyspec] * ng,
        out_specs=(pl.BlockSpec((tq, 512), lambda i, j: (i, 0)), pl.BlockSpec((tq, 512), lambda i, j: (i, 0)),
                   pl.BlockSpec((HC, tq, LW), lambda i, j: (0, i, 0))) + (anyspec,) * ng,
        scratch_shapes=[pltpu.VMEM((2, tq, tq), F32), pltpu.VMEM((2, tq, tq), BF), stat, stat, pltpu.VMEM((HC, tq, DH), F32),
                        pltpu.VMEM((tq, LW), F32), pltpu.VMEM((tq, LW), F32)] + (_comm_sems(ng) if ng else []),
        compiler_params=_cp(("arbitrary", "arbitrary") if ng else ("parallel", "arbitrary")),
    )(Qa, Ka, V, P, *gather)
    return res[0], res[1], res[2], list(res[3:])


def _attn_bwd_dkv(Qa, Ka, V, dO, lseT, dltT, dP, *, tq, scatter=()):
    S = Qa.shape[1]
    nq = S // tq
    ns = len(scatter)

    def body(*refs):
        qa_ref, ka_ref, v_ref, do_ref, lse_ref, dl_ref = refs[:6]
        dkv_o, dck_o = refs[7 + ns:9 + ns]
        s_s, dp_s, p_s, ds_s, dk_s, dv_s, dck_s = refs[9 + 2 * ns:16 + 2 * ns]
        j, i = pl.program_id(0), pl.program_id(1)
        if ns:
            start, finish = _scatter_copies(refs[7:7 + ns], refs[9 + ns:9 + 2 * ns], *refs[16 + 2 * ns:])
            pl.when((i == 0) & (j == 0))(start)

        @pl.when(i == 0)
        def _():
            dk_s[...] = jnp.zeros_like(dk_s)
            dv_s[...] = jnp.zeros_like(dv_s)
            dck_s[...] = jnp.zeros_like(dck_s)

        def mm(h, b):
            s_s[b] = _dot(ka_ref[h], qa_ref[h], NT_DIMS)
            dp_s[b] = _dot(v_ref[h], do_ref[h], NT_DIMS)

        def step(masked):
            mm(0, 0)
            for h in range(HC):
                b = h % 2
                if h + 1 < HC:
                    mm(h + 1, 1 - b)
                for rc in range(tq // RC):
                    rows = slice(rc * RC, (rc + 1) * RC)
                    acc = None
                    for c, sc in enumerate(_pieces(s_s.at[b], rows, rc, tq, masked, False)):
                        if sc is None:
                            p_s[rows, _lanes(c)] = jnp.zeros((RC, LW), BF)
                            ds_s[rows, _lanes(c)] = jnp.zeros((RC, LW), BF)
                            continue
                        p = jnp.exp(sc - lse_ref[h:h + 1, _lanes(c)])
                        ds = p * (dp_s[b, rows, _lanes(c)] - dl_ref[h:h + 1, _lanes(c)])
                        p_s[rows, _lanes(c)] = p.astype(BF)
                        ds_s[rows, _lanes(c)] = ds.astype(BF)
                        acc = ds if acc is None else acc + ds
                    dck_s[h, rows] += acc
                dv_s[h] += _dot(p_s[...], do_ref[h])
                dk_s[h] += _dot(ds_s[...], qa_ref[h])

        @pl.when(i > j)
        def _():
            step(False)

        @pl.when(i == j)
        def _():
            step(True)

        @pl.when(i == nq - 1)
        def _():
            for h in range(HC):
                dkv_o[:, DH * h:DH * (h + 1)] = dk_s[h][:, 0:DH].astype(BF)
                dkv_o[:, 512 + DH * h:512 + DH * (h + 1)] = dv_s[h].astype(BF)
            dck_o[...] = _lane_pack([jnp.sum(dck_s[h], axis=1, keepdims=True) for h in range(HC)], tq)

        if ns:
            pl.when((i == nq - 1) & (j == nq - 1))(finish)

    def qspec(w):
        return pl.BlockSpec((HC, tq, w), lambda j, i: (0, jnp.maximum(i, j), 0))

    def kspec(w):
        return pl.BlockSpec((HC, tq, w), lambda j, i: (0, j, 0))

    rowv = pl.BlockSpec((8, tq), lambda j, i: (0, jnp.maximum(i, j)))
    anyspec = pl.BlockSpec(memory_space=pl.ANY)
    res = pl.pallas_call(
        body, name="attn_bwd_dkv_scatter" if ns else "attn_bwd_dkv", grid=(nq, nq),
        out_shape=(jax.ShapeDtypeStruct(dP.shape, BF), jax.ShapeDtypeStruct((S, 128), F32)) + _same_shapes(scatter),
        in_specs=[qspec(2 * DH), kspec(2 * DH), kspec(DH), qspec(DH), rowv, rowv, anyspec] + [anyspec] * ns,
        out_specs=(pl.BlockSpec((tq, 1024), lambda j, i: (j, C_KV // 1024)), pl.BlockSpec((tq, 128), lambda j, i: (j, 0)))
        + (anyspec,) * ns,
        scratch_shapes=[pltpu.VMEM((2, tq, tq), F32), pltpu.VMEM((2, tq, tq), F32), pltpu.VMEM((tq, tq), BF), pltpu.VMEM((tq, tq), BF),
                        pltpu.VMEM((HC, tq, 2 * DH), F32), pltpu.VMEM((HC, tq, DH), F32), pltpu.VMEM((HC, tq, LW), F32)]
        + (_comm_sems(ns) if ns else []),
        input_output_aliases={6: 0},
        compiler_params=_cp(("arbitrary", "arbitrary") if ns else ("parallel", "arbitrary")),
    )(Qa, Ka, V, dO, lseT, dltT, dP, *scatter)
    return res[0], res[1], list(res[2:])


def _attn_bwd_dq(Qa, Ka, V, dO, lse, dlt, dP, *, tq):
    S = Qa.shape[1]
    nq = S // tq

    def body(qa_ref, ka_ref, v_ref, do_ref, lse_ref, dl_ref, dp_in, dq_o, dcq_o, s_s, dp_s, ds_s, dq_s, dcq_s):
        del dp_in
        i, j = pl.program_id(0), pl.program_id(1)

        @pl.when(j == 0)
        def _():
            dq_s[...] = jnp.zeros_like(dq_s)
            dcq_s[...] = jnp.zeros_like(dcq_s)

        def mm(h, b):
            s_s[b] = _dot(qa_ref[h], ka_ref[h], NT_DIMS)
            dp_s[b] = _dot(do_ref[h], v_ref[h], NT_DIMS)

        def step(masked):
            mm(0, 0)
            for h in range(HC):
                b = h % 2
                if h + 1 < HC:
                    mm(h + 1, 1 - b)
                for rc in range(tq // RC):
                    rows = slice(rc * RC, (rc + 1) * RC)
                    lb = lse_ref[h, rows]
                    db = dl_ref[h, rows]
                    acc = None
                    for c, sc in enumerate(_pieces(s_s.at[b], rows, rc, tq, masked, True)):
                        if sc is None:
                            ds_s[rows, _lanes(c)] = jnp.zeros((RC, LW), BF)
                            continue
                        ds = jnp.exp(sc - lb) * (dp_s[b, rows, _lanes(c)] - db)
                        ds_s[rows, _lanes(c)] = ds.astype(BF)
                        acc = ds if acc is None else acc + ds
                    dcq_s[h, rows] += acc
                dq_s[h] += _dot(ds_s[...], ka_ref[h])

        @pl.when(j < i)
        def _():
            step(False)

        @pl.when(j == i)
        def _():
            step(True)
            for h in range(HC):
                dq_o[:, DH * h:DH * (h + 1)] = (dq_s[h][:, 0:DH] * SCALE).astype(BF)
            dcq_o[...] = _lane_pack([jnp.sum(dcq_s[h], axis=1, keepdims=True) for h in range(HC)], tq)

    def qspec(w):
        return pl.BlockSpec((HC, tq, w), lambda i, j: (0, i, 0))

    def kspec(w):
        return pl.BlockSpec((HC, tq, w), lambda i, j: (0, jnp.minimum(i, j), 0))

    colv = pl.BlockSpec((tq, 128), lambda i, j: (i, 0))
    return pl.pallas_call(
        body, name="attn_bwd_dq", grid=(nq, nq),
        out_shape=(jax.ShapeDtypeStruct(dP.shape, BF), jax.ShapeDtypeStruct((S, 128), F32)),
        in_specs=[qspec(2 * DH), kspec(2 * DH), kspec(DH), qspec(DH), qspec(LW), qspec(LW), pl.BlockSpec(memory_space=pl.ANY)],
        out_specs=(pl.BlockSpec((tq, 512), lambda i, j: (i, C_Q // 512)), colv),
        scratch_shapes=[pltpu.VMEM((2, tq, tq), F32), pltpu.VMEM((2, tq, tq), F32), pltpu.VMEM((tq, tq), BF),
                        pltpu.VMEM((HC, tq, 2 * DH), F32), pltpu.VMEM((HC, tq, LW), F32)],
        input_output_aliases={6: 0},
        compiler_params=_cp(("parallel", "arbitrary")),
    )(Qa, Ka, V, dO, lse, dlt, dP)


def _xattn_probs(qm_ref, kv_ref, h):
    sl = slice(DH * h, DH * (h + 1))
    s = _dot(qm_ref[:, sl].astype(BF), kv_ref[:, sl].astype(BF), NT_DIMS) * SCALE
    p = jnp.exp(s - jnp.max(s, axis=1, keepdims=True))
    return p / jnp.sum(p, axis=1, keepdims=True)


def _xattn_fwd(P, kv, *, ts):
    S = P.shape[0]

    def body(qm_ref, kv_ref, zm_o, o_s):
        for h in range(HM):
            sl = slice(DH * h, DH * (h + 1))
            p = _xattn_probs(qm_ref, kv_ref, h)
            o_s[:, sl] = _dot(p.astype(BF), kv_ref[:, ML + DH * h:ML + DH * (h + 1)].astype(BF))
        mg = qm_ref[:, 256:512]
        zm_o[...] = (o_s[...] * mg * _sig(mg)).astype(BF)

    return pl.pallas_call(
        body, name="xattn_fwd", grid=(S // ts,),
        out_shape=jax.ShapeDtypeStruct((S, 256), BF),
        in_specs=[pl.BlockSpec((ts, 512), lambda i: (i, C_M // 512)), pl.BlockSpec((ML, 512), lambda i: (0, 0))],
        out_specs=pl.BlockSpec((ts, 256), lambda i: (i, 0)),
        scratch_shapes=[pltpu.VMEM((ts, 256), F32)],
        compiler_params=_cp(("parallel",)),
    )(P, kv)


def _xattn_bwd(P, kv, dzm, dP, *, ts):
    S = P.shape[0]

    def body(qm_ref, kv_ref, dz_ref, dp_in, dqm_o, dkv_o):
        del dp_in
        i = pl.program_id(0)

        @pl.when(i == 0)
        def _():
            dkv_o[...] = jnp.zeros_like(dkv_o)

        mg = qm_ref[:, 256:512]
        sg = _sig(mg)
        for h in range(HM):
            sl = slice(DH * h, DH * (h + 1))
            vsl = slice(ML + DH * h, ML + DH * (h + 1))
            p = _xattn_probs(qm_ref, kv_ref, h)
            pb = p.astype(BF)
            vh = kv_ref[:, vsl].astype(BF)
            o = _dot(pb, vh)
            dz = dz_ref[:, sl]
            do = dz * mg[:, sl] * sg[:, sl]
            dqm_o[:, 256 + DH * h:256 + DH * (h + 1)] = (dz * o * _dsilu(mg[:, sl], sg[:, sl])).astype(BF)
            dob = do.astype(BF)
            dpv = _dot(dob, vh, NT_DIMS)
            ds = p * (dpv - jnp.sum(do * o, axis=1, keepdims=True))
            dsb = ds.astype(BF)
            dqm_o[:, sl] = (_dot(dsb, kv_ref[:, sl].astype(BF)) * SCALE).astype(BF)
            dkv_o[:, sl] += _dot(dsb, qm_ref[:, sl].astype(BF), TN_DIMS) * SCALE
            dkv_o[:, vsl] += _dot(pb, dob, TN_DIMS)

    return pl.pallas_call(
        body, name="xattn_bwd", grid=(S // ts,),
        out_shape=(jax.ShapeDtypeStruct(dP.shape, BF), jax.ShapeDtypeStruct((ML, 512), F32)),
        in_specs=[pl.BlockSpec((ts, 512), lambda i: (i, C_M // 512)), pl.BlockSpec((ML, 512), lambda i: (0, 0)),
                  pl.BlockSpec((ts, 256), lambda i: (i, 0)), pl.BlockSpec(memory_space=pl.ANY)],
        out_specs=(pl.BlockSpec((ts, 512), lambda i: (i, C_M // 512)), pl.BlockSpec((ML, 512), lambda i: (0, 0))),
        input_output_aliases={3: 0},
        compiler_params=_cp(("arbitrary",)),
    )(P, kv, dzm, dP)


def _merge_fwd(za, zb, zc, zm, P, x, pa, pb, pc, pm, wo, lng, lnb, *, ts):
    S = x.shape[0]

    def body(za_r, zb_r, zc_r, zm_r, g_r, x_r, pa_r, pb_r, pc_r, pm_r, wo_r, lng_r, lnb_r,
             mg_o, ya_o, yb_o, yc_o, ym_o, xn_o, xh_o, rs_o):
        merged = jnp.zeros((ts, D), F32)
        for t, (z_r, p_r, y_o) in enumerate(((za_r, pa_r, ya_o), (zb_r, pb_r, yb_o), (zc_r, pc_r, yc_o), (zm_r, pm_r, ym_o))):
            y = _dot(z_r[...], p_r[...])
            merged = merged + _sig(g_r[:, D * t:D * (t + 1)]) * y
            y_o[...] = y.astype(BF)
        mb = merged.astype(BF)
        mg_o[...] = mb
        r = ALPHA * x_r[...] + _dot(mb, wo_r[...])
        xh, rstd = _ln_hat(r)
        xh_o[...] = xh
        rs_o[...] = rstd
        xn_o[...] = xh * lng_r[...] + lnb_r[...]

    def rows(w):
        return pl.BlockSpec((ts, w), lambda i: (i, 0))

    def full(a):
        return pl.BlockSpec(a.shape, lambda i: (0, 0))

    sd = lambda dt: jax.ShapeDtypeStruct((S, D), dt)
    return pl.pallas_call(
        body, name="merge_fwd", grid=(S // ts,),
        out_shape=(sd(BF), sd(BF), sd(BF), sd(BF), sd(BF), sd(F32), sd(F32), jax.ShapeDtypeStruct((S, 1), F32)),
        in_specs=[rows(512), rows(512), rows(512), rows(256), pl.BlockSpec((ts, 4 * D), lambda i: (i, 0)), rows(D),
                  full(pa), full(pb), full(pc), full(pm), full(wo), full(lng), full(lnb)],
        out_specs=(rows(D),) * 7 + (rows(1),),
        compiler_params=_cp(("parallel",)),
    )(za, zb, zc, zm, P, x, pa, pb, pc, pm, wo, lng, lnb)


def _loss_fwd(y, tgt, *, ts):
    S = y.shape[0]

    def body(y_r, t_r, dy_o, l_o):
        @pl.when(pl.program_id(0) == 0)
        def _():
            l_o[...] = jnp.zeros_like(l_o)

        e = y_r[...] - t_r[...]
        dy_o[...] = e / D
        l_o[...] += 0.5 * jnp.sum(_sum_r(e * e), axis=1, keepdims=True) / D

    rows = pl.BlockSpec((ts, D), lambda i: (i, 0))
    return pl.pallas_call(
        body, name="loss", grid=(S // ts,),
        out_shape=(jax.ShapeDtypeStruct((S, D), F32), jax.ShapeDtypeStruct((1, 1), F32)),
        in_specs=[rows, rows], out_specs=(rows, pl.BlockSpec((1, 1), lambda i: (0, 0))),
        compiler_params=_cp(("arbitrary",)),
    )(y, tgt)


def _out_bwd(dxn, xh, rstd, merged, wo, lng, *, ts):
    S = dxn.shape[0]

    def body(dxn_r, xh_r, rs_r, mg_r, wo_r, lng_r, dr_o, dm_o, dwo_o, dlng_o, dlnb_o):
        @pl.when(pl.program_id(0) == 0)
        def _():
            dwo_o[...] = jnp.zeros_like(dwo_o)
            dlng_o[...] = jnp.zeros_like(dlng_o)
            dlnb_o[...] = jnp.zeros_like(dlnb_o)

        dxn = dxn_r[...]
        xh = xh_r[...]
        dr = _ln_bwd(dxn * lng_r[...], xh, rs_r[...])
        dr_o[...] = dr
        drb = dr.astype(BF)
        dm_o[...] = _dot(drb, wo_r[...], NT_DIMS)
        dwo_o[...] += _dot(mg_r[...], drb, TN_DIMS)
        dlng_o[...] += _sum_r(dxn * xh)
        dlnb_o[...] += _sum_r(dxn)

    rows = pl.BlockSpec((ts, D), lambda i: (i, 0))
    full = lambda shape: pl.BlockSpec(shape, lambda i: (0, 0))
    sd = jax.ShapeDtypeStruct((S, D), F32)
    vec = jax.ShapeDtypeStruct((1, D), F32)
    return pl.pallas_call(
        body, name="out_bwd", grid=(S // ts,),
        out_shape=(sd, sd, jax.ShapeDtypeStruct((D, D), F32), vec, vec),
        in_specs=[rows, rows, pl.BlockSpec((ts, 1), lambda i: (i, 0)), rows, full((D, D)), full((1, D))],
        out_specs=(rows, rows, full((D, D)), full((1, D)), full((1, D))),
        compiler_params=_cp(("arbitrary",)),
    )(dxn, xh, rstd, merged, wo, lng)


def _merge_bwd(dm, P, ya, yb, yc, ym, za, zb, zc, zm, pa, pb, pc, pm, *, ts):
    S = dm.shape[0]

    def body(dm_r, g_r, ya_r, yb_r, yc_r, ym_r, za_r, zb_r, zc_r, zm_r, pa_r, pb_r, pc_r, pm_r,
             dg_o, dza_o, dzb_o, dzc_o, dzm_o, dpa_o, dpb_o, dpc_o, dpm_o):
        @pl.when(pl.program_id(0) == 0)
        def _():
            for o in (dpa_o, dpb_o, dpc_o, dpm_o):
                o[...] = jnp.zeros_like(o)

        dm = dm_r[...]
        for t, (y_r, z_r, p_r, dz_o, dp_o) in enumerate(((ya_r, za_r, pa_r, dza_o, dpa_o), (yb_r, zb_r, pb_r, dzb_o, dpb_o),
                                                        (yc_r, zc_r, pc_r, dzc_o, dpc_o), (ym_r, zm_r, pm_r, dzm_o, dpm_o))):
            gate = _sig(g_r[:, D * t:D * (t + 1)])
            dg_o[:, D * t:D * (t + 1)] = (dm * y_r[...].astype(F32) * gate * (1.0 - gate)).astype(BF)
            dyb = (dm * gate).astype(BF)
            dz_o[...] = _dot(dyb, p_r[...], NT_DIMS)
            dp_o[...] += _dot(z_r[...], dyb, TN_DIMS)

    def rows(w):
        return pl.BlockSpec((ts, w), lambda i: (i, 0))

    def full(a):
        return pl.BlockSpec(a.shape, lambda i: (0, 0))

    return pl.pallas_call(
        body, name="merge_bwd", grid=(S // ts,),
        out_shape=(jax.ShapeDtypeStruct((S, NP), BF),
                   jax.ShapeDtypeStruct((S, 512), F32), jax.ShapeDtypeStruct((S, 512), F32),
                   jax.ShapeDtypeStruct((S, 512), F32), jax.ShapeDtypeStruct((S, 256), F32),
                   jax.ShapeDtypeStruct(pa.shape, F32), jax.ShapeDtypeStruct(pb.shape, F32),
                   jax.ShapeDtypeStruct(pc.shape, F32), jax.ShapeDtypeStruct(pm.shape, F32)),
        in_specs=[rows(D), pl.BlockSpec((ts, 4 * D), lambda i: (i, 0)), rows(D), rows(D), rows(D), rows(D),
                  rows(512), rows(512), rows(512), rows(256), full(pa), full(pb), full(pc), full(pm)],
        out_specs=(pl.BlockSpec((ts, 4 * D), lambda i: (i, 0)), rows(512), rows(512), rows(512), rows(256),
                   full(pa), full(pb), full(pc), full(pm)),
        compiler_params=_cp(("arbitrary",)),
    )(dm, P, ya, yb, yc, ym, za, zb, zc, zm, pa, pb, pc, pm)


def _branch_bwd(P, ca, cb, u, vb, dza, dzb, dzc, oc, wA, gA, betaA, wB, dP, *, ts):
    S = P.shape[0]
    nt = S // ts

    def rev(i):
        return nt - 1 - i

    def rows(w):
        return pl.BlockSpec((ts, w), lambda i: (rev(i), 0))

    def halo(rows_):
        return pl.BlockSpec((rows_, 512), lambda i: (jnp.maximum(rev(i) * (ts // rows_) - 1, 0), 0))

    def full(shape):
        return pl.BlockSpec(shape, lambda i: (0, 0))

    def body(pg, ca_r, cb_r, u_r, vb_r, uh_r, vh_r, dza_r, dzb_r, dzc_r, oc_r, wA_r, gA_r, betaA_r, wB_r, dp_in,
             dpg_o, do_o, dl_o, dwA_o, dbA_o, dgA_o, dbetaA_o, dwB_o, dwinA, uwin, haloA, dwinB, vwin, haloB):
        del dp_in
        i = pl.program_id(0)
        nz = (rev(i) > 0).astype(F32)

        @pl.when(i == 0)
        def _():
            for o in (dwA_o, dbA_o, dgA_o, dbetaA_o, dwB_o, haloA, haloB):
                o[...] = jnp.zeros_like(o)

        def col(j):
            return pg[:, 512 * j:512 * (j + 1)]

        def put(j, val):
            dpg_o[:, 512 * j:512 * (j + 1)] = val.astype(BF)

        a_gate = col(2)
        xh, rstd = _ln_hat(ca_r[...])
        gA_v = gA_r[...]
        n = xh * gA_v + betaA_r[...]
        sn = _sig(n)
        a = n * sn
        sg = _sig(a_gate)
        dza = dza_r[...]
        put(2, dza * a * _dsilu(a_gate, sg))
        dn = dza * a_gate * sg * _dsilu(n, sn)
        dgA_o[...] += _sum_r(dn * xh)
        dbetaA_o[...] += _sum_r(dn)
        dca = _ln_bwd(dn * gA_v, xh, rstd)
        dbA_o[...] += _sum_r(dca)
        dwinA[0:ts, :] = dca
        dwinA[ts:, :] = haloA[...]
        haloA[...] = dca[0:HALO_A, :]
        uwin[0:HALO_A, :] = uh_r[...] * nz
        uwin[HALO_A:, :] = u_r[...]
        du = jnp.zeros((ts, 512), F32)
        for k in range(KA):
            du = du + dwinA[pl.ds(KA - 1 - k, ts), :] * wA_r[k:k + 1, :]
            dwA_o[k:k + 1, :] += _sum_r(dca * uwin[pl.ds(HALO_A - KA + 1 + k, ts), :])
        sv = _sig(col(1))
        put(0, du * sv)
        put(1, du * col(0) * sv * (1.0 - sv))

        b_gate = col(6)
        sgb = _sig(b_gate)
        cbv = cb_r[...]
        b_b = col(4)
        dzb = dzb_r[...]
        put(6, dzb * b_b * cbv * _dsilu(b_gate, sgb))
        dhb = dzb * b_gate * sgb
        put(4, dhb * cbv)
        dcb = dhb * b_b
        dwinB[0:ts, :] = dcb
        dwinB[ts:, :] = haloB[...]
        haloB[...] = dcb[0:HALO_B, :]
        vwin[0:HALO_B, :] = vh_r[...] * nz
        vwin[HALO_B:, :] = vb_r[...]
        dv = jnp.zeros((ts, 512), F32)
        for k in range(KB):
            dv = dv + dwinB[pl.ds(KB - 1 - k, ts), :] * wB_r[k:k + 1, :]
            dwB_o[k:k + 1, :] += _sum_r(dcb * vwin[pl.ds(HALO_B - KB + 1 + k, ts), :])
        put(5, dv * col(3))
        put(3, dv * col(5))

        c_gate = col(7)
        sgc = _sig(c_gate)
        dzc = dzc_r[...]
        ocv = oc_r[...]
        put(7, dzc * ocv * _dsilu(c_gate, sgc))
        do = dzc * c_gate * sgc
        for h in range(HC):
            do_o[h] = do[:, DH * h:DH * (h + 1)].astype(BF)
        dd = do * ocv
        for h in range(HC):
            dl_o[h] = jnp.broadcast_to(jnp.sum(dd[:, DH * h:DH * (h + 1)], axis=1, keepdims=True), (ts, LW))

    v512 = jax.ShapeDtypeStruct((1, 512), F32)
    return pl.pallas_call(
        body, name="branch_bwd", grid=(nt,),
        out_shape=(jax.ShapeDtypeStruct(dP.shape, BF), jax.ShapeDtypeStruct((HC, S, DH), BF), jax.ShapeDtypeStruct((HC, S, LW), F32),
                   jax.ShapeDtypeStruct((32, 512), F32), v512, v512, v512, jax.ShapeDtypeStruct((8, 512), F32)),
        in_specs=[pl.BlockSpec((ts, 4096), lambda i: (rev(i), C_AB // 4096)),
                  rows(512), rows(512), rows(512), rows(512), halo(HALO_A), halo(HALO_B),
                  rows(512), rows(512), rows(512), rows(512),
                  full((32, 512)), full((1, 512)), full((1, 512)), full((8, 512)), pl.BlockSpec(memory_space=pl.ANY)],
        out_specs=(pl.BlockSpec((ts, 4096), lambda i: (rev(i), C_AB // 4096)),
                   pl.BlockSpec((HC, ts, DH), lambda i: (0, rev(i), 0)), pl.BlockSpec((HC, ts, LW), lambda i: (0, rev(i), 0)),
                   full((32, 512)), full((1, 512)), full((1, 512)), full((1, 512)), full((8, 512))),
        scratch_shapes=[pltpu.VMEM((ts + HALO_A, 512), F32), pltpu.VMEM((ts + HALO_A, 512), F32), pltpu.VMEM((HALO_A, 512), F32),
                        pltpu.VMEM((ts + HALO_B, 512), F32), pltpu.VMEM((ts + HALO_B, 512), F32), pltpu.VMEM((HALO_B, 512), F32)],
        input_output_aliases={15: 0},
        compiler_params=_cp(("arbitrary",)),
    )(P, ca, cb, u, vb, u, vb, dza, dzb, dzc, oc, wA, gA, betaA, wB, dP)


def _cum_bwd(P, dcum, bfg, dP, *, ts):
    S = P.shape[0]
    nt = S // ts

    def body(f_ref, dc_ref, bf_r, dp_in, df_o, dbf_o, carry):
        del dp_in
        i = pl.program_id(0)

        @pl.when(i == 0)
        def _():
            carry[...] = jnp.zeros_like(carry)
            dbf_o[...] = jnp.zeros_like(dbf_o)

        r = lax.broadcasted_iota(jnp.int32, (ts, ts), 0)
        c = lax.broadcasted_iota(jnp.int32, (ts, ts), 1)
        tri = (r <= c).astype(F32)
        dlogf = jnp.dot(tri, dc_ref[...], precision=HIGHEST, preferred_element_type=F32) + carry[...]
        carry[...] = dlogf[0:1, :]
        x = f_ref[...] + bf_r[...]
        lane = lax.broadcasted_iota(jnp.int32, (ts, 128), 1)
        df = jnp.where(lane < HC, dlogf * _sig(-x), 0.0)
        df_o[...] = df.astype(BF)
        dbf_o[...] += _sum_r(df)

    blk = pl.BlockSpec((ts, 128), lambda i: (nt - 1 - i, C_F // 128))
    return pl.pallas_call(
        body, name="cum_bwd", grid=(nt,),
        out_shape=(jax.ShapeDtypeStruct(dP.shape, BF), jax.ShapeDtypeStruct((1, 128), F32)),
        in_specs=[blk, pl.BlockSpec((ts, 128), lambda i: (nt - 1 - i, 0)), pl.BlockSpec((1, 128), lambda i: (0, 0)),
                  pl.BlockSpec(memory_space=pl.ANY)],
        out_specs=(blk, pl.BlockSpec((1, 128), lambda i: (0, 0))),
        scratch_shapes=[pltpu.VMEM((1, 128), F32)],
        input_output_aliases={3: 0},
        compiler_params=_cp(("arbitrary",)),
    )(P, dcum, bfg, dP)


def _adamw(w, m, v, gparts, *, name, tr):
    rws, cols = w.shape
    tr = min(tr, rws)
    assert rws % tr == 0 and gparts.shape == (NDEV, rws, cols), (name, w.shape, gparts.shape)
    c1 = 1.0 - ADAM_B1 ** ADAM_STEP
    c2 = 1.0 - ADAM_B2 ** ADAM_STEP

    def body(w_r, m_r, v_r, g_r, g_o, d_o, m_o, v_o):
        g = g_r[0].astype(F32)
        for p in range(1, NDEV):
            g = g + g_r[p].astype(F32)
        mn = ADAM_B1 * m_r[...] + (1.0 - ADAM_B1) * g
        vn = ADAM_B2 * v_r[...] + (1.0 - ADAM_B2) * (g * g)
        g_o[...] = g
        m_o[...] = mn
        v_o[...] = vn
        d_o[...] = -ADAM_LR * ((mn / c1) / (jnp.sqrt(vn / c2) + ADAM_EPS) + ADAM_WD * w_r[...])

    blk = pl.BlockSpec((tr, cols), lambda i: (i, 0))
    shp = jax.ShapeDtypeStruct((rws, cols), F32)
    return pl.pallas_call(
        body, name=name, grid=(rws // tr,), out_shape=(shp,) * 4,
        in_specs=[blk, blk, blk, pl.BlockSpec((NDEV, tr, cols), lambda i: (0, i, 0))],
        out_specs=(blk,) * 4, compiler_params=_cp(("parallel",)),
    )(w, m, v, gparts)


def _slot(p):
    return 4 * p[0] + 2 * p[1] + p[2]


def _comm_sems(na):
    return [pltpu.SemaphoreType.DMA((na, 7)), pltpu.SemaphoreType.DMA((na, 7)), pltpu.SemaphoreType.DMA((na,))]


def _gather_copies(ins, outs, send_sems, recv_sems, local_sems):
    na = len(ins)
    x, y, c = lax.axis_index("x"), lax.axis_index("y"), lax.axis_index("c")
    me, sib = (x, y, c), (x, y, 1 - c)
    chips = [(1 - x, y), (x, 1 - y), (1 - x, 1 - y)]

    def cp(a, k, block, to, src=None):
        dst = outs[a].at[_slot(block)]
        return pltpu.make_async_remote_copy(src_ref=dst if src is None else src, dst_ref=dst,
                                            send_sem=send_sems.at[a, k], recv_sem=recv_sems.at[a, k],
                                            device_id=to, device_id_type=pl.DeviceIdType.MESH)

    def mine(a):
        return pltpu.make_async_copy(ins[a], outs[a].at[_slot(me)], local_sems.at[a])

    def first(a):
        return [cp(a, 0, me, sib, src=ins[a])] + [cp(a, 1 + j, me, (*chip, c), src=ins[a]) for j, chip in enumerate(chips)]

    def start():
        for a in range(na):
            mine(a).start()
            for f in first(a):
                f.start()

    def finish():
        for j, chip in enumerate(chips):
            for a in range(na):
                cp(a, 1 + j, (*chip, c), me).wait_recv()
                cp(a, 4 + j, (*chip, c), sib).start()
        for a in range(na):
            cp(a, 0, sib, me).wait_recv()
            for j, chip in enumerate(chips):
                cp(a, 4 + j, (*chip, 1 - c), me).wait_recv()
        for a in range(na):
            for f in first(a):
                f.wait_send()
            for j, chip in enumerate(chips):
                cp(a, 4 + j, (*chip, c), sib).wait_send()
            mine(a).wait()

    return start, finish


def _scatter_copies(ins, outs, send_sems, recv_sems, local_sems):
    na = len(ins)
    x, y, c = lax.axis_index("x"), lax.axis_index("y"), lax.axis_index("c")
    me = (x, y, c)
    peers = [(x ^ ((k >> 2) & 1), y ^ ((k >> 1) & 1), c ^ (k & 1)) for k in range(1, NDEV)]

    def cp(a, k, peer):
        return pltpu.make_async_remote_copy(src_ref=ins[a].at[_slot(peer)], dst_ref=outs[a].at[_slot(me)],
                                            send_sem=send_sems.at[a, k], recv_sem=recv_sems.at[a, k],
                                            device_id=peer, device_id_type=pl.DeviceIdType.MESH)

    def landed(a, k, peer):
        dst = outs[a].at[_slot(peer)]
        return pltpu.make_async_remote_copy(src_ref=dst, dst_ref=dst, send_sem=send_sems.at[a, k], recv_sem=recv_sems.at[a, k],
                                            device_id=peer, device_id_type=pl.DeviceIdType.MESH)

    def mine(a):
        return pltpu.make_async_copy(ins[a].at[_slot(me)], outs[a].at[_slot(me)], local_sems.at[a])

    def start():
        for a in range(na):
            mine(a).start()
            for k, peer in enumerate(peers):
                cp(a, k, peer).start()

    def finish():
        for a in range(na):
            for k, peer in enumerate(peers):
                landed(a, k, peer).wait_recv()
        for a in range(na):
            for k, peer in enumerate(peers):
                cp(a, k, peer).wait_send()
            mine(a).wait()

    return start, finish


def _comm_call(arrs, copies, out_shapes, *, name):
    na = len(arrs)

    def body(*refs):
        start, finish = copies(refs[:na], refs[na:2 * na], *refs[2 * na:])
        start()
        finish()

    anyspec = pl.BlockSpec(memory_space=pl.ANY)
    return pl.pallas_call(body, name=name, out_shape=out_shapes, in_specs=[anyspec] * na, out_specs=(anyspec,) * na,
                          scratch_shapes=_comm_sems(na))(*arrs)


def _gathered_shapes(arrs):
    return tuple(jax.ShapeDtypeStruct((NDEV,) + a.shape, a.dtype) for a in arrs)


def _same_shapes(arrs):
    return tuple(jax.ShapeDtypeStruct(a.shape, a.dtype) for a in arrs)


def _all_gather(arrs, *, name):
    return _comm_call(arrs, _gather_copies, _gathered_shapes(arrs), name=name)


def _all_to_all(arrs, *, name):
    return _comm_call(arrs, _scatter_copies, _same_shapes(arrs), name=name)


def _gathered_to_layout(g4):
    parts = []
    for a, b in _RUNS:
        for d in range(a // SHARD_IN, (b - 1) // SHARD_IN + 1):
            lo, hi = max(a, d * SHARD_IN), min(b, (d + 1) * SHARD_IN)
            parts.append(g4[d, ..., lo - d * SHARD_IN:hi - d * SHARD_IN])
    parts.append(jnp.zeros(g4.shape[1:-1] + (NP - IN_COLS,), g4.dtype))
    return jnp.concatenate(parts, axis=-1)


def _layout_to_shards(w):
    offs, off = {}, 0
    for a, b in _RUNS:
        offs[a] = (b, off)
        off += b - a
    shards = []
    for d in range(NDEV):
        parts = []
        for a in sorted(offs):
            b, off = offs[a]
            lo, hi = max(a, d * SHARD_IN), min(b, (d + 1) * SHARD_IN)
            if lo < hi:
                parts.append(w[..., off + lo - a:off + hi - a])
        shards.append(jnp.concatenate(parts, axis=-1))
    return jnp.stack(shards)


def _tiles(S):
    ts = min(256, S)
    tsb = min(128, S)
    tq = min(512, S)
    return ts, tsb, tq


def _layer_fwd(x, mem_n, w, gather=()):
    ts, _, tq = _tiles(x.shape[0])
    P = _mm(x, w["W"], name="proj_fwd", tm=1024, tn=1152, tk=D)
    kv = _mm(mem_n, w["w_kv_mem"], name="kv_fwd", tm=ML, tn=512, tk=D)
    za, zb, u, ca, vb, cb, cum = _pre_fwd(P, w["wA"], w["conv_a_b"], w["ln_a_g"], w["ln_a_b"], w["wB"], w["b_forget"], ts=ts)
    Qa, Ka, V = _attn_prep(P, cum, ts=ts)
    oc, zc, lse, gathered = _attn_fwd(P, Qa, Ka, V, tq=tq, gather=gather)
    zm = _xattn_fwd(P, kv, ts=tq)
    merged, ya, yb, yc, ym, xn, xh, rstd = _merge_fwd(za, zb, zc, zm, P, x, w["p_a"], w["p_b"], w["p_c"], w["p_m"], w["w_out"],
                                                      w["ln_g"], w["ln_b"], ts=ts)
    saved = (x, P, kv, za, zb, zc, zm, u, ca, vb, cb, Qa, Ka, V, oc, lse, merged, ya, yb, yc, ym, xh, rstd)
    return xn, saved, gathered


def _layer_bwd(dx, saved, mem_n, w, scatter=()):
    (xl, P, kv, za, zb, zc, zm, u, ca, vb, cb, Qa, Ka, V, oc, lse, merged, ya, yb, yc, ym, xh, rstd) = saved
    ts, tsb, tq = _tiles(xl.shape[0])
    g = {}
    dr, dm, g["w_out"], g["ln_g"], g["ln_b"] = _out_bwd(dx, xh, rstd, merged, w["w_out"], w["ln_g"], ts=ts)
    dP, dza, dzb, dzc, dzm, g["p_a"], g["p_b"], g["p_c"], g["p_m"] = _merge_bwd(
        dm, P, ya, yb, yc, ym, za, zb, zc, zm, w["p_a"], w["p_b"], w["p_c"], w["p_m"], ts=tsb)
    dP, do, dlt, dwA, g["conv_a_b"], g["ln_a_g"], g["ln_a_b"], dwB = _branch_bwd(
        P, ca, cb, u, vb, dza, dzb, dzc, oc, w["wA"], w["ln_a_g"], w["ln_a_b"], w["wB"], dP, ts=ts)
    g["conv_a_w"], g["conv_b_w"] = dwA[:KA], dwB[:KB]
    dP, dck, received = _attn_bwd_dkv(Qa, Ka, V, do, lse[:, :, 0], dlt[:, :, 0], dP, tq=tq, scatter=scatter)
    dP, dcq = _attn_bwd_dq(Qa, Ka, V, do, lse, dlt, dP, tq=tq)
    dP, dbf = _cum_bwd(P, dcq - dck, w["b_forget"], dP, ts=ts)
    g["b_forget"] = dbf[0, :HC]
    dP, dkv = _xattn_bwd(P, kv, dzm, dP, ts=tq)
    g["w_kv_mem"] = _mm(mem_n.T, dkv, name="wkv_bwd", tm=D, tn=512, tk=ML)
    dmem_n = _mm(dkv, w["w_kv_mem"], name="memn_bwd", nt=True, tm=ML, tn=D, tk=512)
    g["w_in"] = _mm(xl.T.astype(BF), dP, name="win_bwd", out_dtype=BF, tm=D, tn=1152, tk=1024)
    dx = _mm(dP, w["W"], name="x_bwd", nt=True, tm=1024, tn=D, tk=1152, add=dr, add_scale=ALPHA)
    return dx, g, dmem_n, received


_SMALL = (("b_forget", (NL, HC)), ("conv_a_b", (NL, 512)), ("ln_a_g", (NL, 512)), ("ln_a_b", (NL, 512)),
          ("mem_ln_g", (D,)), ("mem_ln_b", (D,)), ("ln_g", (NL, D)), ("ln_b", (NL, D)),
          ("conv_a_w", (NL, KA, 512)), ("conv_b_w", (NL, KB, 512)))


def _pack(parts, rows_mult=8):
    flat = jnp.concatenate([p.reshape(-1).astype(F32) for p in parts])
    n = flat.shape[0]
    rows = -(-n // 128)
    rows = -(-rows // rows_mult) * rows_mult
    return jnp.pad(flat, (0, rows * 128 - n)).reshape(rows, 128)


def _unpack(buf, shapes):
    flat = buf.reshape(-1)
    out, off = [], 0
    for shp in shapes:
        n = 1
        for d in shp:
            n *= d
        out.append(flat[off:off + n].reshape(shp))
        off += n
    return out


def kernel(x, mem, w_in, b_forget, conv_a_w, conv_a_b, ln_a_g, ln_a_b, conv_b_w, w_kv_mem, mem_ln_g, mem_ln_b, p_a, p_b, p_c, p_m, w_out, ln_g, ln_b, loss_target, m_w_in, m_b_forget, m_conv_a_w, m_conv_a_b, m_ln_a_g, m_ln_a_b, m_conv_b_w, m_w_kv_mem, m_mem_ln_g, m_mem_ln_b, m_p_a, m_p_b, m_p_c, m_p_m, m_w_out, m_ln_g, m_ln_b, v_w_in, v_b_forget, v_conv_a_w, v_conv_a_b, v_ln_a_g, v_ln_a_b, v_conv_b_w, v_w_kv_mem, v_mem_ln_g, v_mem_ln_b, v_p_a, v_p_b, v_p_c, v_p_m, v_w_out, v_ln_g, v_ln_b):
    wts = dict(w_in=w_in, b_forget=b_forget, conv_a_w=conv_a_w, conv_a_b=conv_a_b, ln_a_g=ln_a_g, ln_a_b=ln_a_b, conv_b_w=conv_b_w,
               w_kv_mem=w_kv_mem, mem_ln_g=mem_ln_g, mem_ln_b=mem_ln_b, p_a=p_a, p_b=p_b, p_c=p_c, p_m=p_m, w_out=w_out, ln_g=ln_g, ln_b=ln_b)
    mom = dict(w_in=m_w_in, b_forget=m_b_forget, conv_a_w=m_conv_a_w, conv_a_b=m_conv_a_b, ln_a_g=m_ln_a_g, ln_a_b=m_ln_a_b,
               conv_b_w=m_conv_b_w, w_kv_mem=m_w_kv_mem, mem_ln_g=m_mem_ln_g, mem_ln_b=m_mem_ln_b, p_a=m_p_a, p_b=m_p_b, p_c=m_p_c,
               p_m=m_p_m, w_out=m_w_out, ln_g=m_ln_g, ln_b=m_ln_b)
    vel = dict(w_in=v_w_in, b_forget=v_b_forget, conv_a_w=v_conv_a_w, conv_a_b=v_conv_a_b, ln_a_g=v_ln_a_g, ln_a_b=v_ln_a_b,
               conv_b_w=v_conv_b_w, w_kv_mem=v_w_kv_mem, mem_ln_g=v_mem_ln_g, mem_ln_b=v_mem_ln_b, p_a=v_p_a, p_b=v_p_b, p_c=v_p_c,
               p_m=v_p_m, w_out=v_w_out, ln_g=v_ln_g, ln_b=v_ln_b)
    names = ("w_in", "b_forget", "conv_a_w", "conv_a_b", "ln_a_g", "ln_a_b", "conv_b_w", "w_kv_mem", "mem_ln_g", "mem_ln_b",
             "p_a", "p_b", "p_c", "p_m", "w_out", "ln_g", "ln_b")
    mid = ("p_a", "p_b", "p_c", "p_m", "w_out", "w_kv_mem")
    me = 4 * lax.axis_index("x") + 2 * lax.axis_index("y") + lax.axis_index("c")

    row_sharded = ("w_out", "w_kv_mem")
    mid_shapes = [wts[n].shape[1:] for n in mid]
    mid_nrows = [s[0] * s[1] // 128 for s in mid_shapes]

    def mid_pack(d, l):
        return jnp.concatenate([d[n][l].reshape(-1, 128) for n in mid], axis=0)

    def layer_weights(l, g_win, g16, wA, wB):
        w = {"W": _gathered_to_layout(g_win), "wA": wA[l], "wB": wB[l]}
        off = 0
        for n, shp, nr in zip(mid, mid_shapes, mid_nrows):
            blk = g16[:, off:off + nr].reshape((NDEV,) + shp)
            off += nr
            w[n] = blk.reshape(NDEV * shp[0], shp[1]) if n in row_sharded else blk.transpose(1, 0, 2).reshape(shp[0], NDEV * shp[1])
        w["b_forget"] = jnp.pad(b_forget[l], (0, 128 - HC)).reshape(1, 128)
        for n, a in (("conv_a_b", conv_a_b), ("ln_a_g", ln_a_g), ("ln_a_b", ln_a_b), ("ln_g", ln_g), ("ln_b", ln_b)):
            w[n] = a[l].reshape(1, -1)
        return w

    def grad_chunks(g):
        parts = []
        for n, shp in zip(mid, mid_shapes):
            a = g[n]
            a = a.reshape(NDEV, shp[0], shp[1]) if n in row_sharded else a.reshape(shp[0], NDEV, shp[1]).transpose(1, 0, 2)
            parts.append(a.reshape(NDEV, -1, 128))
        return [_layout_to_shards(g["w_in"]), jnp.concatenate(parts, axis=1).astype(BF)]

    shards = [[w_in[l].astype(BF), mid_pack(wts, l).astype(BF)] for l in range(NL)]
    pk32 = jnp.concatenate([conv_a_w, conv_b_w], axis=1).reshape(NL * (KA + KB), 512 // NDEV)
    g_win, g16, g32 = _all_gather(shards[0] + [pk32], name="gather_first")
    conv = g32.reshape(NDEV, NL, KA + KB, 512 // NDEV).transpose(1, 2, 0, 3).reshape(NL, KA + KB, 512)
    wA = jnp.pad(conv[:, :KA], ((0, 0), (0, 32 - KA), (0, 0)))
    wB = jnp.pad(conv[:, KA:], ((0, 0), (0, 8 - KB), (0, 0)))

    mem_n, mem_hat = _mem_ln_fwd(mem[0], mem_ln_g.reshape(1, D), mem_ln_b.reshape(1, D))
    xl, lw, saved = x[0], [], []
    for l in range(NL):
        lw.append(layer_weights(l, g_win, g16, wA, wB))
        xl, sv, got = _layer_fwd(xl, mem_n, lw[l], gather=shards[l + 1] if l + 1 < NL else ())
        saved.append(sv)
        if got:
            g_win, g16 = got
    dx, loss = _loss_fwd(xl, loss_target[0], ts=_tiles(xl.shape[0])[0])
    loss = lax.psum(loss[0, 0], ("x", "y", "c"))

    g = [None] * NL
    dmem_n = [None] * NL
    recv = [None] * NL
    pending = ()
    for l in reversed(range(NL)):
        dx, g[l], dmem_n[l], got = _layer_bwd(dx, saved[l], mem_n, lw[l], scatter=pending)
        if got:
            recv[l + 1] = got
        pending = grad_chunks(g[l])
    recv[0] = _all_to_all(pending, name="scatter_last")
    gs = {n: jnp.stack([g[l][n].reshape(shp[1:]) for l in range(NL)]) for n, shp in _SMALL if len(shp) > 1}
    gs["mem_ln_g"], gs["mem_ln_b"] = _mem_ln_bwd(dmem_n, mem_hat)
    (r_small,) = _all_gather([_pack([gs[n] for n, _ in _SMALL])], name="gather_small")

    res = {}
    r_win = jnp.concatenate([recv[l][0] for l in range(NL)], axis=1)
    res["w_in"] = [a.reshape(NL, D, SHARD_IN) for a in
                   _adamw(w_in.reshape(NL * D, SHARD_IN), m_w_in.reshape(NL * D, SHARD_IN), v_w_in.reshape(NL * D, SHARD_IN),
                          r_win, name="adamw_w_in", tr=128)]
    pk = lambda d: jnp.concatenate([mid_pack(d, l) for l in range(NL)], axis=0)
    o16 = _adamw(pk(wts), pk(mom), pk(vel), jnp.concatenate([recv[l][1] for l in range(NL)], axis=1), name="adamw_mid", tr=1024)
    for idx, (n, shp, nr) in enumerate(zip(mid, mid_shapes, mid_nrows)):
        off = [l * sum(mid_nrows) + sum(mid_nrows[:idx]) for l in range(NL)]
        res[n] = [jnp.stack([o[f:f + nr].reshape(shp) for f in off]) for o in o16]

    def small_view(d, n):
        a = d[n]
        if n in ("conv_a_w", "conv_b_w"):
            fullw = jnp.zeros(a.shape[:2] + (512,), F32)
            return lax.dynamic_update_slice(fullw, a, (0, 0, me * (512 // NDEV)))
        return a

    spk = lambda d: _pack([small_view(d, n) for n, _ in _SMALL])
    osm = _adamw(spk(wts), spk(mom), spk(vel), r_small, name="adamw_small", tr=1024)
    osm = [_unpack(o, [s for _, s in _SMALL]) for o in osm]
    for idx, (n, _) in enumerate(_SMALL):
        vals = [o[idx] for o in osm]
        if n in ("conv_a_w", "conv_b_w"):
            vals = [lax.dynamic_slice(a, (0, 0, me * (512 // NDEV)), a.shape[:2] + (512 // NDEV,)) for a in vals]
        res[n] = vals

    outs = [loss, dx[None]]
    for k in range(4):
        outs += [res[n][k] for n in names]
    return tuple(outs)
```

```python
import jax
import jax.numpy as jnp
from jax import lax
from jax.experimental import pallas as pl
from jax.experimental.pallas import tpu as pltpu

F32 = jnp.float32
BF = jnp.bfloat16
HIGHEST = lax.Precision.HIGHEST

D = 1024
NL = 4
NDEV = 8
HC, DH = 8, 64
HM = 4
ML = 256
KA, KB = 31, 3
HALO_A, HALO_B = 32, 8
ALPHA = (2.0 * NL) ** 0.25
EPS = 1e-5
SCALE = DH ** -0.5
NEG = -1e30

ADAM_LR, ADAM_B1, ADAM_B2, ADAM_EPS, ADAM_WD, ADAM_STEP = 0.001, 0.9, 0.999, 1e-08, 0.01, 10

C_G = 0
C_AB = 4096
C_Q = 8192
C_M = 8704
C_KV = 9216
C_F = 10240
NP = 10368
_RUNS = ((6152, 10248), (0, 3584), (5128, 5640), (3584, 4096), (5640, 6152), (4096, 5120), (5120, 5128))
IN_COLS = 10248
SHARD_IN = IN_COLS // NDEV

VMEM_LIMIT = 56 * 1024 * 1024

NT_DIMS = (((1,), (1,)), ((), ()))
TN_DIMS = (((0,), (0,)), ((), ()))


def _cp(sem=None):
    return pltpu.CompilerParams(dimension_semantics=sem, vmem_limit_bytes=VMEM_LIMIT)


def _sig(x):
    return 1.0 / (1.0 + jnp.exp(-x))


def _dsilu(x, s):
    return s * (1.0 + x * (1.0 - s))


def _mean_l(x):
    return jnp.mean(x, axis=-1, keepdims=True)


def _sum_r(x):
    return jnp.sum(x, axis=0, keepdims=True)


def _ln_hat(x):
    mu = _mean_l(x)
    xc = x - mu
    rstd = lax.rsqrt(_mean_l(xc * xc) + EPS)
    return xc * rstd, rstd


def _ln_bwd(dxh, xh, rstd):
    return rstd * (dxh - _mean_l(dxh) - xh * _mean_l(dxh * xh))


def _dot(a, b, dims=None):
    if dims is None:
        return jnp.dot(a, b, preferred_element_type=F32)
    return lax.dot_general(a, b, dims, preferred_element_type=F32)


def _taps_by_phase(n_taps):
    return [(b, list(range(b, n_taps, 8))) for b in range(min(8, n_taps))]


CONV_ROWS = 64


def _conv_chunks(ts, width):
    return [(slice(r, r + CONV_ROWS), r, slice(c, c + 128)) for c in range(0, width, 128) for r in range(0, ts, CONV_ROWS)]


def _dwconv(out_ref, win_ref, w_ref, n_taps, first_row, ts, reverse=False, bias_ref=None):
    for rows, r0, cl in _conv_chunks(ts, out_ref.shape[1]):
        acc = None
        for b, taps in _taps_by_phase(n_taps):
            vb = win_ref[pl.ds(first_row + b + r0, CONV_ROWS + 8 * (len(taps) - 1)), cl]
            for a, k in enumerate(taps):
                kw = n_taps - 1 - k if reverse else k
                term = vb[8 * a:8 * a + CONV_ROWS] * w_ref[kw:kw + 1, cl]
                acc = term if acc is None else acc + term
        out_ref[rows, cl] = acc if bias_ref is None else acc + bias_ref[:, cl]


def _dwcorr_acc(dwp_ref, x_ref, win_ref, n_taps, first_row, ts):
    for rows, r0, cl in _conv_chunks(ts, x_ref.shape[1]):
        xc = x_ref[rows, cl]
        for b, taps in _taps_by_phase(n_taps):
            vb = win_ref[pl.ds(first_row + b + r0, CONV_ROWS + 8 * (len(taps) - 1)), cl]
            for a, k in enumerate(taps):
                prod = xc * vb[8 * a:8 * a + CONV_ROWS]
                part = prod[0:8]
                for q in range(1, CONV_ROWS // 8):
                    part = part + prod[8 * q:8 * q + 8]
                dwp_ref[8 * k:8 * k + 8, cl] += part


def _lane_pack(cols, rows):
    lane = lax.broadcasted_iota(jnp.int32, (rows, 128), 1)
    out = jnp.zeros((rows, 128), F32)
    for h, c in enumerate(cols):
        out = jnp.where(lane == h, c, out)
    return out


def _mm(a, b, *, name, nt=False, out_dtype=F32, tm=512, tn=512, tk=512, add=None, add_scale=1.0, scatter=()):
    m, kdim = a.shape
    n = b.shape[0] if nt else b.shape[1]
    tm, tn, tk = min(tm, m), min(tn, n), min(tk, kdim)
    assert m % tm == 0 and n % tn == 0 and kdim % tk == 0, (name, a.shape, b.shape, tm, tn, tk)
    grid = (m // tm, n // tn, kdim // tk)
    nk = grid[2]
    nin = 2 if add is None else 3
    ns = len(scatter)

    def body(*refs):
        a_ref, b_ref = refs[:2]
        add_ref = None if add is None else refs[2]
        o_ref = refs[nin + ns]
        step = [pl.program_id(d) for d in range(3)]
        if ns:
            start, finish_comm = _scatter_copies(refs[nin:nin + ns], refs[nin + ns + 1:nin + 2 * ns + 1], *refs[-3:])
            pl.when((step[0] == 0) & (step[1] == 0) & (step[2] == 0))(start)

        def finish(r):
            if add is not None:
                r = r + add_scale * add_ref[...]
            o_ref[...] = r.astype(out_dtype)

        part = _dot(a_ref[...].astype(BF), b_ref[...].astype(BF), NT_DIMS if nt else None)
        if nk == 1:
            finish(part)
        else:
            acc_ref = refs[nin + 2 * ns + 1]
            k = step[2]

            @pl.when(k == 0)
            def _():
                acc_ref[...] = part

            @pl.when(k > 0)
            def _():
                acc_ref[...] += part

            @pl.when(k == nk - 1)
            def _():
                finish(acc_ref[...])

        if ns:
            pl.when((step[0] == grid[0] - 1) & (step[1] == grid[1] - 1) & (step[2] == nk - 1))(finish_comm)

    anyspec = pl.BlockSpec(memory_space=pl.ANY)
    in_specs = [pl.BlockSpec((tm, tk), lambda i, j, k: (i, k)),
                pl.BlockSpec((tn, tk), lambda i, j, k: (j, k)) if nt else pl.BlockSpec((tk, tn), lambda i, j, k: (k, j))]
    args = [a, b]
    if add is not None:
        in_specs.append(pl.BlockSpec((tm, tn), lambda i, j, k: (i, j)))
        args.append(add)
    res = pl.pallas_call(
        body, name=name, grid=grid,
        out_shape=(jax.ShapeDtypeStruct((m, n), out_dtype),) + _same_shapes(scatter),
        in_specs=in_specs + [anyspec] * ns, out_specs=(pl.BlockSpec((tm, tn), lambda i, j, k: (i, j)),) + (anyspec,) * ns,
        scratch_shapes=([pltpu.VMEM((tm, tn), F32)] if nk > 1 else []) + (_comm_sems(ns) if ns else []),
        compiler_params=_cp(("arbitrary",) * 3 if ns else ("parallel", "parallel", "arbitrary")),
    )(*args, *scatter)
    return (res[0], list(res[1:])) if ns else res[0]


def _mem_ln_fwd(mem, g, b):
    def body(m_ref, g_ref, b_ref, n_ref, h_ref):
        xh, _ = _ln_hat(m_ref[...])
        h_ref[...] = xh
        n_ref[...] = xh * g_ref[...] + b_ref[...]

    shp = jax.ShapeDtypeStruct(mem.shape, F32)
    return pl.pallas_call(body, name="mem_ln_fwd", out_shape=(shp, shp), compiler_params=_cp())(mem, g, b)


def _mem_ln_bwd(dns, mhat):
    def body(*refs):
        d_refs, h_ref, dg_ref, db_ref = refs[:NL], refs[NL], refs[NL + 1], refs[NL + 2]
        dn = d_refs[0][...]
        for r in d_refs[1:]:
            dn = dn + r[...]
        dg_ref[...] = _sum_r(dn * h_ref[...])
        db_ref[...] = _sum_r(dn)

    shp = jax.ShapeDtypeStruct((1, D), F32)
    return pl.pallas_call(body, name="mem_ln_bwd", out_shape=(shp, shp), compiler_params=_cp())(*dns, mhat)


def _pre_fwd(P, wA, bA, gA, betaA, wB, bfg, *, ts):
    S = P.shape[0]
    nt = S // ts
    cb = C_AB // 512

    def cur(j):
        return pl.BlockSpec((ts, 512), lambda i, j=j: (i, cb + j))

    def halo(j, rows):
        return pl.BlockSpec((rows, 512), lambda i, j=j: (jnp.maximum(i * (ts // rows) - 1, 0), cb + j))

    def full(shape):
        return pl.BlockSpec(shape, lambda i: (0, 0))

    def body(au, av, ag, bh, bb, bc, bg, f_ref, au_h, av_h, bh_h, bc_h, wA_r, bA_r, gA_r, betaA_r, wB_r, bf_r,
             za_o, zb_o, u_o, ca_o, vb_o, cb_o, cum_o, winA, winB, carry):
        i = pl.program_id(0)
        nz = (i > 0).astype(F32)

        u = au[...] * _sig(av[...])
        winA[0:HALO_A, :] = au_h[...] * _sig(av_h[...]) * nz
        winA[HALO_A:, :] = u
        _dwconv(ca_o, winA, wA_r, KA, HALO_A - KA + 1, ts, bias_ref=bA_r)
        ca = ca_o[...]
        xh, _ = _ln_hat(ca)
        n = xh * gA_r[...] + betaA_r[...]
        a = n * _sig(n)
        agv = ag[...]
        za_o[...] = (a * agv * _sig(agv)).astype(BF)
        u_o[...] = u

        vb = bc[...] * bh[...]
        winB[0:HALO_B, :] = bc_h[...] * bh_h[...] * nz
        winB[HALO_B:, :] = vb
        _dwconv(cb_o, winB, wB_r, KB, HALO_B - KB + 1, ts)
        accb = cb_o[...]
        bgv = bg[...]
        zb_o[...] = (bb[...] * accb * bgv * _sig(bgv)).astype(BF)
        vb_o[...] = vb

        @pl.when(i == 0)
        def _():
            carry[...] = jnp.zeros_like(carry)

        x = f_ref[...] + bf_r[...]
        logf = jnp.minimum(x, 0.0) - jnp.log1p(jnp.exp(-jnp.abs(x)))
        r = lax.broadcasted_iota(jnp.int32, (ts, ts), 0)
        c = lax.broadcasted_iota(jnp.int32, (ts, ts), 1)
        tri = (r >= c).astype(F32)
        cum = jnp.dot(tri, logf, precision=HIGHEST, preferred_element_type=F32) + carry[...]
        cum_o[...] = cum
        carry[...] = cum[ts - 1:ts, :]

    s512 = lambda dt: jax.ShapeDtypeStruct((S, 512), dt)
    o512 = pl.BlockSpec((ts, 512), lambda i: (i, 0))
    return pl.pallas_call(
        body, name="pre_fwd", grid=(nt,),
        out_shape=(s512(BF), s512(BF), s512(F32), s512(F32), s512(F32), s512(F32), jax.ShapeDtypeStruct((S, 128), F32)),
        in_specs=[cur(0), cur(1), cur(2), cur(3), cur(4), cur(5), cur(6),
                  pl.BlockSpec((ts, 128), lambda i: (i, C_F // 128)),
                  halo(0, HALO_A), halo(1, HALO_A), halo(3, HALO_B), halo(5, HALO_B),
                  full((32, 512)), full((1, 512)), full((1, 512)), full((1, 512)), full((8, 512)), full((1, 128))],
        out_specs=(o512, o512, o512, o512, o512, o512, pl.BlockSpec((ts, 128), lambda i: (i, 0))),
        scratch_shapes=[pltpu.VMEM((ts + HALO_A, 512), F32), pltpu.VMEM((ts + HALO_B, 512), F32), pltpu.VMEM((1, 128), F32)],
        compiler_params=_cp(("arbitrary",)),
    )(P, P, P, P, P, P, P, P, P, P, P, P, wA, bA, gA, betaA, wB, bfg)


RC = 32


def _split3(c):
    c1 = c.astype(BF).astype(F32)
    r = c - c1
    c2 = r.astype(BF).astype(F32)
    return c1, c2, r - c2


def _attn_prep(P, cum, *, ts):
    S = P.shape[0]

    def body(q_ref, kv_ref, cum_ref, qa_o, ka_o, v_o):
        lane = lax.broadcasted_iota(jnp.int32, (ts, DH), 1)
        for h in range(HC):
            sl = slice(DH * h, DH * (h + 1))
            c1, c2, c3 = _split3(cum_ref[:, h:h + 1])
            lo = jnp.where(lane == 0, c1, jnp.where(lane == 1, c2, jnp.where(lane == 2, c3, 0.0)))
            hi = jnp.where(lane == 3, c1, jnp.where(lane == 4, c2, jnp.where(lane == 5, c3, 0.0)))
            qa_o[h, :, 0:DH] = (q_ref[:, sl] * SCALE).astype(BF)
            qa_o[h, :, DH:2 * DH] = (lo + jnp.where((lane >= 3) & (lane < 6), 1.0, 0.0)).astype(BF)
            ka_o[h, :, 0:DH] = kv_ref[:, sl].astype(BF)
            ka_o[h, :, DH:2 * DH] = (jnp.where(lane < 3, 1.0, 0.0) - hi).astype(BF)
            v_o[h] = kv_ref[:, 512 + DH * h:512 + DH * (h + 1)].astype(BF)

    aug = jax.ShapeDtypeStruct((HC, S, 2 * DH), BF)
    return pl.pallas_call(
        body, name="attn_prep", grid=(S // ts,),
        out_shape=(aug, aug, jax.ShapeDtypeStruct((HC, S, DH), BF)),
        in_specs=[pl.BlockSpec((ts, 512), lambda i: (i, C_Q // 512)), pl.BlockSpec((ts, 1024), lambda i: (i, C_KV // 1024)),
                  pl.BlockSpec((ts, 128), lambda i: (i, 0))],
        out_specs=(pl.BlockSpec((HC, ts, 2 * DH), lambda i: (0, i, 0)), pl.BlockSpec((HC, ts, 2 * DH), lambda i: (0, i, 0)),
                   pl.BlockSpec((HC, ts, DH), lambda i: (0, i, 0))),
        compiler_params=_cp(("parallel",)),
    )(P, P, cum)


LW = 128


def _lanes(c):
    return slice(LW * c, LW * (c + 1))


def _diag_slices(rc, n, rows_are_queries):
    out = []
    for c in range(n // LW):
        r0, r1, c0, c1 = rc * RC, rc * RC + RC - 1, LW * c, LW * c + LW - 1
        lo, hi = (c1 <= r0, c0 > r1) if rows_are_queries else (r1 <= c0, r0 > c1)
        if lo:
            out.append("all")
        elif hi:
            out.append("none")
        else:
            r = lax.broadcasted_iota(jnp.int32, (RC, LW), 0) + r0
            cc = lax.broadcasted_iota(jnp.int32, (RC, LW), 1) + c0
            out.append((r >= cc) if rows_are_queries else (cc >= r))
    return out


def _pieces(ref2d, rows, rc, n, masked, rows_are_queries):
    kinds = _diag_slices(rc, n, rows_are_queries) if masked else ["all"] * (n // LW)
    out = []
    for c, kind in enumerate(kinds):
        if isinstance(kind, str):
            out.append(ref2d[rows, _lanes(c)] if kind == "all" else None)
        else:
            out.append(jnp.where(kind, ref2d[rows, _lanes(c)], NEG))
    return out


def _attn_fwd(P, Qa, Ka, V, *, tq, gather=()):
    S = P.shape[0]
    nq = S // tq
    ng = len(gather)

    def body(*refs):
        qa_ref, ka_ref, v_ref, cg_ref = refs[:4]
        o_ref, zc_ref, lse_ref = refs[4 + ng:7 + ng]
        s_s, p_s, m_s, l_s, acc_s, pm_s, al_s = refs[7 + 2 * ng:14 + 2 * ng]
        i, j = pl.program_id(0), pl.program_id(1)
        if ng:
            start, finish = _gather_copies(refs[4:4 + ng], refs[7 + ng:7 + 2 * ng], *refs[14 + 2 * ng:])
            pl.when((i == 0) & (j == 0))(start)

        @pl.when(j == 0)
        def _():
            m_s[...] = jnp.full_like(m_s, NEG)
            l_s[...] = jnp.zeros_like(l_s)
            acc_s[...] = jnp.zeros_like(acc_s)

        def step(masked):
            s_s[0] = _dot(qa_ref[0], ka_ref[0], NT_DIMS)
            for h in range(HC):
                b = h % 2
                if h + 1 < HC:
                    s_s[1 - b] = _dot(qa_ref[h + 1], ka_ref[h + 1], NT_DIMS)
                for rc in range(tq // RC):
                    rows = slice(rc * RC, (rc + 1) * RC)
                    pm = None
                    for sc in _pieces(s_s.at[b], rows, rc, tq, masked, True):
                        if sc is not None:
                            pm = sc if pm is None else jnp.maximum(pm, sc)
                    pm_s[rows, :] = pm
                m_prev = m_s[h]
                m_new = jnp.maximum(m_prev, jnp.max(pm_s[...], axis=1, keepdims=True))
                alpha = jnp.exp(m_prev - m_new)
                m_s[h] = m_new
                al_s[...] = alpha
                for rc in range(tq // RC):
                    rows = slice(rc * RC, (rc + 1) * RC)
                    mb = m_s[h, rows]
                    ps = None
                    for c, sc in enumerate(_pieces(s_s.at[b], rows, rc, tq, masked, True)):
                        if sc is None:
                            p_s[b, rows, _lanes(c)] = jnp.zeros((RC, LW), BF)
                            continue
                        p = jnp.exp(sc - mb)
                        ps = p if ps is None else ps + p
                        p_s[b, rows, _lanes(c)] = p.astype(BF)
                    l_s[h, rows] = al_s[rows] * l_s[h, rows] + ps
                acc_s[h] = al_s[:, 0:DH] * acc_s[h] + _dot(p_s[b], v_ref[h])

        @pl.when(j < i)
        def _():
            step(False)

        @pl.when(j == i)
        def _():
            step(True)
            for h in range(HC):
                l = jnp.sum(l_s[h], axis=1, keepdims=True)
                o_ref[:, DH * h:DH * (h + 1)] = acc_s[h] / l
                lse_ref[h] = m_s[h] + jnp.log(l)
            cg = cg_ref[...]
            zc_ref[...] = (o_ref[...] * cg * _sig(cg)).astype(BF)

        if ng:
            pl.when((i == nq - 1) & (j == nq - 1))(finish)

    stat = pltpu.VMEM((HC, tq, LW), F32)
    anyspec = pl.BlockSpec(memory_space=pl.ANY)
    res = pl.pallas_call(
        body, name="attn_fwd_gather" if ng else "attn_fwd", grid=(nq, nq),
        out_shape=(jax.ShapeDtypeStruct((S, 512), F32), jax.ShapeDtypeStruct((S, 512), BF), jax.ShapeDtypeStruct((HC, S, LW), F32))
        + _gathered_shapes(gather),
        in_specs=[pl.BlockSpec((HC, tq, 2 * DH), lambda i, j: (0, i, 0)),
                  pl.BlockSpec((HC, tq, 2 * DH), lambda i, j: (0, jnp.minimum(i, j), 0)),
                  pl.BlockSpec((HC, tq, DH), lambda i, j: (0, jnp.minimum(i, j), 0)),
                  pl.BlockSpec((tq, 512), lambda i, j: (i, C_AB // 512 + 7))] + [anyspec] * ng,
        out_specs=(pl.BlockSpec((tq, 512), lambda i, j: (i, 0)), pl.BlockSpec((tq, 512), lambda i, j: (i, 0)),
                   pl.BlockSpec((HC, tq, LW), lambda i, j: (0, i, 0))) + (anyspec,) * ng,
        scratch_shapes=[pltpu.VMEM((2, tq, tq), F32), pltpu.VMEM((2, tq, tq), BF), stat, stat, pltpu.VMEM((HC, tq, DH), F32),
                        pltpu.VMEM((tq, LW), F32), pltpu.VMEM((tq, LW), F32)] + (_comm_sems(ng) if ng else []),
        compiler_params=_cp(("arbitrary", "arbitrary") if ng else ("parallel", "arbitrary")),
    )(Qa, Ka, V, P, *gather)
    return res[0], res[1], res[2], list(res[3:])


def _attn_bwd_dkv(Qa, Ka, V, dO, lseT, dltT, dP, *, tq, scatter=()):
    S = Qa.shape[1]
    nq = S // tq
    ns = len(scatter)

    def body(*refs):
        qa_ref, ka_ref, v_ref, do_ref, lse_ref, dl_ref = refs[:6]
        dkv_o, dck_o = refs[7 + ns:9 + ns]
        s_s, dp_s, p_s, ds_s, dk_s, dv_s, dck_s = refs[9 + 2 * ns:16 + 2 * ns]
        j, i = pl.program_id(0), pl.program_id(1)
        if ns:
            start, finish = _scatter_copies(refs[7:7 + ns], refs[9 + ns:9 + 2 * ns], *refs[16 + 2 * ns:])
            pl.when((i == 0) & (j == 0))(start)

        @pl.when(i == 0)
        def _():
            dk_s[...] = jnp.zeros_like(dk_s)
            dv_s[...] = jnp.zeros_like(dv_s)
            dck_s[...] = jnp.zeros_like(dck_s)

        def mm(h, b):
            s_s[b] = _dot(ka_ref[h], qa_ref[h], NT_DIMS)
            dp_s[b] = _dot(v_ref[h], do_ref[h], NT_DIMS)

        def step(masked):
            mm(0, 0)
            for h in range(HC):
                b = h % 2
                if h + 1 < HC:
                    mm(h + 1, 1 - b)
                for rc in range(tq // RC):
                    rows = slice(rc * RC, (rc + 1) * RC)
                    acc = None
                    for c, sc in enumerate(_pieces(s_s.at[b], rows, rc, tq, masked, False)):
                        if sc is None:
                            p_s[rows, _lanes(c)] = jnp.zeros((RC, LW), BF)
                            ds_s[rows, _lanes(c)] = jnp.zeros((RC, LW), BF)
                            continue
                        p = jnp.exp(sc - lse_ref[h:h + 1, _lanes(c)])
                        ds = p * (dp_s[b, rows, _lanes(c)] - dl_ref[h:h + 1, _lanes(c)])
                        p_s[rows, _lanes(c)] = p.astype(BF)
                        ds_s[rows, _lanes(c)] = ds.astype(BF)
                        acc = ds if acc is None else acc + ds
                    dck_s[h, rows] += acc
                dv_s[h] += _dot(p_s[...], do_ref[h])
                dk_s[h] += _dot(ds_s[...], qa_ref[h])

        @pl.when(i > j)
        def _():
            step(False)

        @pl.when(i == j)
        def _():
            step(True)

        @pl.when(i == nq - 1)
        def _():
            for h in range(HC):
                dkv_o[:, DH * h:DH * (h + 1)] = dk_s[h][:, 0:DH].astype(BF)
                dkv_o[:, 512 + DH * h:512 + DH * (h + 1)] = dv_s[h].astype(BF)
            dck_o[...] = _lane_pack([jnp.sum(dck_s[h], axis=1, keepdims=True) for h in range(HC)], tq)

        if ns:
            pl.when((i == nq - 1) & (j == nq - 1))(finish)

    def qspec(w):
        return pl.BlockSpec((HC, tq, w), lambda j, i: (0, jnp.maximum(i, j), 0))

    def kspec(w):
        return pl.BlockSpec((HC, tq, w), lambda j, i: (0, j, 0))

    rowv = pl.BlockSpec((8, tq), lambda j, i: (0, jnp.maximum(i, j)))
    anyspec = pl.BlockSpec(memory_space=pl.ANY)
    res = pl.pallas_call(
        body, name="attn_bwd_dkv_scatter" if ns else "attn_bwd_dkv", grid=(nq, nq),
        out_shape=(jax.ShapeDtypeStruct(dP.shape, BF), jax.ShapeDtypeStruct((S, 128), F32)) + _same_shapes(scatter),
        in_specs=[qspec(2 * DH), kspec(2 * DH), kspec(DH), qspec(DH), rowv, rowv, anyspec] + [anyspec] * ns,
        out_specs=(pl.BlockSpec((tq, 1024), lambda j, i: (j, C_KV // 1024)), pl.BlockSpec((tq, 128), lambda j, i: (j, 0)))
        + (anyspec,) * ns,
        scratch_shapes=[pltpu.VMEM((2, tq, tq), F32), pltpu.VMEM((2, tq, tq), F32), pltpu.VMEM((tq, tq), BF), pltpu.VMEM((tq, tq), BF),
                        pltpu.VMEM((HC, tq, 2 * DH), F32), pltpu.VMEM((HC, tq, DH), F32), pltpu.VMEM((HC, tq, LW), F32)]
        + (_comm_sems(ns) if ns else []),
        input_output_aliases={6: 0},
        compiler_params=_cp(("arbitrary", "arbitrary") if ns else ("parallel", "arbitrary")),
    )(Qa, Ka, V, dO, lseT, dltT, dP, *scatter)
    return res[0], res[1], list(res[2:])


def _attn_bwd_dq(Qa, Ka, V, dO, lse, dlt, dP, *, tq):
    S = Qa.shape[1]
    nq = S // tq

    def body(qa_ref, ka_ref, v_ref, do_ref, lse_ref, dl_ref, dp_in, dq_o, dcq_o, s_s, dp_s, ds_s, dq_s, dcq_s):
        del dp_in
        i, j = pl.program_id(0), pl.program_id(1)

        @pl.when(j == 0)
        def _():
            dq_s[...] = jnp.zeros_like(dq_s)
            dcq_s[...] = jnp.zeros_like(dcq_s)

        def mm(h, b):
            s_s[b] = _dot(qa_ref[h], ka_ref[h], NT_DIMS)
            dp_s[b] = _dot(do_ref[h], v_ref[h], NT_DIMS)

        def step(masked):
            mm(0, 0)
            for h in range(HC):
                b = h % 2
                if h + 1 < HC:
                    mm(h + 1, 1 - b)
                for rc in range(tq // RC):
                    rows = slice(rc * RC, (rc + 1) * RC)
                    lb = lse_ref[h, rows]
                    db = dl_ref[h, rows]
                    acc = None
                    for c, sc in enumerate(_pieces(s_s.at[b], rows, rc, tq, masked, True)):
                        if sc is None:
                            ds_s[rows, _lanes(c)] = jnp.zeros((RC, LW), BF)
                            continue
                        ds = jnp.exp(sc - lb) * (dp_s[b, rows, _lanes(c)] - db)
                        ds_s[rows, _lanes(c)] = ds.astype(BF)
                        acc = ds if acc is None else acc + ds
                    dcq_s[h, rows] += acc
                dq_s[h] += _dot(ds_s[...], ka_ref[h])

        @pl.when(j < i)
        def _():
            step(False)

        @pl.when(j == i)
        def _():
            step(True)
            for h in range(HC):
                dq_o[:, DH * h:DH * (h + 1)] = (dq_s[h][:, 0:DH] * SCALE).astype(BF)
            dcq_o[...] = _lane_pack([jnp.sum(dcq_s[h], axis=1, keepdims=True) for h in range(HC)], tq)

    def qspec(w):
        return pl.BlockSpec((HC, tq, w), lambda i, j: (0, i, 0))

    def kspec(w):
        return pl.BlockSpec((HC, tq, w), lambda i, j: (0, jnp.minimum(i, j), 0))

    colv = pl.BlockSpec((tq, 128), lambda i, j: (i, 0))
    return pl.pallas_call(
        body, name="attn_bwd_dq", grid=(nq, nq),
        out_shape=(jax.ShapeDtypeStruct(dP.shape, BF), jax.ShapeDtypeStruct((S, 128), F32)),
        in_specs=[qspec(2 * DH), kspec(2 * DH), kspec(DH), qspec(DH), qspec(LW), qspec(LW), pl.BlockSpec(memory_space=pl.ANY)],
        out_specs=(pl.BlockSpec((tq, 512), lambda i, j: (i, C_Q // 512)), colv),
        scratch_shapes=[pltpu.VMEM((2, tq, tq), F32), pltpu.VMEM((2, tq, tq), F32), pltpu.VMEM((tq, tq), BF),
                        pltpu.VMEM((HC, tq, 2 * DH), F32), pltpu.VMEM((HC, tq, LW), F32)],
        input_output_aliases={6: 0},
        compiler_params=_cp(("parallel", "arbitrary")),
    )(Qa, Ka, V, dO, lse, dlt, dP)


def _xattn_probs(qm_ref, kv_ref, h):
    sl = slice(DH * h, DH * (h + 1))
    s = _dot(qm_ref[:, sl].astype(BF), kv_ref[:, sl].astype(BF), NT_DIMS) * SCALE
    p = jnp.exp(s - jnp.max(s, axis=1, keepdims=True))
    return p / jnp.sum(p, axis=1, keepdims=True)


def _xattn_fwd(P, kv, *, ts):
    S = P.shape[0]

    def body(qm_ref, kv_ref, zm_o, o_s):
        for h in range(HM):
            sl = slice(DH * h, DH * (h + 1))
            p = _xattn_probs(qm_ref, kv_ref, h)
            o_s[:, sl] = _dot(p.astype(BF), kv_ref[:, ML + DH * h:ML + DH * (h + 1)].astype(BF))
        mg = qm_ref[:, 256:512]
        zm_o[...] = (o_s[...] * mg * _sig(mg)).astype(BF)

    return pl.pallas_call(
        body, name="xattn_fwd", grid=(S // ts,),
        out_shape=jax.ShapeDtypeStruct((S, 256), BF),
        in_specs=[pl.BlockSpec((ts, 512), lambda i: (i, C_M // 512)), pl.BlockSpec((ML, 512), lambda i: (0, 0))],
        out_specs=pl.BlockSpec((ts, 256), lambda i: (i, 0)),
        scratch_shapes=[pltpu.VMEM((ts, 256), F32)],
        compiler_params=_cp(("parallel",)),
    )(P, kv)


def _xattn_bwd(P, kv, dzm, dP, *, ts):
    S = P.shape[0]

    def body(qm_ref, kv_ref, dz_ref, dp_in, dqm_o, dkv_o):
        del dp_in
        i = pl.program_id(0)

        @pl.when(i == 0)
        def _():
            dkv_o[...] = jnp.zeros_like(dkv_o)

        mg = qm_ref[:, 256:512]
        sg = _sig(mg)
        for h in range(HM):
            sl = slice(DH * h, DH * (h + 1))
            vsl = slice(ML + DH * h, ML + DH * (h + 1))
            p = _xattn_probs(qm_ref, kv_ref, h)
            pb = p.astype(BF)
            vh = kv_ref[:, vsl].astype(BF)
            o = _dot(pb, vh)
            dz = dz_ref[:, sl]
            do = dz * mg[:, sl] * sg[:, sl]
            dqm_o[:, 256 + DH * h:256 + DH * (h + 1)] = (dz * o * _dsilu(mg[:, sl], sg[:, sl])).astype(BF)
            dob = do.astype(BF)
            dpv = _dot(dob, vh, NT_DIMS)
            ds = p * (dpv - jnp.sum(do * o, axis=1, keepdims=True))
            dsb = ds.astype(BF)
            dqm_o[:, sl] = (_dot(dsb, kv_ref[:, sl].astype(BF)) * SCALE).astype(BF)
            dkv_o[:, sl] += _dot(dsb, qm_ref[:, sl].astype(BF), TN_DIMS) * SCALE
            dkv_o[:, vsl] += _dot(pb, dob, TN_DIMS)

    return pl.pallas_call(
        body, name="xattn_bwd", grid=(S // ts,),
        out_shape=(jax.ShapeDtypeStruct(dP.shape, BF), jax.ShapeDtypeStruct((ML, 512), F32)),
        in_specs=[pl.BlockSpec((ts, 512), lambda i: (i, C_M // 512)), pl.BlockSpec((ML, 512), lambda i: (0, 0)),
                  pl.BlockSpec((ts, 256), lambda i: (i, 0)), pl.BlockSpec(memory_space=pl.ANY)],
        out_specs=(pl.BlockSpec((ts, 512), lambda i: (i, C_M // 512)), pl.BlockSpec((ML, 512), lambda i: (0, 0))),
        input_output_aliases={3: 0},
        compiler_params=_cp(("arbitrary",)),
    )(P, kv, dzm, dP)


def _merge_fwd(za, zb, zc, zm, P, x, pa, pb, pc, pm, wo, lng, lnb, *, ts):
    S = x.shape[0]

    def body(za_r, zb_r, zc_r, zm_r, g_r, x_r, pa_r, pb_r, pc_r, pm_r, wo_r, lng_r, lnb_r,
             mg_o, ya_o, yb_o, yc_o, ym_o, xn_o, xh_o, rs_o):
        merged = jnp.zeros((ts, D), F32)
        for t, (z_r, p_r, y_o) in enumerate(((za_r, pa_r, ya_o), (zb_r, pb_r, yb_o), (zc_r, pc_r, yc_o), (zm_r, pm_r, ym_o))):
            y = _dot(z_r[...], p_r[...])
            merged = merged + _sig(g_r[:, D * t:D * (t + 1)]) * y
            y_o[...] = y.astype(BF)
        mb = merged.astype(BF)
        mg_o[...] = mb
        r = ALPHA * x_r[...] + _dot(mb, wo_r[...])
        xh, rstd = _ln_hat(r)
        xh_o[...] = xh
        rs_o[...] = rstd
        xn_o[...] = xh * lng_r[...] + lnb_r[...]

    def rows(w):
        return pl.BlockSpec((ts, w), lambda i: (i, 0))

    def full(a):
        return pl.BlockSpec(a.shape, lambda i: (0, 0))

    sd = lambda dt: jax.ShapeDtypeStruct((S, D), dt)
    return pl.pallas_call(
        body, name="merge_fwd", grid=(S // ts,),
        out_shape=(sd(BF), sd(BF), sd(BF), sd(BF), sd(BF), sd(F32), sd(F32), jax.ShapeDtypeStruct((S, 1), F32)),
        in_specs=[rows(512), rows(512), rows(512), rows(256), pl.BlockSpec((ts, 4 * D), lambda i: (i, 0)), rows(D),
                  full(pa), full(pb), full(pc), full(pm), full(wo), full(lng), full(lnb)],
        out_specs=(rows(D),) * 7 + (rows(1),),
        compiler_params=_cp(("parallel",)),
    )(za, zb, zc, zm, P, x, pa, pb, pc, pm, wo, lng, lnb)


def _loss_fwd(y, tgt, *, ts):
    S = y.shape[0]

    def body(y_r, t_r, dy_o, l_o):
        @pl.when(pl.program_id(0) == 0)
        def _():
            l_o[...] = jnp.zeros_like(l_o)

        e = y_r[...] - t_r[...]
        dy_o[...] = e / D
        l_o[...] += 0.5 * jnp.sum(_sum_r(e * e), axis=1, keepdims=True) / D

    rows = pl.BlockSpec((ts, D), lambda i: (i, 0))
    return pl.pallas_call(
        body, name="loss", grid=(S // ts,),
        out_shape=(jax.ShapeDtypeStruct((S, D), F32), jax.ShapeDtypeStruct((1, 1), F32)),
        in_specs=[rows, rows], out_specs=(rows, pl.BlockSpec((1, 1), lambda i: (0, 0))),
        compiler_params=_cp(("arbitrary",)),
    )(y, tgt)


def _out_bwd(dxn, xh, rstd, merged, wo, lng, *, ts):
    S = dxn.shape[0]

    def body(dxn_r, xh_r, rs_r, mg_r, wo_r, lng_r, dr_o, dm_o, dwo_o, dlng_o, dlnb_o):
        @pl.when(pl.program_id(0) == 0)
        def _():
            dwo_o[...] = jnp.zeros_like(dwo_o)
            dlng_o[...] = jnp.zeros_like(dlng_o)
            dlnb_o[...] = jnp.zeros_like(dlnb_o)

        dxn = dxn_r[...]
        xh = xh_r[...]
        dr = _ln_bwd(dxn * lng_r[...], xh, rs_r[...])
        dr_o[...] = dr
        drb = dr.astype(BF)
        dm_o[...] = _dot(drb, wo_r[...], NT_DIMS)
        dwo_o[...] += _dot(mg_r[...], drb, TN_DIMS)
        dlng_o[...] += _sum_r(dxn * xh)
        dlnb_o[...] += _sum_r(dxn)

    rows = pl.BlockSpec((ts, D), lambda i: (i, 0))
    full = lambda shape: pl.BlockSpec(shape, lambda i: (0, 0))
    sd = jax.ShapeDtypeStruct((S, D), F32)
    vec = jax.ShapeDtypeStruct((1, D), F32)
    return pl.pallas_call(
        body, name="out_bwd", grid=(S // ts,),
        out_shape=(sd, sd, jax.ShapeDtypeStruct((D, D), F32), vec, vec),
        in_specs=[rows, rows, pl.BlockSpec((ts, 1), lambda i: (i, 0)), rows, full((D, D)), full((1, D))],
        out_specs=(rows, rows, full((D, D)), full((1, D)), full((1, D))),
        compiler_params=_cp(("arbitrary",)),
    )(dxn, xh, rstd, merged, wo, lng)


def _merge_bwd(dm, P, ya, yb, yc, ym, za, zb, zc, zm, pa, pb, pc, pm, *, ts):
    S = dm.shape[0]

    def body(dm_r, g_r, ya_r, yb_r, yc_r, ym_r, za_r, zb_r, zc_r, zm_r, pa_r, pb_r, pc_r, pm_r,
             dg_o, dza_o, dzb_o, dzc_o, dzm_o, dpa_o, dpb_o, dpc_o, dpm_o):
        @pl.when(pl.program_id(0) == 0)
        def _():
            for o in (dpa_o, dpb_o, dpc_o, dpm_o):
                o[...] = jnp.zeros_like(o)

        dm = dm_r[...]
        for t, (y_r, z_r, p_r, dz_o, dp_o) in enumerate(((ya_r, za_r, pa_r, dza_o, dpa_o), (yb_r, zb_r, pb_r, dzb_o, dpb_o),
                                                        (yc_r, zc_r, pc_r, dzc_o, dpc_o), (ym_r, zm_r, pm_r, dzm_o, dpm_o))):
            gate = _sig(g_r[:, D * t:D * (t + 1)])
            dg_o[:, D * t:D * (t + 1)] = (dm * y_r[...].astype(F32) * gate * (1.0 - gate)).astype(BF)
            dyb = (dm * gate).astype(BF)
            dz_o[...] = _dot(dyb, p_r[...], NT_DIMS)
            dp_o[...] += _dot(z_r[...], dyb, TN_DIMS)

    def rows(w):
        return pl.BlockSpec((ts, w), lambda i: (i, 0))

    def full(a):
        return pl.BlockSpec(a.shape, lambda i: (0, 0))

    return pl.pallas_call(
        body, name="merge_bwd", grid=(S // ts,),
        out_shape=(jax.ShapeDtypeStruct((S, NP), BF),
                   jax.ShapeDtypeStruct((S, 512), F32), jax.ShapeDtypeStruct((S, 512), F32),
                   jax.ShapeDtypeStruct((S, 512), F32), jax.ShapeDtypeStruct((S, 256), F32),
                   jax.ShapeDtypeStruct(pa.shape, F32), jax.ShapeDtypeStruct(pb.shape, F32),
                   jax.ShapeDtypeStruct(pc.shape, F32), jax.ShapeDtypeStruct(pm.shape, F32)),
        in_specs=[rows(D), pl.BlockSpec((ts, 4 * D), lambda i: (i, 0)), rows(D), rows(D), rows(D), rows(D),
                  rows(512), rows(512), rows(512), rows(256), full(pa), full(pb), full(pc), full(pm)],
        out_specs=(pl.BlockSpec((ts, 4 * D), lambda i: (i, 0)), rows(512), rows(512), rows(512), rows(256),
                   full(pa), full(pb), full(pc), full(pm)),
        compiler_params=_cp(("arbitrary",)),
    )(dm, P, ya, yb, yc, ym, za, zb, zc, zm, pa, pb, pc, pm)


def _branch_bwd(P, ca, cb, u, vb, dza, dzb, dzc, oc, wA, gA, betaA, wB, dP, *, ts):
    S = P.shape[0]
    nt = S // ts

    def rev(i):
        return nt - 1 - i

    def rows(w):
        return pl.BlockSpec((ts, w), lambda i: (rev(i), 0))

    def halo(rows_):
        return pl.BlockSpec((rows_, 512), lambda i: (jnp.maximum(rev(i) * (ts // rows_) - 1, 0), 0))

    def full(shape):
        return pl.BlockSpec(shape, lambda i: (0, 0))

    def body(pg, ca_r, cb_r, u_r, vb_r, uh_r, vh_r, dza_r, dzb_r, dzc_r, oc_r, wA_r, gA_r, betaA_r, wB_r, dp_in,
             dpg_o, do_o, dl_o, dwA_o, dbA_o, dgA_o, dbetaA_o, dwB_o, dwinA, uwin, haloA, dwinB, vwin, haloB, cv_s, dwpA, dwpB):
        del dp_in
        i = pl.program_id(0)
        nz = (rev(i) > 0).astype(F32)

        @pl.when(i == 0)
        def _():
            for o in (dwA_o, dbA_o, dgA_o, dbetaA_o, dwB_o, haloA, haloB, dwpA, dwpB):
                o[...] = jnp.zeros_like(o)

        def col(j):
            return pg[:, 512 * j:512 * (j + 1)]

        def put(j, val):
            dpg_o[:, 512 * j:512 * (j + 1)] = val.astype(BF)

        a_gate = col(2)
        xh, rstd = _ln_hat(ca_r[...])
        gA_v = gA_r[...]
        n = xh * gA_v + betaA_r[...]
        sn = _sig(n)
        a = n * sn
        sg = _sig(a_gate)
        dza = dza_r[...]
        put(2, dza * a * _dsilu(a_gate, sg))
        dn = dza * a_gate * sg * _dsilu(n, sn)
        dgA_o[...] += _sum_r(dn * xh)
        dbetaA_o[...] += _sum_r(dn)
        dca = _ln_bwd(dn * gA_v, xh, rstd)
        dbA_o[...] += _sum_r(dca)
        dwinA[0:ts, :] = dca
        dwinA[ts:, :] = haloA[...]
        haloA[...] = dca[0:HALO_A, :]
        uwin[0:HALO_A, :] = uh_r[...] * nz
        uwin[HALO_A:, :] = u_r[...]
        _dwcorr_acc(dwpA, dwinA.at[pl.ds(0, ts)], uwin, KA, HALO_A - KA + 1, ts)
        _dwconv(cv_s, dwinA, wA_r, KA, 0, ts, reverse=True)
        du = cv_s[...]
        sv = _sig(col(1))
        put(0, du * sv)
        put(1, du * col(0) * sv * (1.0 - sv))

        b_gate = col(6)
        sgb = _sig(b_gate)
        cbv = cb_r[...]
        b_b = col(4)
        dzb = dzb_r[...]
        put(6, dzb * b_b * cbv * _dsilu(b_gate, sgb))
        dhb = dzb * b_gate * sgb
        put(4, dhb * cbv)
        dcb = dhb * b_b
        dwinB[0:ts, :] = dcb
        dwinB[ts:, :] = haloB[...]
        haloB[...] = dcb[0:HALO_B, :]
        vwin[0:HALO_B, :] = vh_r[...] * nz
        vwin[HALO_B:, :] = vb_r[...]
        _dwcorr_acc(dwpB, dwinB.at[pl.ds(0, ts)], vwin, KB, HALO_B - KB + 1, ts)
        _dwconv(cv_s, dwinB, wB_r, KB, 0, ts, reverse=True)
        dv = cv_s[...]
        put(5, dv * col(3))
        put(3, dv * col(5))

        c_gate = col(7)
        sgc = _sig(c_gate)
        dzc = dzc_r[...]
        ocv = oc_r[...]
        put(7, dzc * ocv * _dsilu(c_gate, sgc))
        do = dzc * c_gate * sgc
        for h in range(HC):
            do_o[h] = do[:, DH * h:DH * (h + 1)].astype(BF)
        dd = do * ocv
        for h in range(HC):
            dl_o[h] = jnp.broadcast_to(jnp.sum(dd[:, DH * h:DH * (h + 1)], axis=1, keepdims=True), (ts, LW))

        @pl.when(i == nt - 1)
        def _():
            for k in range(KA):
                dwA_o[k:k + 1, :] = _sum_r(dwpA[8 * k:8 * k + 8, :])
            for k in range(KB):
                dwB_o[k:k + 1, :] = _sum_r(dwpB[8 * k:8 * k + 8, :])

    v512 = jax.ShapeDtypeStruct((1, 512), F32)
    return pl.pallas_call(
        body, name="branch_bwd", grid=(nt,),
        out_shape=(jax.ShapeDtypeStruct(dP.shape, BF), jax.ShapeDtypeStruct((HC, S, DH), BF), jax.ShapeDtypeStruct((HC, S, LW), F32),
                   jax.ShapeDtypeStruct((32, 512), F32), v512, v512, v512, jax.ShapeDtypeStruct((8, 512), F32)),
        in_specs=[pl.BlockSpec((ts, 4096), lambda i: (rev(i), C_AB // 4096)),
                  rows(512), rows(512), rows(512), rows(512), halo(HALO_A), halo(HALO_B),
                  rows(512), rows(512), rows(512), rows(512),
                  full((32, 512)), full((1, 512)), full((1, 512)), full((8, 512)), pl.BlockSpec(memory_space=pl.ANY)],
        out_specs=(pl.BlockSpec((ts, 4096), lambda i: (rev(i), C_AB // 4096)),
                   pl.BlockSpec((HC, ts, DH), lambda i: (0, rev(i), 0)), pl.BlockSpec((HC, ts, LW), lambda i: (0, rev(i), 0)),
                   full((32, 512)), full((1, 512)), full((1, 512)), full((1, 512)), full((8, 512))),
        scratch_shapes=[pltpu.VMEM((ts + HALO_A, 512), F32), pltpu.VMEM((ts + HALO_A, 512), F32), pltpu.VMEM((HALO_A, 512), F32),
                        pltpu.VMEM((ts + HALO_B, 512), F32), pltpu.VMEM((ts + HALO_B, 512), F32), pltpu.VMEM((HALO_B, 512), F32),
                        pltpu.VMEM((ts, 512), F32), pltpu.VMEM((8 * 32, 512), F32), pltpu.VMEM((8 * 8, 512), F32)],
        input_output_aliases={15: 0},
        compiler_params=_cp(("arbitrary",)),
    )(P, ca, cb, u, vb, u, vb, dza, dzb, dzc, oc, wA, gA, betaA, wB, dP)


def _cum_bwd(P, dcum, bfg, dP, *, ts):
    S = P.shape[0]
    nt = S // ts

    def body(f_ref, dc_ref, bf_r, dp_in, df_o, dbf_o, carry):
        del dp_in
        i = pl.program_id(0)

        @pl.when(i == 0)
        def _():
            carry[...] = jnp.zeros_like(carry)
            dbf_o[...] = jnp.zeros_like(dbf_o)

        r = lax.broadcasted_iota(jnp.int32, (ts, ts), 0)
        c = lax.broadcasted_iota(jnp.int32, (ts, ts), 1)
        tri = (r <= c).astype(F32)
        dlogf = jnp.dot(tri, dc_ref[...], precision=HIGHEST, preferred_element_type=F32) + carry[...]
        carry[...] = dlogf[0:1, :]
        x = f_ref[...] + bf_r[...]
        lane = lax.broadcasted_iota(jnp.int32, (ts, 128), 1)
        df = jnp.where(lane < HC, dlogf * _sig(-x), 0.0)
        df_o[...] = df.astype(BF)
        dbf_o[...] += _sum_r(df)

    blk = pl.BlockSpec((ts, 128), lambda i: (nt - 1 - i, C_F // 128))
    return pl.pallas_call(
        body, name="cum_bwd", grid=(nt,),
        out_shape=(jax.ShapeDtypeStruct(dP.shape, BF), jax.ShapeDtypeStruct((1, 128), F32)),
        in_specs=[blk, pl.BlockSpec((ts, 128), lambda i: (nt - 1 - i, 0)), pl.BlockSpec((1, 128), lambda i: (0, 0)),
                  pl.BlockSpec(memory_space=pl.ANY)],
        out_specs=(blk, pl.BlockSpec((1, 128), lambda i: (0, 0))),
        scratch_shapes=[pltpu.VMEM((1, 128), F32)],
        input_output_aliases={3: 0},
        compiler_params=_cp(("arbitrary",)),
    )(P, dcum, bfg, dP)


def _adamw(w, m, v, gparts, *, name, tr):
    rws, cols = w.shape
    tr = min(tr, rws)
    assert rws % tr == 0 and gparts.shape == (NDEV, rws, cols), (name, w.shape, gparts.shape)
    c1 = 1.0 - ADAM_B1 ** ADAM_STEP
    c2 = 1.0 - ADAM_B2 ** ADAM_STEP

    def body(w_r, m_r, v_r, g_r, g_o, d_o, m_o, v_o):
        g = g_r[0].astype(F32)
        for p in range(1, NDEV):
            g = g + g_r[p].astype(F32)
        mn = ADAM_B1 * m_r[...] + (1.0 - ADAM_B1) * g
        vn = ADAM_B2 * v_r[...] + (1.0 - ADAM_B2) * (g * g)
        g_o[...] = g
        m_o[...] = mn
        v_o[...] = vn
        d_o[...] = -ADAM_LR * ((mn / c1) / (jnp.sqrt(vn / c2) + ADAM_EPS) + ADAM_WD * w_r[...])

    blk = pl.BlockSpec((tr, cols), lambda i: (i, 0))
    shp = jax.ShapeDtypeStruct((rws, cols), F32)
    return pl.pallas_call(
        body, name=name, grid=(rws // tr,), out_shape=(shp,) * 4,
        in_specs=[blk, blk, blk, pl.BlockSpec((NDEV, tr, cols), lambda i: (0, i, 0))],
        out_specs=(blk,) * 4, compiler_params=_cp(("parallel",)),
    )(w, m, v, gparts)


def _slot(p):
    return 4 * p[0] + 2 * p[1] + p[2]


def _comm_sems(na):
    return [pltpu.SemaphoreType.DMA((na, 7)), pltpu.SemaphoreType.DMA((na, 7)), pltpu.SemaphoreType.DMA((na,))]


def _gather_copies(ins, outs, send_sems, recv_sems, local_sems):
    na = len(ins)
    x, y, c = lax.axis_index("x"), lax.axis_index("y"), lax.axis_index("c")
    me, sib = (x, y, c), (x, y, 1 - c)
    chips = [(1 - x, y), (x, 1 - y), (1 - x, 1 - y)]

    def cp(a, k, block, to, src=None):
        dst = outs[a].at[_slot(block)]
        return pltpu.make_async_remote_copy(src_ref=dst if src is None else src, dst_ref=dst,
                                            send_sem=send_sems.at[a, k], recv_sem=recv_sems.at[a, k],
                                            device_id=to, device_id_type=pl.DeviceIdType.MESH)

    def mine(a):
        return pltpu.make_async_copy(ins[a], outs[a].at[_slot(me)], local_sems.at[a])

    def first(a):
        return [cp(a, 0, me, sib, src=ins[a])] + [cp(a, 1 + j, me, (*chip, c), src=ins[a]) for j, chip in enumerate(chips)]

    def start():
        for a in range(na):
            mine(a).start()
            for f in first(a):
                f.start()

    def finish():
        for j, chip in enumerate(chips):
            for a in range(na):
                cp(a, 1 + j, (*chip, c), me).wait_recv()
                cp(a, 4 + j, (*chip, c), sib).start()
        for a in range(na):
            cp(a, 0, sib, me).wait_recv()
            for j, chip in enumerate(chips):
                cp(a, 4 + j, (*chip, 1 - c), me).wait_recv()
        for a in range(na):
            for f in first(a):
                f.wait_send()
            for j, chip in enumerate(chips):
                cp(a, 4 + j, (*chip, c), sib).wait_send()
            mine(a).wait()

    return start, finish


def _scatter_copies(ins, outs, send_sems, recv_sems, local_sems):
    na = len(ins)
    x, y, c = lax.axis_index("x"), lax.axis_index("y"), lax.axis_index("c")
    me = (x, y, c)
    peers = [(x ^ ((k >> 2) & 1), y ^ ((k >> 1) & 1), c ^ (k & 1)) for k in range(1, NDEV)]

    def cp(a, k, peer):
        return pltpu.make_async_remote_copy(src_ref=ins[a].at[_slot(peer)], dst_ref=outs[a].at[_slot(me)],
                                            send_sem=send_sems.at[a, k], recv_sem=recv_sems.at[a, k],
                                            device_id=peer, device_id_type=pl.DeviceIdType.MESH)

    def landed(a, k, peer):
        dst = outs[a].at[_slot(peer)]
        return pltpu.make_async_remote_copy(src_ref=dst, dst_ref=dst, send_sem=send_sems.at[a, k], recv_sem=recv_sems.at[a, k],
                                            device_id=peer, device_id_type=pl.DeviceIdType.MESH)

    def mine(a):
        return pltpu.make_async_copy(ins[a].at[_slot(me)], outs[a].at[_slot(me)], local_sems.at[a])

    def start():
        for a in range(na):
            mine(a).start()
            for k, peer in enumerate(peers):
                cp(a, k, peer).start()

    def finish():
        for a in range(na):
            for k, peer in enumerate(peers):
                landed(a, k, peer).wait_recv()
        for a in range(na):
            for k, peer in enumerate(peers):
                cp(a, k, peer).wait_send()
            mine(a).wait()

    return start, finish


def _comm_call(arrs, copies, out_shapes, *, name):
    na = len(arrs)

    def body(*refs):
        start, finish = copies(refs[:na], refs[na:2 * na], *refs[2 * na:])
        start()
        finish()

    anyspec = pl.BlockSpec(memory_space=pl.ANY)
    return pl.pallas_call(body, name=name, out_shape=out_shapes, in_specs=[anyspec] * na, out_specs=(anyspec,) * na,
                          scratch_shapes=_comm_sems(na))(*arrs)


def _gathered_shapes(arrs):
    return tuple(jax.ShapeDtypeStruct((NDEV,) + a.shape, a.dtype) for a in arrs)


def _same_shapes(arrs):
    return tuple(jax.ShapeDtypeStruct(a.shape, a.dtype) for a in arrs)


def _all_gather(arrs, *, name):
    return _comm_call(arrs, _gather_copies, _gathered_shapes(arrs), name=name)


def _gathered_to_layout(g4):
    parts = []
    for a, b in _RUNS:
        for d in range(a // SHARD_IN, (b - 1) // SHARD_IN + 1):
            lo, hi = max(a, d * SHARD_IN), min(b, (d + 1) * SHARD_IN)
            parts.append(g4[d, ..., lo - d * SHARD_IN:hi - d * SHARD_IN])
    parts.append(jnp.zeros(g4.shape[1:-1] + (NP - IN_COLS,), g4.dtype))
    return jnp.concatenate(parts, axis=-1)


def _layout_to_shards(w):
    offs, off = {}, 0
    for a, b in _RUNS:
        offs[a] = (b, off)
        off += b - a
    shards = []
    for d in range(NDEV):
        parts = []
        for a in sorted(offs):
            b, off = offs[a]
            lo, hi = max(a, d * SHARD_IN), min(b, (d + 1) * SHARD_IN)
            if lo < hi:
                parts.append(w[..., off + lo - a:off + hi - a])
        shards.append(jnp.concatenate(parts, axis=-1))
    return jnp.stack(shards)


def _tiles(S):
    ts = min(256, S)
    tsb = min(128, S)
    tq = min(512, S)
    return ts, tsb, tq


def _layer_fwd(x, mem_n, w, gather=()):
    ts, _, tq = _tiles(x.shape[0])
    P = _mm(x, w["W"], name="proj_fwd", tm=1024, tn=1152, tk=D)
    kv = _mm(mem_n, w["w_kv_mem"], name="kv_fwd", tm=ML, tn=512, tk=D)
    za, zb, u, ca, vb, cb, cum = _pre_fwd(P, w["wA"], w["conv_a_b"], w["ln_a_g"], w["ln_a_b"], w["wB"], w["b_forget"], ts=ts)
    Qa, Ka, V = _attn_prep(P, cum, ts=ts)
    oc, zc, lse, gathered = _attn_fwd(P, Qa, Ka, V, tq=tq, gather=gather)
    zm = _xattn_fwd(P, kv, ts=tq)
    merged, ya, yb, yc, ym, xn, xh, rstd = _merge_fwd(za, zb, zc, zm, P, x, w["p_a"], w["p_b"], w["p_c"], w["p_m"], w["w_out"],
                                                      w["ln_g"], w["ln_b"], ts=ts)
    saved = (x, P, kv, za, zb, zc, zm, u, ca, vb, cb, Qa, Ka, V, oc, lse, merged, ya, yb, yc, ym, xh, rstd)
    return xn, saved, gathered


def _layer_bwd(dx, saved, mem_n, w, scatter=(), own_chunks=None):
    (xl, P, kv, za, zb, zc, zm, u, ca, vb, cb, Qa, Ka, V, oc, lse, merged, ya, yb, yc, ym, xh, rstd) = saved
    ts, tsb, tq = _tiles(xl.shape[0])
    g = {}
    dr, dm, g["w_out"], g["ln_g"], g["ln_b"] = _out_bwd(dx, xh, rstd, merged, w["w_out"], w["ln_g"], ts=ts)
    dP, dza, dzb, dzc, dzm, g["p_a"], g["p_b"], g["p_c"], g["p_m"] = _merge_bwd(
        dm, P, ya, yb, yc, ym, za, zb, zc, zm, w["p_a"], w["p_b"], w["p_c"], w["p_m"], ts=tsb)
    dP, do, dlt, dwA, g["conv_a_b"], g["ln_a_g"], g["ln_a_b"], dwB = _branch_bwd(
        P, ca, cb, u, vb, dza, dzb, dzc, oc, w["wA"], w["ln_a_g"], w["ln_a_b"], w["wB"], dP, ts=ts)
    g["conv_a_w"], g["conv_b_w"] = dwA[:KA], dwB[:KB]
    dP, dck, received = _attn_bwd_dkv(Qa, Ka, V, do, lse[:, :, 0], dlt[:, :, 0], dP, tq=tq, scatter=scatter)
    dP, dcq = _attn_bwd_dq(Qa, Ka, V, do, lse, dlt, dP, tq=tq)
    dP, dbf = _cum_bwd(P, dcq - dck, w["b_forget"], dP, ts=ts)
    g["b_forget"] = dbf[0, :HC]
    dP, dkv = _xattn_bwd(P, kv, dzm, dP, ts=tq)
    g["w_kv_mem"] = _mm(mem_n.T, dkv, name="wkv_bwd", tm=D, tn=512, tk=ML)
    dmem_n = _mm(dkv, w["w_kv_mem"], name="memn_bwd", nt=True, tm=ML, tn=D, tk=512)
    g["w_in"] = _mm(xl.T.astype(BF), dP, name="win_bwd", out_dtype=BF, tm=D, tn=1152, tk=1024)
    if own_chunks is None:
        dx = _mm(dP, w["W"], name="x_bwd", nt=True, tm=1024, tn=D, tk=1152, add=dr, add_scale=ALPHA)
        return dx, g, dmem_n, received, None
    dx, received_own = _mm(dP, w["W"], name="x_bwd_scatter", nt=True, tm=1024, tn=D, tk=1152, add=dr, add_scale=ALPHA,
                           scatter=own_chunks(g))
    return dx, g, dmem_n, received, received_own


_SMALL = (("b_forget", (NL, HC)), ("conv_a_b", (NL, 512)), ("ln_a_g", (NL, 512)), ("ln_a_b", (NL, 512)),
          ("mem_ln_g", (D,)), ("mem_ln_b", (D,)), ("ln_g", (NL, D)), ("ln_b", (NL, D)),
          ("conv_a_w", (NL, KA, 512)), ("conv_b_w", (NL, KB, 512)))


def _pack(parts, rows_mult=8):
    flat = jnp.concatenate([p.reshape(-1).astype(F32) for p in parts])
    n = flat.shape[0]
    rows = -(-n // 128)
    rows = -(-rows // rows_mult) * rows_mult
    return jnp.pad(flat, (0, rows * 128 - n)).reshape(rows, 128)


def _unpack(buf, shapes):
    flat = buf.reshape(-1)
    out, off = [], 0
    for shp in shapes:
        n = 1
        for d in shp:
            n *= d
        out.append(flat[off:off + n].reshape(shp))
        off += n
    return out


def kernel(x, mem, w_in, b_forget, conv_a_w, conv_a_b, ln_a_g, ln_a_b, conv_b_w, w_kv_mem, mem_ln_g, mem_ln_b, p_a, p_b, p_c, p_m, w_out, ln_g, ln_b, loss_target, m_w_in, m_b_forget, m_conv_a_w, m_conv_a_b, m_ln_a_g, m_ln_a_b, m_conv_b_w, m_w_kv_mem, m_mem_ln_g, m_mem_ln_b, m_p_a, m_p_b, m_p_c, m_p_m, m_w_out, m_ln_g, m_ln_b, v_w_in, v_b_forget, v_conv_a_w, v_conv_a_b, v_ln_a_g, v_ln_a_b, v_conv_b_w, v_w_kv_mem, v_mem_ln_g, v_mem_ln_b, v_p_a, v_p_b, v_p_c, v_p_m, v_w_out, v_ln_g, v_ln_b):
    wts = dict(w_in=w_in, b_forget=b_forget, conv_a_w=conv_a_w, conv_a_b=conv_a_b, ln_a_g=ln_a_g, ln_a_b=ln_a_b, conv_b_w=conv_b_w,
               w_kv_mem=w_kv_mem, mem_ln_g=mem_ln_g, mem_ln_b=mem_ln_b, p_a=p_a, p_b=p_b, p_c=p_c, p_m=p_m, w_out=w_out, ln_g=ln_g, ln_b=ln_b)
    mom = dict(w_in=m_w_in, b_forget=m_b_forget, conv_a_w=m_conv_a_w, conv_a_b=m_conv_a_b, ln_a_g=m_ln_a_g, ln_a_b=m_ln_a_b,
               conv_b_w=m_conv_b_w, w_kv_mem=m_w_kv_mem, mem_ln_g=m_mem_ln_g, mem_ln_b=m_mem_ln_b, p_a=m_p_a, p_b=m_p_b, p_c=m_p_c,
               p_m=m_p_m, w_out=m_w_out, ln_g=m_ln_g, ln_b=m_ln_b)
    vel = dict(w_in=v_w_in, b_forget=v_b_forget, conv_a_w=v_conv_a_w, conv_a_b=v_conv_a_b, ln_a_g=v_ln_a_g, ln_a_b=v_ln_a_b,
               conv_b_w=v_conv_b_w, w_kv_mem=v_w_kv_mem, mem_ln_g=v_mem_ln_g, mem_ln_b=v_mem_ln_b, p_a=v_p_a, p_b=v_p_b, p_c=v_p_c,
               p_m=v_p_m, w_out=v_w_out, ln_g=v_ln_g, ln_b=v_ln_b)
    names = ("w_in", "b_forget", "conv_a_w", "conv_a_b", "ln_a_g", "ln_a_b", "conv_b_w", "w_kv_mem", "mem_ln_g", "mem_ln_b",
             "p_a", "p_b", "p_c", "p_m", "w_out", "ln_g", "ln_b")
    mid = ("p_a", "p_b", "p_c", "p_m", "w_out", "w_kv_mem")
    me = 4 * lax.axis_index("x") + 2 * lax.axis_index("y") + lax.axis_index("c")

    row_sharded = ("w_out", "w_kv_mem")
    mid_shapes = [wts[n].shape[1:] for n in mid]
    mid_nrows = [s[0] * s[1] // 128 for s in mid_shapes]

    def mid_pack(d, l):
        return jnp.concatenate([d[n][l].reshape(-1, 128) for n in mid], axis=0)

    def layer_weights(l, g_win, g16, wA, wB):
        w = {"W": _gathered_to_layout(g_win), "wA": wA[l], "wB": wB[l]}
        off = 0
        for n, shp, nr in zip(mid, mid_shapes, mid_nrows):
            blk = g16[:, off:off + nr].reshape((NDEV,) + shp)
            off += nr
            w[n] = blk.reshape(NDEV * shp[0], shp[1]) if n in row_sharded else blk.transpose(1, 0, 2).reshape(shp[0], NDEV * shp[1])
        w["b_forget"] = jnp.pad(b_forget[l], (0, 128 - HC)).reshape(1, 128)
        for n, a in (("conv_a_b", conv_a_b), ("ln_a_g", ln_a_g), ("ln_a_b", ln_a_b), ("ln_g", ln_g), ("ln_b", ln_b)):
            w[n] = a[l].reshape(1, -1)
        return w

    def grad_chunks(g):
        parts = []
        for n, shp in zip(mid, mid_shapes):
            a = g[n]
            a = a.reshape(NDEV, shp[0], shp[1]) if n in row_sharded else a.reshape(shp[0], NDEV, shp[1]).transpose(1, 0, 2)
            parts.append(a.reshape(NDEV, -1, 128))
        return [_layout_to_shards(g["w_in"]), jnp.concatenate(parts, axis=1).astype(BF)]

    shards = [[w_in[l].astype(BF), mid_pack(wts, l).astype(BF)] for l in range(NL)]
    pk32 = jnp.concatenate([conv_a_w, conv_b_w], axis=1).reshape(NL * (KA + KB), 512 // NDEV)
    g_win, g16, g32 = _all_gather(shards[0] + [pk32], name="gather_first")
    conv = g32.reshape(NDEV, NL, KA + KB, 512 // NDEV).transpose(1, 2, 0, 3).reshape(NL, KA + KB, 512)
    wA = jnp.pad(conv[:, :KA], ((0, 0), (0, 32 - KA), (0, 0)))
    wB = jnp.pad(conv[:, KA:], ((0, 0), (0, 8 - KB), (0, 0)))

    mem_n, mem_hat = _mem_ln_fwd(mem[0], mem_ln_g.reshape(1, D), mem_ln_b.reshape(1, D))
    xl, lw, saved = x[0], [], []
    for l in range(NL):
        lw.append(layer_weights(l, g_win, g16, wA, wB))
        xl, sv, got = _layer_fwd(xl, mem_n, lw[l], gather=shards[l + 1] if l + 1 < NL else ())
        saved.append(sv)
        if got:
            g_win, g16 = got
    dx, loss = _loss_fwd(xl, loss_target[0], ts=_tiles(xl.shape[0])[0])
    loss = lax.psum(loss[0, 0], ("x", "y", "c"))

    g = [None] * NL
    dmem_n = [None] * NL
    recv = [None] * NL
    pending = ()
    for l in reversed(range(NL)):
        dx, g[l], dmem_n[l], got, recv[l] = _layer_bwd(dx, saved[l], mem_n, lw[l], scatter=pending,
                                                       own_chunks=grad_chunks if l == 0 else None)
        if got:
            recv[l + 1] = got
        pending = grad_chunks(g[l]) if l else ()
    gs = {n: jnp.stack([g[l][n].reshape(shp[1:]) for l in range(NL)]) for n, shp in _SMALL if len(shp) > 1}
    gs["mem_ln_g"], gs["mem_ln_b"] = _mem_ln_bwd(dmem_n, mem_hat)
    (r_small,) = _all_gather([_pack([gs[n] for n, _ in _SMALL])], name="gather_small")

    res = {}
    r_win = jnp.concatenate([recv[l][0] for l in range(NL)], axis=1)
    res["w_in"] = [a.reshape(NL, D, SHARD_IN) for a in
                   _adamw(w_in.reshape(NL * D, SHARD_IN), m_w_in.reshape(NL * D, SHARD_IN), v_w_in.reshape(NL * D, SHARD_IN),
                          r_win, name="adamw_w_in", tr=128)]
    pk = lambda d: jnp.concatenate([mid_pack(d, l) for l in range(NL)], axis=0)
    o16 = _adamw(pk(wts), pk(mom), pk(vel), jnp.concatenate([recv[l][1] for l in range(NL)], axis=1), name="adamw_mid", tr=1024)
    for idx, (n, shp, nr) in enumerate(zip(mid, mid_shapes, mid_nrows)):
        off = [l * sum(mid_nrows) + sum(mid_nrows[:idx]) for l in range(NL)]
        res[n] = [jnp.stack([o[f:f + nr].reshape(shp) for f in off]) for o in o16]

    def small_view(d, n):
        a = d[n]
        if n in ("conv_a_w", "conv_b_w"):
            fullw = jnp.zeros(a.shape[:2] + (512,), F32)
            return lax.dynamic_update_slice(fullw, a, (0, 0, me * (512 // NDEV)))
        return a

    spk = lambda d: _pack([small_view(d, n) for n, _ in _SMALL])
    osm = _adamw(spk(wts), spk(mom), spk(vel), r_small, name="adamw_small", tr=1024)
    osm = [_unpack(o, [s for _, s in _SMALL]) for o in osm]
    for idx, (n, _) in enumerate(_SMALL):
        vals = [o[idx] for o in osm]
        if n in ("conv_a_w", "conv_b_w"):
            vals = [lax.dynamic_slice(a, (0, 0, me * (512 // NDEV)), a.shape[:2] + (512 // NDEV,)) for a in vals]
        res[n] = vals

    outs = [loss, dx[None]]
    for k in range(4):
        outs += [res[n][k] for n in names]
    return tuple(outs)
```

```python
import jax
import jax.numpy as jnp
from jax import lax
from jax.experimental import pallas as pl
from jax.experimental.pallas import tpu as pltpu

F32 = jnp.float32
BF = jnp.bfloat16
HIGHEST = lax.Precision.HIGHEST

D = 1024
NL = 4
NDEV = 8
HC, DH = 8, 64
HM = 4
ML = 256
KA, KB = 31, 3
HALO_A, HALO_B = 32, 8
HALO_B16 = 16
ALPHA = (2.0 * NL) ** 0.25
EPS = 1e-5
SCALE = DH ** -0.5
NEG = -1e30

ADAM_LR, ADAM_B1, ADAM_B2, ADAM_EPS, ADAM_WD, ADAM_STEP = 0.001, 0.9, 0.999, 1e-08, 0.01, 10

C_G = 0
C_AB = 4096
C_Q = 8192
C_M = 8704
C_KV = 9216
C_F = 10240
NP = 10368
_RUNS = ((6152, 10248), (0, 3584), (5128, 5640), (3584, 4096), (5640, 6152), (4096, 5120), (5120, 5128))
IN_COLS = 10248
SHARD_IN = IN_COLS // NDEV

VMEM_LIMIT = 56 * 1024 * 1024

NT_DIMS = (((1,), (1,)), ((), ()))
TN_DIMS = (((0,), (0,)), ((), ()))


def _cp(sem=None):
    return pltpu.CompilerParams(dimension_semantics=sem, vmem_limit_bytes=VMEM_LIMIT)


def _sig(x):
    return 1.0 / (1.0 + jnp.exp(-x))


def _dsilu(x, s):
    return s * (1.0 + x * (1.0 - s))


def _mean_l(x):
    return jnp.mean(x, axis=-1, keepdims=True)


def _sum_r(x):
    return jnp.sum(x, axis=0, keepdims=True)


def _ln_hat(x):
    mu = _mean_l(x)
    xc = x - mu
    rstd = lax.rsqrt(_mean_l(xc * xc) + EPS)
    return xc * rstd, rstd


def _ln_bwd(dxh, xh, rstd):
    return rstd * (dxh - _mean_l(dxh) - xh * _mean_l(dxh * xh))


def _dot(a, b, dims=None):
    if dims is None:
        return jnp.dot(a, b, preferred_element_type=F32)
    return lax.dot_general(a, b, dims, preferred_element_type=F32)


def _taps_by_phase(n_taps):
    return [(b, list(range(b, n_taps, 8))) for b in range(min(8, n_taps))]


CONV_ROWS = 64


def _conv_chunks(ts, width):
    return [(slice(r, r + CONV_ROWS), r, slice(c, c + 128)) for c in range(0, width, 128) for r in range(0, ts, CONV_ROWS)]


def _dwconv(out_ref, win_ref, w_ref, n_taps, first_row, ts, reverse=False, bias_ref=None):
    for rows, r0, cl in _conv_chunks(ts, out_ref.shape[1]):
        acc = None
        for b, taps in _taps_by_phase(n_taps):
            vb = win_ref[pl.ds(first_row + b + r0, CONV_ROWS + 8 * (len(taps) - 1)), cl]
            for a, k in enumerate(taps):
                kw = n_taps - 1 - k if reverse else k
                term = vb[8 * a:8 * a + CONV_ROWS] * w_ref[kw:kw + 1, cl]
                acc = term if acc is None else acc + term
        out_ref[rows, cl] = acc if bias_ref is None else acc + bias_ref[:, cl]


def _dwcorr_acc(dwp_ref, x_ref, win_ref, n_taps, first_row, ts):
    for rows, r0, cl in _conv_chunks(ts, x_ref.shape[1]):
        xc = x_ref[rows, cl]
        for b, taps in _taps_by_phase(n_taps):
            vb = win_ref[pl.ds(first_row + b + r0, CONV_ROWS + 8 * (len(taps) - 1)), cl]
            for a, k in enumerate(taps):
                prod = xc * vb[8 * a:8 * a + CONV_ROWS]
                part = prod[0:8]
                for q in range(1, CONV_ROWS // 8):
                    part = part + prod[8 * q:8 * q + 8]
                dwp_ref[8 * k:8 * k + 8, cl] += part


def _lane_pack(cols, rows):
    lane = lax.broadcasted_iota(jnp.int32, (rows, 128), 1)
    out = jnp.zeros((rows, 128), F32)
    for h, c in enumerate(cols):
        out = jnp.where(lane == h, c, out)
    return out


def _mm(a, b, *, name, nt=False, out_dtype=F32, tm=512, tn=512, tk=512, add=None, add_scale=1.0, scatter=()):
    m, kdim = a.shape
    n = b.shape[0] if nt else b.shape[1]
    tm, tn, tk = min(tm, m), min(tn, n), min(tk, kdim)
    assert m % tm == 0 and n % tn == 0 and kdim % tk == 0, (name, a.shape, b.shape, tm, tn, tk)
    grid = (m // tm, n // tn, kdim // tk)
    nk = grid[2]
    nin = 2 if add is None else 3
    ns = len(scatter)

    def body(*refs):
        a_ref, b_ref = refs[:2]
        add_ref = None if add is None else refs[2]
        o_ref = refs[nin + ns]
        step = [pl.program_id(d) for d in range(3)]
        if ns:
            start, finish_comm = _scatter_copies(refs[nin:nin + ns], refs[nin + ns + 1:nin + 2 * ns + 1], *refs[-3:])
            pl.when((step[0] == 0) & (step[1] == 0) & (step[2] == 0))(start)

        def finish(r):
            if add is not None:
                r = r + add_scale * add_ref[...]
            o_ref[...] = r.astype(out_dtype)

        part = _dot(a_ref[...].astype(BF), b_ref[...].astype(BF), NT_DIMS if nt else None)
        if nk == 1:
            finish(part)
        else:
            acc_ref = refs[nin + 2 * ns + 1]
            k = step[2]

            @pl.when(k == 0)
            def _():
                acc_ref[...] = part

            @pl.when(k > 0)
            def _():
                acc_ref[...] += part

            @pl.when(k == nk - 1)
            def _():
                finish(acc_ref[...])

        if ns:
            pl.when((step[0] == grid[0] - 1) & (step[1] == grid[1] - 1) & (step[2] == nk - 1))(finish_comm)

    anyspec = pl.BlockSpec(memory_space=pl.ANY)
    in_specs = [pl.BlockSpec((tm, tk), lambda i, j, k: (i, k)),
                pl.BlockSpec((tn, tk), lambda i, j, k: (j, k)) if nt else pl.BlockSpec((tk, tn), lambda i, j, k: (k, j))]
    args = [a, b]
    if add is not None:
        in_specs.append(pl.BlockSpec((tm, tn), lambda i, j, k: (i, j)))
        args.append(add)
    res = pl.pallas_call(
        body, name=name, grid=grid,
        out_shape=(jax.ShapeDtypeStruct((m, n), out_dtype),) + _same_shapes(scatter),
        in_specs=in_specs + [anyspec] * ns, out_specs=(pl.BlockSpec((tm, tn), lambda i, j, k: (i, j)),) + (anyspec,) * ns,
        scratch_shapes=([pltpu.VMEM((tm, tn), F32)] if nk > 1 else []) + (_comm_sems(ns) if ns else []),
        compiler_params=_cp(("arbitrary",) * 3 if ns else ("parallel", "parallel", "arbitrary")),
    )(*args, *scatter)
    return (res[0], list(res[1:])) if ns else res[0]


def _mem_ln_fwd(mem, g, b):
    def body(m_ref, g_ref, b_ref, n_ref, h_ref):
        xh, _ = _ln_hat(m_ref[...])
        h_ref[...] = xh
        n_ref[...] = xh * g_ref[...] + b_ref[...]

    shp = jax.ShapeDtypeStruct(mem.shape, F32)
    return pl.pallas_call(body, name="mem_ln_fwd", out_shape=(shp, shp), compiler_params=_cp())(mem, g, b)


def _mem_ln_bwd(dns, mhat):
    def body(*refs):
        d_refs, h_ref, dg_ref, db_ref = refs[:NL], refs[NL], refs[NL + 1], refs[NL + 2]
        dn = d_refs[0][...]
        for r in d_refs[1:]:
            dn = dn + r[...]
        dg_ref[...] = _sum_r(dn * h_ref[...])
        db_ref[...] = _sum_r(dn)

    shp = jax.ShapeDtypeStruct((1, D), F32)
    return pl.pallas_call(body, name="mem_ln_bwd", out_shape=(shp, shp), compiler_params=_cp())(*dns, mhat)


def _pre_fwd(P, F, wA, bA, gA, betaA, wB, bfg, *, ts):
    S = P.shape[0]
    nt = S // ts
    cb = C_AB // 512

    def cur(j):
        return pl.BlockSpec((ts, 512), lambda i, j=j: (i, cb + j))

    def halo(j, rows):
        return pl.BlockSpec((rows, 512), lambda i, j=j: (jnp.maximum(i * (ts // rows) - 1, 0), cb + j))

    def full(shape):
        return pl.BlockSpec(shape, lambda i: (0, 0))

    def body(au, av, ag, bh, bb, bc, bg, f_ref, au_h, av_h, bh_h, bc_h, wA_r, bA_r, gA_r, betaA_r, wB_r, bf_r,
             za_o, zb_o, u_o, ca_o, vb_o, cb_o, cum_o, winA, winB, carry):
        i = pl.program_id(0)
        nz = (i > 0).astype(F32)

        f32 = lambda ref: ref[...].astype(F32)
        u = f32(au) * _sig(f32(av))
        winA[0:HALO_A, :] = f32(au_h) * _sig(f32(av_h)) * nz
        winA[HALO_A:, :] = u
        _dwconv(ca_o, winA, wA_r, KA, HALO_A - KA + 1, ts, bias_ref=bA_r)
        ca = ca_o[...]
        xh, _ = _ln_hat(ca)
        n = xh * gA_r[...] + betaA_r[...]
        a = n * _sig(n)
        agv = f32(ag)
        za_o[...] = (a * agv * _sig(agv)).astype(BF)
        u_o[...] = u

        vb = f32(bc) * f32(bh)
        winB[0:HALO_B, :] = (f32(bc_h) * f32(bh_h))[HALO_B16 - HALO_B:] * nz
        winB[HALO_B:, :] = vb
        _dwconv(cb_o, winB, wB_r, KB, HALO_B - KB + 1, ts)
        accb = cb_o[...]
        bgv = f32(bg)
        zb_o[...] = (f32(bb) * accb * bgv * _sig(bgv)).astype(BF)
        vb_o[...] = vb

        @pl.when(i == 0)
        def _():
            carry[...] = jnp.zeros_like(carry)

        x = f_ref[...] + bf_r[...]
        logf = jnp.minimum(x, 0.0) - jnp.log1p(jnp.exp(-jnp.abs(x)))
        r = lax.broadcasted_iota(jnp.int32, (ts, ts), 0)
        c = lax.broadcasted_iota(jnp.int32, (ts, ts), 1)
        tri = (r >= c).astype(F32)
        cum = jnp.dot(tri, logf, precision=HIGHEST, preferred_element_type=F32) + carry[...]
        cum_o[...] = cum
        carry[...] = cum[ts - 1:ts, :]

    s512 = lambda dt: jax.ShapeDtypeStruct((S, 512), dt)
    o512 = pl.BlockSpec((ts, 512), lambda i: (i, 0))
    return pl.pallas_call(
        body, name="pre_fwd", grid=(nt,),
        out_shape=(s512(BF), s512(BF), s512(F32), s512(F32), s512(F32), s512(F32), jax.ShapeDtypeStruct((S, 128), F32)),
        in_specs=[cur(0), cur(1), cur(2), cur(3), cur(4), cur(5), cur(6),
                  pl.BlockSpec((ts, 128), lambda i: (i, 0)),
                  halo(0, HALO_A), halo(1, HALO_A), halo(3, HALO_B16), halo(5, HALO_B16),
                  full((32, 512)), full((1, 512)), full((1, 512)), full((1, 512)), full((8, 512)), full((1, 128))],
        out_specs=(o512, o512, o512, o512, o512, o512, pl.BlockSpec((ts, 128), lambda i: (i, 0))),
        scratch_shapes=[pltpu.VMEM((ts + HALO_A, 512), F32), pltpu.VMEM((ts + HALO_B, 512), F32), pltpu.VMEM((1, 128), F32)],
        compiler_params=_cp(("arbitrary",)),
    )(P, P, P, P, P, P, P, F, P, P, P, P, wA, bA, gA, betaA, wB, bfg)


RC = 32


def _split3(c):
    c1 = c.astype(BF).astype(F32)
    r = c - c1
    c2 = r.astype(BF).astype(F32)
    return c1, c2, r - c2


def _attn_prep(P, cum, *, ts):
    S = P.shape[0]

    def body(q_ref, kv_ref, cum_ref, qa_o, ka_o, v_o):
        lane = lax.broadcasted_iota(jnp.int32, (ts, DH), 1)
        for h in range(HC):
            sl = slice(DH * h, DH * (h + 1))
            c1, c2, c3 = _split3(cum_ref[:, h:h + 1])
            lo = jnp.where(lane == 0, c1, jnp.where(lane == 1, c2, jnp.where(lane == 2, c3, 0.0)))
            hi = jnp.where(lane == 3, c1, jnp.where(lane == 4, c2, jnp.where(lane == 5, c3, 0.0)))
            qa_o[h, :, 0:DH] = (q_ref[:, sl].astype(F32) * SCALE).astype(BF)
            qa_o[h, :, DH:2 * DH] = (lo + jnp.where((lane >= 3) & (lane < 6), 1.0, 0.0)).astype(BF)
            ka_o[h, :, 0:DH] = kv_ref[:, sl].astype(BF)
            ka_o[h, :, DH:2 * DH] = (jnp.where(lane < 3, 1.0, 0.0) - hi).astype(BF)
            v_o[h] = kv_ref[:, 512 + DH * h:512 + DH * (h + 1)].astype(BF)

    aug = jax.ShapeDtypeStruct((HC, S, 2 * DH), BF)
    return pl.pallas_call(
        body, name="attn_prep", grid=(S // ts,),
        out_shape=(aug, aug, jax.ShapeDtypeStruct((HC, S, DH), BF)),
        in_specs=[pl.BlockSpec((ts, 512), lambda i: (i, C_Q // 512)), pl.BlockSpec((ts, 1024), lambda i: (i, C_KV // 1024)),
                  pl.BlockSpec((ts, 128), lambda i: (i, 0))],
        out_specs=(pl.BlockSpec((HC, ts, 2 * DH), lambda i: (0, i, 0)), pl.BlockSpec((HC, ts, 2 * DH), lambda i: (0, i, 0)),
                   pl.BlockSpec((HC, ts, DH), lambda i: (0, i, 0))),
        compiler_params=_cp(("parallel",)),
    )(P, P, cum)


LW = 128


def _lanes(c):
    return slice(LW * c, LW * (c + 1))


def _diag_slices(rc, n, rows_are_queries):
    out = []
    for c in range(n // LW):
        r0, r1, c0, c1 = rc * RC, rc * RC + RC - 1, LW * c, LW * c + LW - 1
        lo, hi = (c1 <= r0, c0 > r1) if rows_are_queries else (r1 <= c0, r0 > c1)
        if lo:
            out.append("all")
        elif hi:
            out.append("none")
        else:
            r = lax.broadcasted_iota(jnp.int32, (RC, LW), 0) + r0
            cc = lax.broadcasted_iota(jnp.int32, (RC, LW), 1) + c0
            out.append((r >= cc) if rows_are_queries else (cc >= r))
    return out


def _pieces(ref2d, rows, rc, n, masked, rows_are_queries):
    kinds = _diag_slices(rc, n, rows_are_queries) if masked else ["all"] * (n // LW)
    out = []
    for c, kind in enumerate(kinds):
        if isinstance(kind, str):
            out.append(ref2d[rows, _lanes(c)] if kind == "all" else None)
        else:
            out.append(jnp.where(kind, ref2d[rows, _lanes(c)], NEG))
    return out


def _attn_fwd(P, Qa, Ka, V, *, tq, gather=()):
    S = P.shape[0]
    nq = S // tq
    ng = len(gather)

    def body(*refs):
        qa_ref, ka_ref, v_ref, cg_ref = refs[:4]
        o_ref, zc_ref, lse_ref = refs[4 + ng:7 + ng]
        s_s, p_s, m_s, l_s, acc_s, pm_s, al_s = refs[7 + 2 * ng:14 + 2 * ng]
        i, j = pl.program_id(0), pl.program_id(1)
        if ng:
            start, finish = _gather_copies(refs[4:4 + ng], refs[7 + ng:7 + 2 * ng], *refs[14 + 2 * ng:])
            pl.when((i == 0) & (j == 0))(start)

        @pl.when(j == 0)
        def _():
            m_s[...] = jnp.full_like(m_s, NEG)
            l_s[...] = jnp.zeros_like(l_s)
            acc_s[...] = jnp.zeros_like(acc_s)

        def step(masked):
            s_s[0] = _dot(qa_ref[0], ka_ref[0], NT_DIMS)
            for h in range(HC):
                b = h % 2
                if h + 1 < HC:
                    s_s[1 - b] = _dot(qa_ref[h + 1], ka_ref[h + 1], NT_DIMS)
                for rc in range(tq // RC):
                    rows = slice(rc * RC, (rc + 1) * RC)
                    pm = None
                    for sc in _pieces(s_s.at[b], rows, rc, tq, masked, True):
                        if sc is not None:
                            pm = sc if pm is None else jnp.maximum(pm, sc)
                    pm_s[rows, :] = pm
                m_prev = m_s[h]
                m_new = jnp.maximum(m_prev, jnp.max(pm_s[...], axis=1, keepdims=True))
                alpha = jnp.exp(m_prev - m_new)
                m_s[h] = m_new
                al_s[...] = alpha
                for rc in range(tq // RC):
                    rows = slice(rc * RC, (rc + 1) * RC)
                    mb = m_s[h, rows]
                    ps = None
                    for c, sc in enumerate(_pieces(s_s.at[b], rows, rc, tq, masked, True)):
                        if sc is None:
                            p_s[b, rows, _lanes(c)] = jnp.zeros((RC, LW), BF)
                            continue
                        p = jnp.exp(sc - mb)
                        ps = p if ps is None else ps + p
                        p_s[b, rows, _lanes(c)] = p.astype(BF)
                    l_s[h, rows] = al_s[rows] * l_s[h, rows] + ps
                acc_s[h] = al_s[:, 0:DH] * acc_s[h] + _dot(p_s[b], v_ref[h])

        @pl.when(j < i)
        def _():
            step(False)

        @pl.when(j == i)
        def _():
            step(True)
            for h in range(HC):
                l = jnp.sum(l_s[h], axis=1, keepdims=True)
                o_ref[:, DH * h:DH * (h + 1)] = acc_s[h] / l
                lse_ref[h] = m_s[h] + jnp.log(l)
            cg = cg_ref[...].astype(F32)
            zc_ref[...] = (o_ref[...] * cg * _sig(cg)).astype(BF)

        if ng:
            pl.when((i == nq - 1) & (j == nq - 1))(finish)

    stat = pltpu.VMEM((HC, tq, LW), F32)
    anyspec = pl.BlockSpec(memory_space=pl.ANY)
    res = pl.pallas_call(
        body, name="attn_fwd_gather" if ng else "attn_fwd", grid=(nq, nq),
        out_shape=(jax.ShapeDtypeStruct((S, 512), F32), jax.ShapeDtypeStruct((S, 512), BF), jax.ShapeDtypeStruct((HC, S, LW), F32))
        + _gathered_shapes(gather),
        in_specs=[pl.BlockSpec((HC, tq, 2 * DH), lambda i, j: (0, i, 0)),
                  pl.BlockSpec((HC, tq, 2 * DH), lambda i, j: (0, jnp.minimum(i, j), 0)),
                  pl.BlockSpec((HC, tq, DH), lambda i, j: (0, jnp.minimum(i, j), 0)),
                  pl.BlockSpec((tq, 512), lambda i, j: (i, C_AB // 512 + 7))] + [anyspec] * ng,
        out_specs=(pl.BlockSpec((tq, 512), lambda i, j: (i, 0)), pl.BlockSpec((tq, 512), lambda i, j: (i, 0)),
                   pl.BlockSpec((HC, tq, LW), lambda i, j: (0, i, 0))) + (anyspec,) * ng,
        scratch_shapes=[pltpu.VMEM((2, tq, tq), F32), pltpu.VMEM((2, tq, tq), BF), stat, stat, pltpu.VMEM((HC, tq, DH), F32),
                        pltpu.VMEM((tq, LW), F32), pltpu.VMEM((tq, LW), F32)] + (_comm_sems(ng) if ng else []),
        compiler_params=_cp(("arbitrary", "arbitrary") if ng else ("parallel", "arbitrary")),
    )(Qa, Ka, V, P, *gather)
    return res[0], res[1], res[2], list(res[3:])


def _attn_bwd_dkv(Qa, Ka, V, dO, lseT, dltT, dP, *, tq, scatter=()):
    S = Qa.shape[1]
    nq = S // tq
    ns = len(scatter)

    def body(*refs):
        qa_ref, ka_ref, v_ref, do_ref, lse_ref, dl_ref = refs[:6]
        dkv_o, dck_o = refs[7 + ns:9 + ns]
        s_s, dp_s, p_s, ds_s, dk_s, dv_s, dck_s = refs[9 + 2 * ns:16 + 2 * ns]
        j, i = pl.program_id(0), pl.program_id(1)
        if ns:
            start, finish = _scatter_copies(refs[7:7 + ns], refs[9 + ns:9 + 2 * ns], *refs[16 + 2 * ns:])
            pl.when((i == 0) & (j == 0))(start)

        @pl.when(i == 0)
        def _():
            dk_s[...] = jnp.zeros_like(dk_s)
            dv_s[...] = jnp.zeros_like(dv_s)
            dck_s[...] = jnp.zeros_like(dck_s)

        def mm(h, b):
            s_s[b] = _dot(ka_ref[h], qa_ref[h], NT_DIMS)
            dp_s[b] = _dot(v_ref[h], do_ref[h], NT_DIMS)

        def step(masked):
            mm(0, 0)
            for h in range(HC):
                b = h % 2
                if h + 1 < HC:
                    mm(h + 1, 1 - b)
                for rc in range(tq // RC):
                    rows = slice(rc * RC, (rc + 1) * RC)
                    acc = None
                    for c, sc in enumerate(_pieces(s_s.at[b], rows, rc, tq, masked, False)):
                        if sc is None:
                            p_s[rows, _lanes(c)] = jnp.zeros((RC, LW), BF)
                            ds_s[rows, _lanes(c)] = jnp.zeros((RC, LW), BF)
                            continue
                        p = jnp.exp(sc - lse_ref[h:h + 1, _lanes(c)])
                        ds = p * (dp_s[b, rows, _lanes(c)] - dl_ref[h:h + 1, _lanes(c)])
                        p_s[rows, _lanes(c)] = p.astype(BF)
                        ds_s[rows, _lanes(c)] = ds.astype(BF)
                        acc = ds if acc is None else acc + ds
                    dck_s[h, rows] += acc
                dv_s[h] += _dot(p_s[...], do_ref[h])
                dk_s[h] += _dot(ds_s[...], qa_ref[h])

        @pl.when(i > j)
        def _():
            step(False)

        @pl.when(i == j)
        def _():
            step(True)

        @pl.when(i == nq - 1)
        def _():
            for h in range(HC):
                dkv_o[:, DH * h:DH * (h + 1)] = dk_s[h][:, 0:DH].astype(BF)
                dkv_o[:, 512 + DH * h:512 + DH * (h + 1)] = dv_s[h].astype(BF)
            dck_o[...] = _lane_pack([jnp.sum(dck_s[h], axis=1, keepdims=True) for h in range(HC)], tq)

        if ns:
            pl.when((i == nq - 1) & (j == nq - 1))(finish)

    def qspec(w):
        return pl.BlockSpec((HC, tq, w), lambda j, i: (0, jnp.maximum(i, j), 0))

    def kspec(w):
        return pl.BlockSpec((HC, tq, w), lambda j, i: (0, j, 0))

    rowv = pl.BlockSpec((8, tq), lambda j, i: (0, jnp.maximum(i, j)))
    anyspec = pl.BlockSpec(memory_space=pl.ANY)
    res = pl.pallas_call(
        body, name="attn_bwd_dkv_scatter" if ns else "attn_bwd_dkv", grid=(nq, nq),
        out_shape=(jax.ShapeDtypeStruct(dP.shape, BF), jax.ShapeDtypeStruct((S, 128), F32)) + _same_shapes(scatter),
        in_specs=[qspec(2 * DH), kspec(2 * DH), kspec(DH), qspec(DH), rowv, rowv, anyspec] + [anyspec] * ns,
        out_specs=(pl.BlockSpec((tq, 1024), lambda j, i: (j, C_KV // 1024)), pl.BlockSpec((tq, 128), lambda j, i: (j, 0)))
        + (anyspec,) * ns,
        scratch_shapes=[pltpu.VMEM((2, tq, tq), F32), pltpu.VMEM((2, tq, tq), F32), pltpu.VMEM((tq, tq), BF), pltpu.VMEM((tq, tq), BF),
                        pltpu.VMEM((HC, tq, 2 * DH), F32), pltpu.VMEM((HC, tq, DH), F32), pltpu.VMEM((HC, tq, LW), F32)]
        + (_comm_sems(ns) if ns else []),
        input_output_aliases={6: 0},
        compiler_params=_cp(("arbitrary", "arbitrary") if ns else ("parallel", "arbitrary")),
    )(Qa, Ka, V, dO, lseT, dltT, dP, *scatter)
    return res[0], res[1], list(res[2:])


def _attn_bwd_dq(Qa, Ka, V, dO, lse, dlt, dP, *, tq):
    S = Qa.shape[1]
    nq = S // tq

    def body(qa_ref, ka_ref, v_ref, do_ref, lse_ref, dl_ref, dp_in, dq_o, dcq_o, s_s, dp_s, ds_s, dq_s, dcq_s):
        del dp_in
        i, j = pl.program_id(0), pl.program_id(1)

        @pl.when(j == 0)
        def _():
            dq_s[...] = jnp.zeros_like(dq_s)
            dcq_s[...] = jnp.zeros_like(dcq_s)

        def mm(h, b):
            s_s[b] = _dot(qa_ref[h], ka_ref[h], NT_DIMS)
            dp_s[b] = _dot(do_ref[h], v_ref[h], NT_DIMS)

        def step(masked):
            mm(0, 0)
            for h in range(HC):
                b = h % 2
                if h + 1 < HC:
                    mm(h + 1, 1 - b)
                for rc in range(tq // RC):
                    rows = slice(rc * RC, (rc + 1) * RC)
                    lb = lse_ref[h, rows]
                    db = dl_ref[h, rows]
                    acc = None
                    for c, sc in enumerate(_pieces(s_s.at[b], rows, rc, tq, masked, True)):
                        if sc is None:
                            ds_s[rows, _lanes(c)] = jnp.zeros((RC, LW), BF)
                            continue
                        ds = jnp.exp(sc - lb) * (dp_s[b, rows, _lanes(c)] - db)
                        ds_s[rows, _lanes(c)] = ds.astype(BF)
                        acc = ds if acc is None else acc + ds
                    dcq_s[h, rows] += acc
                dq_s[h] += _dot(ds_s[...], ka_ref[h])

        @pl.when(j < i)
        def _():
            step(False)

        @pl.when(j == i)
        def _():
            step(True)
            for h in range(HC):
                dq_o[:, DH * h:DH * (h + 1)] = (dq_s[h][:, 0:DH] * SCALE).astype(BF)
            dcq_o[...] = _lane_pack([jnp.sum(dcq_s[h], axis=1, keepdims=True) for h in range(HC)], tq)

    def qspec(w):
        return pl.BlockSpec((HC, tq, w), lambda i, j: (0, i, 0))

    def kspec(w):
        return pl.BlockSpec((HC, tq, w), lambda i, j: (0, jnp.minimum(i, j), 0))

    colv = pl.BlockSpec((tq, 128), lambda i, j: (i, 0))
    return pl.pallas_call(
        body, name="attn_bwd_dq", grid=(nq, nq),
        out_shape=(jax.ShapeDtypeStruct(dP.shape, BF), jax.ShapeDtypeStruct((S, 128), F32)),
        in_specs=[qspec(2 * DH), kspec(2 * DH), kspec(DH), qspec(DH), qspec(LW), qspec(LW), pl.BlockSpec(memory_space=pl.ANY)],
        out_specs=(pl.BlockSpec((tq, 512), lambda i, j: (i, C_Q // 512)), colv),
        scratch_shapes=[pltpu.VMEM((2, tq, tq), F32), pltpu.VMEM((2, tq, tq), F32), pltpu.VMEM((tq, tq), BF),
                        pltpu.VMEM((HC, tq, 2 * DH), F32), pltpu.VMEM((HC, tq, LW), F32)],
        input_output_aliases={6: 0},
        compiler_params=_cp(("parallel", "arbitrary")),
    )(Qa, Ka, V, dO, lse, dlt, dP)


def _xattn_probs(qm_ref, kv_ref, h):
    sl = slice(DH * h, DH * (h + 1))
    s = _dot(qm_ref[:, sl].astype(BF), kv_ref[:, sl].astype(BF), NT_DIMS) * SCALE
    p = jnp.exp(s - jnp.max(s, axis=1, keepdims=True))
    return p / jnp.sum(p, axis=1, keepdims=True)


def _xattn_fwd(P, kv, *, ts):
    S = P.shape[0]

    def body(qm_ref, kv_ref, zm_o, o_s):
        for h in range(HM):
            sl = slice(DH * h, DH * (h + 1))
            p = _xattn_probs(qm_ref, kv_ref, h)
            o_s[:, sl] = _dot(p.astype(BF), kv_ref[:, ML + DH * h:ML + DH * (h + 1)].astype(BF))
        mg = qm_ref[:, 256:512].astype(F32)
        zm_o[...] = (o_s[...] * mg * _sig(mg)).astype(BF)

    return pl.pallas_call(
        body, name="xattn_fwd", grid=(S // ts,),
        out_shape=jax.ShapeDtypeStruct((S, 256), BF),
        in_specs=[pl.BlockSpec((ts, 512), lambda i: (i, C_M // 512)), pl.BlockSpec((ML, 512), lambda i: (0, 0))],
        out_specs=pl.BlockSpec((ts, 256), lambda i: (i, 0)),
        scratch_shapes=[pltpu.VMEM((ts, 256), F32)],
        compiler_params=_cp(("parallel",)),
    )(P, kv)


def _xattn_bwd(P, kv, dzm, dP, *, ts):
    S = P.shape[0]

    def body(qm_ref, kv_ref, dz_ref, dp_in, dqm_o, dkv_o):
        del dp_in
        i = pl.program_id(0)

        @pl.when(i == 0)
        def _():
            dkv_o[...] = jnp.zeros_like(dkv_o)

        mg = qm_ref[:, 256:512].astype(F32)
        sg = _sig(mg)
        for h in range(HM):
            sl = slice(DH * h, DH * (h + 1))
            vsl = slice(ML + DH * h, ML + DH * (h + 1))
            p = _xattn_probs(qm_ref, kv_ref, h)
            pb = p.astype(BF)
            vh = kv_ref[:, vsl].astype(BF)
            o = _dot(pb, vh)
            dz = dz_ref[:, sl]
            do = dz * mg[:, sl] * sg[:, sl]
            dqm_o[:, 256 + DH * h:256 + DH * (h + 1)] = (dz * o * _dsilu(mg[:, sl], sg[:, sl])).astype(BF)
            dob = do.astype(BF)
            dpv = _dot(dob, vh, NT_DIMS)
            ds = p * (dpv - jnp.sum(do * o, axis=1, keepdims=True))
            dsb = ds.astype(BF)
            dqm_o[:, sl] = (_dot(dsb, kv_ref[:, sl].astype(BF)) * SCALE).astype(BF)
            dkv_o[:, sl] += _dot(dsb, qm_ref[:, sl].astype(BF), TN_DIMS) * SCALE
            dkv_o[:, vsl] += _dot(pb, dob, TN_DIMS)

    return pl.pallas_call(
        body, name="xattn_bwd", grid=(S // ts,),
        out_shape=(jax.ShapeDtypeStruct(dP.shape, BF), jax.ShapeDtypeStruct((ML, 512), F32)),
        in_specs=[pl.BlockSpec((ts, 512), lambda i: (i, C_M // 512)), pl.BlockSpec((ML, 512), lambda i: (0, 0)),
                  pl.BlockSpec((ts, 256), lambda i: (i, 0)), pl.BlockSpec(memory_space=pl.ANY)],
        out_specs=(pl.BlockSpec((ts, 512), lambda i: (i, C_M // 512)), pl.BlockSpec((ML, 512), lambda i: (0, 0))),
        input_output_aliases={3: 0},
        compiler_params=_cp(("arbitrary",)),
    )(P, kv, dzm, dP)


def _merge_fwd(za, zb, zc, zm, P, x, pa, pb, pc, pm, wo, lng, lnb, *, ts):
    S = x.shape[0]

    def body(za_r, zb_r, zc_r, zm_r, g_r, x_r, pa_r, pb_r, pc_r, pm_r, wo_r, lng_r, lnb_r,
             mg_o, ya_o, yb_o, yc_o, ym_o, xn_o, xh_o, rs_o):
        merged = jnp.zeros((ts, D), F32)
        for t, (z_r, p_r, y_o) in enumerate(((za_r, pa_r, ya_o), (zb_r, pb_r, yb_o), (zc_r, pc_r, yc_o), (zm_r, pm_r, ym_o))):
            y = _dot(z_r[...], p_r[...])
            merged = merged + _sig(g_r[:, D * t:D * (t + 1)].astype(F32)) * y
            y_o[...] = y.astype(BF)
        mb = merged.astype(BF)
        mg_o[...] = mb
        r = ALPHA * x_r[...] + _dot(mb, wo_r[...])
        xh, rstd = _ln_hat(r)
        xh_o[...] = xh
        rs_o[...] = rstd
        xn_o[...] = xh * lng_r[...] + lnb_r[...]

    def rows(w):
        return pl.BlockSpec((ts, w), lambda i: (i, 0))

    def full(a):
        return pl.BlockSpec(a.shape, lambda i: (0, 0))

    sd = lambda dt: jax.ShapeDtypeStruct((S, D), dt)
    return pl.pallas_call(
        body, name="merge_fwd", grid=(S // ts,),
        out_shape=(sd(BF), sd(BF), sd(BF), sd(BF), sd(BF), sd(F32), sd(F32), jax.ShapeDtypeStruct((S, 1), F32)),
        in_specs=[rows(512), rows(512), rows(512), rows(256), pl.BlockSpec((ts, 4 * D), lambda i: (i, 0)), rows(D),
                  full(pa), full(pb), full(pc), full(pm), full(wo), full(lng), full(lnb)],
        out_specs=(rows(D),) * 7 + (rows(1),),
        compiler_params=_cp(("parallel",)),
    )(za, zb, zc, zm, P, x, pa, pb, pc, pm, wo, lng, lnb)


def _loss_fwd(y, tgt, *, ts):
    S = y.shape[0]

    def body(y_r, t_r, dy_o, l_o):
        @pl.when(pl.program_id(0) == 0)
        def _():
            l_o[...] = jnp.zeros_like(l_o)

        e = y_r[...] - t_r[...]
        dy_o[...] = e / D
        l_o[...] += 0.5 * jnp.sum(_sum_r(e * e), axis=1, keepdims=True) / D

    rows = pl.BlockSpec((ts, D), lambda i: (i, 0))
    return pl.pallas_call(
        body, name="loss", grid=(S // ts,),
        out_shape=(jax.ShapeDtypeStruct((S, D), F32), jax.ShapeDtypeStruct((1, 1), F32)),
        in_specs=[rows, rows], out_specs=(rows, pl.BlockSpec((1, 1), lambda i: (0, 0))),
        compiler_params=_cp(("arbitrary",)),
    )(y, tgt)


def _out_bwd(dxn, xh, rstd, merged, wo, lng, *, ts):
    S = dxn.shape[0]

    def body(dxn_r, xh_r, rs_r, mg_r, wo_r, lng_r, dr_o, dm_o, dwo_o, dlng_o, dlnb_o):
        @pl.when(pl.program_id(0) == 0)
        def _():
            dwo_o[...] = jnp.zeros_like(dwo_o)
            dlng_o[...] = jnp.zeros_like(dlng_o)
            dlnb_o[...] = jnp.zeros_like(dlnb_o)

        dxn = dxn_r[...]
        xh = xh_r[...]
        dr = _ln_bwd(dxn * lng_r[...], xh, rs_r[...])
        dr_o[...] = dr
        drb = dr.astype(BF)
        dm_o[...] = _dot(drb, wo_r[...], NT_DIMS)
        dwo_o[...] += _dot(mg_r[...], drb, TN_DIMS)
        dlng_o[...] += _sum_r(dxn * xh)
        dlnb_o[...] += _sum_r(dxn)

    rows = pl.BlockSpec((ts, D), lambda i: (i, 0))
    full = lambda shape: pl.BlockSpec(shape, lambda i: (0, 0))
    sd = jax.ShapeDtypeStruct((S, D), F32)
    vec = jax.ShapeDtypeStruct((1, D), F32)
    return pl.pallas_call(
        body, name="out_bwd", grid=(S // ts,),
        out_shape=(sd, sd, jax.ShapeDtypeStruct((D, D), F32), vec, vec),
        in_specs=[rows, rows, pl.BlockSpec((ts, 1), lambda i: (i, 0)), rows, full((D, D)), full((1, D))],
        out_specs=(rows, rows, full((D, D)), full((1, D)), full((1, D))),
        compiler_params=_cp(("arbitrary",)),
    )(dxn, xh, rstd, merged, wo, lng)


def _merge_bwd(dm, P, ya, yb, yc, ym, za, zb, zc, zm, pa, pb, pc, pm, *, ts):
    S = dm.shape[0]

    def body(dm_r, g_r, ya_r, yb_r, yc_r, ym_r, za_r, zb_r, zc_r, zm_r, pa_r, pb_r, pc_r, pm_r,
             dg_o, dza_o, dzb_o, dzc_o, dzm_o, dpa_o, dpb_o, dpc_o, dpm_o):
        @pl.when(pl.program_id(0) == 0)
        def _():
            for o in (dpa_o, dpb_o, dpc_o, dpm_o):
                o[...] = jnp.zeros_like(o)

        dm = dm_r[...]
        for t, (y_r, z_r, p_r, dz_o, dp_o) in enumerate(((ya_r, za_r, pa_r, dza_o, dpa_o), (yb_r, zb_r, pb_r, dzb_o, dpb_o),
                                                        (yc_r, zc_r, pc_r, dzc_o, dpc_o), (ym_r, zm_r, pm_r, dzm_o, dpm_o))):
            gate = _sig(g_r[:, D * t:D * (t + 1)].astype(F32))
            dg_o[:, D * t:D * (t + 1)] = (dm * y_r[...].astype(F32) * gate * (1.0 - gate)).astype(BF)
            dyb = (dm * gate).astype(BF)
            dz_o[...] = _dot(dyb, p_r[...], NT_DIMS)
            dp_o[...] += _dot(z_r[...], dyb, TN_DIMS)

    def rows(w):
        return pl.BlockSpec((ts, w), lambda i: (i, 0))

    def full(a):
        return pl.BlockSpec(a.shape, lambda i: (0, 0))

    return pl.pallas_call(
        body, name="merge_bwd", grid=(S // ts,),
        out_shape=(jax.ShapeDtypeStruct((S, NP), BF),
                   jax.ShapeDtypeStruct((S, 512), F32), jax.ShapeDtypeStruct((S, 512), F32),
                   jax.ShapeDtypeStruct((S, 512), F32), jax.ShapeDtypeStruct((S, 256), F32),
                   jax.ShapeDtypeStruct(pa.shape, F32), jax.ShapeDtypeStruct(pb.shape, F32),
                   jax.ShapeDtypeStruct(pc.shape, F32), jax.ShapeDtypeStruct(pm.shape, F32)),
        in_specs=[rows(D), pl.BlockSpec((ts, 4 * D), lambda i: (i, 0)), rows(D), rows(D), rows(D), rows(D),
                  rows(512), rows(512), rows(512), rows(256), full(pa), full(pb), full(pc), full(pm)],
        out_specs=(pl.BlockSpec((ts, 4 * D), lambda i: (i, 0)), rows(512), rows(512), rows(512), rows(256),
                   full(pa), full(pb), full(pc), full(pm)),
        compiler_params=_cp(("arbitrary",)),
    )(dm, P, ya, yb, yc, ym, za, zb, zc, zm, pa, pb, pc, pm)


def _branch_bwd(P, ca, cb, u, vb, dza, dzb, dzc, oc, wA, gA, betaA, wB, dP, *, ts):
    S = P.shape[0]
    nt = S // ts

    def rev(i):
        return nt - 1 - i

    def rows(w):
        return pl.BlockSpec((ts, w), lambda i: (rev(i), 0))

    def halo(rows_):
        return pl.BlockSpec((rows_, 512), lambda i: (jnp.maximum(rev(i) * (ts // rows_) - 1, 0), 0))

    def full(shape):
        return pl.BlockSpec(shape, lambda i: (0, 0))

    def body(pg, ca_r, cb_r, u_r, vb_r, uh_r, vh_r, dza_r, dzb_r, dzc_r, oc_r, wA_r, gA_r, betaA_r, wB_r, dp_in,
             dpg_o, do_o, dl_o, dwA_o, dbA_o, dgA_o, dbetaA_o, dwB_o, dwinA, uwin, haloA, dwinB, vwin, haloB, cv_s, dwpA, dwpB):
        del dp_in
        i = pl.program_id(0)
        nz = (rev(i) > 0).astype(F32)

        @pl.when(i == 0)
        def _():
            for o in (dwA_o, dbA_o, dgA_o, dbetaA_o, dwB_o, haloA, haloB, dwpA, dwpB):
                o[...] = jnp.zeros_like(o)

        def col(j):
            return pg[:, 512 * j:512 * (j + 1)].astype(F32)

        def put(j, val):
            dpg_o[:, 512 * j:512 * (j + 1)] = val.astype(BF)

        a_gate = col(2)
        xh, rstd = _ln_hat(ca_r[...])
        gA_v = gA_r[...]
        n = xh * gA_v + betaA_r[...]
        sn = _sig(n)
        a = n * sn
        sg = _sig(a_gate)
        dza = dza_r[...]
        put(2, dza * a * _dsilu(a_gate, sg))
        dn = dza * a_gate * sg * _dsilu(n, sn)
        dgA_o[...] += _sum_r(dn * xh)
        dbetaA_o[...] += _sum_r(dn)
        dca = _ln_bwd(dn * gA_v, xh, rstd)
        dbA_o[...] += _sum_r(dca)
        dwinA[0:ts, :] = dca
        dwinA[ts:, :] = haloA[...]
        haloA[...] = dca[0:HALO_A, :]
        uwin[0:HALO_A, :] = uh_r[...] * nz
        uwin[HALO_A:, :] = u_r[...]
        _dwcorr_acc(dwpA, dwinA.at[pl.ds(0, ts)], uwin, KA, HALO_A - KA + 1, ts)
        _dwconv(cv_s, dwinA, wA_r, KA, 0, ts, reverse=True)
        du = cv_s[...]
        sv = _sig(col(1))
        put(0, du * sv)
        put(1, du * col(0) * sv * (1.0 - sv))

        b_gate = col(6)
        sgb = _sig(b_gate)
        cbv = cb_r[...]
        b_b = col(4)
        dzb = dzb_r[...]
        put(6, dzb * b_b * cbv * _dsilu(b_gate, sgb))
        dhb = dzb * b_gate * sgb
        put(4, dhb * cbv)
        dcb = dhb * b_b
        dwinB[0:ts, :] = dcb
        dwinB[ts:, :] = haloB[...]
        haloB[...] = dcb[0:HALO_B, :]
        vwin[0:HALO_B, :] = vh_r[...] * nz
        vwin[HALO_B:, :] = vb_r[...]
        _dwcorr_acc(dwpB, dwinB.at[pl.ds(0, ts)], vwin, KB, HALO_B - KB + 1, ts)
        _dwconv(cv_s, dwinB, wB_r, KB, 0, ts, reverse=True)
        dv = cv_s[...]
        put(5, dv * col(3))
        put(3, dv * col(5))

        c_gate = col(7)
        sgc = _sig(c_gate)
        dzc = dzc_r[...]
        ocv = oc_r[...]
        put(7, dzc * ocv * _dsilu(c_gate, sgc))
        do = dzc * c_gate * sgc
        for h in range(HC):
            do_o[h] = do[:, DH * h:DH * (h + 1)].astype(BF)
        dd = do * ocv
        for h in range(HC):
            dl_o[h] = jnp.broadcast_to(jnp.sum(dd[:, DH * h:DH * (h + 1)], axis=1, keepdims=True), (ts, LW))

        @pl.when(i == nt - 1)
        def _():
            for k in range(KA):
                dwA_o[k:k + 1, :] = _sum_r(dwpA[8 * k:8 * k + 8, :])
            for k in range(KB):
                dwB_o[k:k + 1, :] = _sum_r(dwpB[8 * k:8 * k + 8, :])

    v512 = jax.ShapeDtypeStruct((1, 512), F32)
    return pl.pallas_call(
        body, name="branch_bwd", grid=(nt,),
        out_shape=(jax.ShapeDtypeStruct(dP.shape, BF), jax.ShapeDtypeStruct((HC, S, DH), BF), jax.ShapeDtypeStruct((HC, S, LW), F32),
                   jax.ShapeDtypeStruct((32, 512), F32), v512, v512, v512, jax.ShapeDtypeStruct((8, 512), F32)),
        in_specs=[pl.BlockSpec((ts, 4096), lambda i: (rev(i), C_AB // 4096)),
                  rows(512), rows(512), rows(512), rows(512), halo(HALO_A), halo(HALO_B),
                  rows(512), rows(512), rows(512), rows(512),
                  full((32, 512)), full((1, 512)), full((1, 512)), full((8, 512)), pl.BlockSpec(memory_space=pl.ANY)],
        out_specs=(pl.BlockSpec((ts, 4096), lambda i: (rev(i), C_AB // 4096)),
                   pl.BlockSpec((HC, ts, DH), lambda i: (0, rev(i), 0)), pl.BlockSpec((HC, ts, LW), lambda i: (0, rev(i), 0)),
                   full((32, 512)), full((1, 512)), full((1, 512)), full((1, 512)), full((8, 512))),
        scratch_shapes=[pltpu.VMEM((ts + HALO_A, 512), F32), pltpu.VMEM((ts + HALO_A, 512), F32), pltpu.VMEM((HALO_A, 512), F32),
                        pltpu.VMEM((ts + HALO_B, 512), F32), pltpu.VMEM((ts + HALO_B, 512), F32), pltpu.VMEM((HALO_B, 512), F32),
                        pltpu.VMEM((ts, 512), F32), pltpu.VMEM((8 * 32, 512), F32), pltpu.VMEM((8 * 8, 512), F32)],
        input_output_aliases={15: 0},
        compiler_params=_cp(("arbitrary",)),
    )(P, ca, cb, u, vb, u, vb, dza, dzb, dzc, oc, wA, gA, betaA, wB, dP)


def _cum_bwd(F, dcum, bfg, dP, *, ts):
    S = F.shape[0]
    nt = S // ts

    def body(f_ref, dc_ref, bf_r, dp_in, df_o, dbf_o, carry):
        del dp_in
        i = pl.program_id(0)

        @pl.when(i == 0)
        def _():
            carry[...] = jnp.zeros_like(carry)
            dbf_o[...] = jnp.zeros_like(dbf_o)

        r = lax.broadcasted_iota(jnp.int32, (ts, ts), 0)
        c = lax.broadcasted_iota(jnp.int32, (ts, ts), 1)
        tri = (r <= c).astype(F32)
        dlogf = jnp.dot(tri, dc_ref[...], precision=HIGHEST, preferred_element_type=F32) + carry[...]
        carry[...] = dlogf[0:1, :]
        x = f_ref[...] + bf_r[...]
        lane = lax.broadcasted_iota(jnp.int32, (ts, 128), 1)
        df = jnp.where(lane < HC, dlogf * _sig(-x), 0.0)
        df_o[...] = df.astype(BF)
        dbf_o[...] += _sum_r(df)

    rows = pl.BlockSpec((ts, 128), lambda i: (nt - 1 - i, 0))
    return pl.pallas_call(
        body, name="cum_bwd", grid=(nt,),
        out_shape=(jax.ShapeDtypeStruct(dP.shape, BF), jax.ShapeDtypeStruct((1, 128), F32)),
        in_specs=[rows, rows, pl.BlockSpec((1, 128), lambda i: (0, 0)), pl.BlockSpec(memory_space=pl.ANY)],
        out_specs=(pl.BlockSpec((ts, 128), lambda i: (nt - 1 - i, C_F // 128)), pl.BlockSpec((1, 128), lambda i: (0, 0))),
        scratch_shapes=[pltpu.VMEM((1, 128), F32)],
        input_output_aliases={3: 0},
        compiler_params=_cp(("arbitrary",)),
    )(F, dcum, bfg, dP)


def _adamw(w, m, v, gparts, *, name, tr):
    rws, cols = w.shape
    tr = min(tr, rws)
    assert rws % tr == 0 and gparts.shape == (NDEV, rws, cols), (name, w.shape, gparts.shape)
    c1 = 1.0 - ADAM_B1 ** ADAM_STEP
    c2 = 1.0 - ADAM_B2 ** ADAM_STEP

    def body(w_r, m_r, v_r, g_r, g_o, d_o, m_o, v_o):
        g = g_r[0].astype(F32)
        for p in range(1, NDEV):
            g = g + g_r[p].astype(F32)
        mn = ADAM_B1 * m_r[...] + (1.0 - ADAM_B1) * g
        vn = ADAM_B2 * v_r[...] + (1.0 - ADAM_B2) * (g * g)
        g_o[...] = g
        m_o[...] = mn
        v_o[...] = vn
        d_o[...] = -ADAM_LR * ((mn / c1) / (jnp.sqrt(vn / c2) + ADAM_EPS) + ADAM_WD * w_r[...])

    blk = pl.BlockSpec((tr, cols), lambda i: (i, 0))
    shp = jax.ShapeDtypeStruct((rws, cols), F32)
    return pl.pallas_call(
        body, name=name, grid=(rws // tr,), out_shape=(shp,) * 4,
        in_specs=[blk, blk, blk, pl.BlockSpec((NDEV, tr, cols), lambda i: (0, i, 0))],
        out_specs=(blk,) * 4, compiler_params=_cp(("parallel",)),
    )(w, m, v, gparts)


def _slot(p):
    return 4 * p[0] + 2 * p[1] + p[2]


def _comm_sems(na):
    return [pltpu.SemaphoreType.DMA((na, 7)), pltpu.SemaphoreType.DMA((na, 7)), pltpu.SemaphoreType.DMA((na,))]


def _gather_copies(ins, outs, send_sems, recv_sems, local_sems):
    na = len(ins)
    x, y, c = lax.axis_index("x"), lax.axis_index("y"), lax.axis_index("c")
    me, sib = (x, y, c), (x, y, 1 - c)
    chips = [(1 - x, y), (x, 1 - y), (1 - x, 1 - y)]

    def cp(a, k, block, to, src=None):
        dst = outs[a].at[_slot(block)]
        return pltpu.make_async_remote_copy(src_ref=dst if src is None else src, dst_ref=dst,
                                            send_sem=send_sems.at[a, k], recv_sem=recv_sems.at[a, k],
                                            device_id=to, device_id_type=pl.DeviceIdType.MESH)

    def mine(a):
        return pltpu.make_async_copy(ins[a], outs[a].at[_slot(me)], local_sems.at[a])

    def first(a):
        return [cp(a, 0, me, sib, src=ins[a])] + [cp(a, 1 + j, me, (*chip, c), src=ins[a]) for j, chip in enumerate(chips)]

    def start():
        for a in range(na):
            mine(a).start()
            for f in first(a):
                f.start()

    def finish():
        for j, chip in enumerate(chips):
            for a in range(na):
                cp(a, 1 + j, (*chip, c), me).wait_recv()
                cp(a, 4 + j, (*chip, c), sib).start()
        for a in range(na):
            cp(a, 0, sib, me).wait_recv()
            for j, chip in enumerate(chips):
                cp(a, 4 + j, (*chip, 1 - c), me).wait_recv()
        for a in range(na):
            for f in first(a):
                f.wait_send()
            for j, chip in enumerate(chips):
                cp(a, 4 + j, (*chip, c), sib).wait_send()
            mine(a).wait()

    return start, finish


def _scatter_copies(ins, outs, send_sems, recv_sems, local_sems):
    na = len(ins)
    x, y, c = lax.axis_index("x"), lax.axis_index("y"), lax.axis_index("c")
    me = (x, y, c)
    peers = [(x ^ ((k >> 2) & 1), y ^ ((k >> 1) & 1), c ^ (k & 1)) for k in range(1, NDEV)]

    def cp(a, k, peer):
        return pltpu.make_async_remote_copy(src_ref=ins[a].at[_slot(peer)], dst_ref=outs[a].at[_slot(me)],
                                            send_sem=send_sems.at[a, k], recv_sem=recv_sems.at[a, k],
                                            device_id=peer, device_id_type=pl.DeviceIdType.MESH)

    def landed(a, k, peer):
        dst = outs[a].at[_slot(peer)]
        return pltpu.make_async_remote_copy(src_ref=dst, dst_ref=dst, send_sem=send_sems.at[a, k], recv_sem=recv_sems.at[a, k],
                                            device_id=peer, device_id_type=pl.DeviceIdType.MESH)

    def mine(a):
        return pltpu.make_async_copy(ins[a].at[_slot(me)], outs[a].at[_slot(me)], local_sems.at[a])

    def start():
        for a in range(na):
            mine(a).start()
            for k, peer in enumerate(peers):
                cp(a, k, peer).start()

    def finish():
        for a in range(na):
            for k, peer in enumerate(peers):
                landed(a, k, peer).wait_recv()
        for a in range(na):
            for k, peer in enumerate(peers):
                cp(a, k, peer).wait_send()
            mine(a).wait()

    return start, finish


def _comm_call(arrs, copies, out_shapes, *, name):
    na = len(arrs)

    def body(*refs):
        start, finish = copies(refs[:na], refs[na:2 * na], *refs[2 * na:])
        start()
        finish()

    anyspec = pl.BlockSpec(memory_space=pl.ANY)
    return pl.pallas_call(body, name=name, out_shape=out_shapes, in_specs=[anyspec] * na, out_specs=(anyspec,) * na,
                          scratch_shapes=_comm_sems(na))(*arrs)


def _gathered_shapes(arrs):
    return tuple(jax.ShapeDtypeStruct((NDEV,) + a.shape, a.dtype) for a in arrs)


def _same_shapes(arrs):
    return tuple(jax.ShapeDtypeStruct(a.shape, a.dtype) for a in arrs)


def _all_gather(arrs, *, name):
    return _comm_call(arrs, _gather_copies, _gathered_shapes(arrs), name=name)


def _gathered_to_layout(g4):
    parts = []
    for a, b in _RUNS:
        for d in range(a // SHARD_IN, (b - 1) // SHARD_IN + 1):
            lo, hi = max(a, d * SHARD_IN), min(b, (d + 1) * SHARD_IN)
            parts.append(g4[d, ..., lo - d * SHARD_IN:hi - d * SHARD_IN])
    parts.append(jnp.zeros(g4.shape[1:-1] + (NP - IN_COLS,), g4.dtype))
    return jnp.concatenate(parts, axis=-1)


def _layout_to_shards(w):
    offs, off = {}, 0
    for a, b in _RUNS:
        offs[a] = (b, off)
        off += b - a
    shards = []
    for d in range(NDEV):
        parts = []
        for a in sorted(offs):
            b, off = offs[a]
            lo, hi = max(a, d * SHARD_IN), min(b, (d + 1) * SHARD_IN)
            if lo < hi:
                parts.append(w[..., off + lo - a:off + hi - a])
        shards.append(jnp.concatenate(parts, axis=-1))
    return jnp.stack(shards)


def _tiles(S):
    ts = min(256, S)
    tsb = min(128, S)
    tq = min(512, S)
    return ts, tsb, tq


def _layer_fwd(x, mem_n, w, gather=()):
    ts, _, tq = _tiles(x.shape[0])
    P = _mm(x, w["W"], name="proj_fwd", out_dtype=BF, tm=1024, tn=1152, tk=D)
    F = _mm(x, w["Wf"], name="proj_f_fwd", tm=1024, tn=NP - C_F, tk=D)
    kv = _mm(mem_n, w["w_kv_mem"], name="kv_fwd", tm=ML, tn=512, tk=D)
    za, zb, u, ca, vb, cb, cum = _pre_fwd(P, F, w["wA"], w["conv_a_b"], w["ln_a_g"], w["ln_a_b"], w["wB"], w["b_forget"], ts=ts)
    Qa, Ka, V = _attn_prep(P, cum, ts=ts)
    oc, zc, lse, gathered = _attn_fwd(P, Qa, Ka, V, tq=tq, gather=gather)
    zm = _xattn_fwd(P, kv, ts=tq)
    merged, ya, yb, yc, ym, xn, xh, rstd = _merge_fwd(za, zb, zc, zm, P, x, w["p_a"], w["p_b"], w["p_c"], w["p_m"], w["w_out"],
                                                      w["ln_g"], w["ln_b"], ts=ts)
    saved = (x, P, F, kv, za, zb, zc, zm, u, ca, vb, cb, Qa, Ka, V, oc, lse, merged, ya, yb, yc, ym, xh, rstd)
    return xn, saved, gathered


def _layer_bwd(dx, saved, mem_n, w, scatter=(), own_chunks=None):
    (xl, P, F, kv, za, zb, zc, zm, u, ca, vb, cb, Qa, Ka, V, oc, lse, merged, ya, yb, yc, ym, xh, rstd) = saved
    ts, tsb, tq = _tiles(xl.shape[0])
    g = {}
    dr, dm, g["w_out"], g["ln_g"], g["ln_b"] = _out_bwd(dx, xh, rstd, merged, w["w_out"], w["ln_g"], ts=ts)
    dP, dza, dzb, dzc, dzm, g["p_a"], g["p_b"], g["p_c"], g["p_m"] = _merge_bwd(
        dm, P, ya, yb, yc, ym, za, zb, zc, zm, w["p_a"], w["p_b"], w["p_c"], w["p_m"], ts=tsb)
    dP, do, dlt, dwA, g["conv_a_b"], g["ln_a_g"], g["ln_a_b"], dwB = _branch_bwd(
        P, ca, cb, u, vb, dza, dzb, dzc, oc, w["wA"], w["ln_a_g"], w["ln_a_b"], w["wB"], dP, ts=ts)
    g["conv_a_w"], g["conv_b_w"] = dwA[:KA], dwB[:KB]
    dP, dck, received = _attn_bwd_dkv(Qa, Ka, V, do, lse[:, :, 0], dlt[:, :, 0], dP, tq=tq, scatter=scatter)
    dP, dcq = _attn_bwd_dq(Qa, Ka, V, do, lse, dlt, dP, tq=tq)
    dP, dbf = _cum_bwd(F, dcq - dck, w["b_forget"], dP, ts=ts)
    g["b_forget"] = dbf[0, :HC]
    dP, dkv = _xattn_bwd(P, kv, dzm, dP, ts=tq)
    g["w_kv_mem"] = _mm(mem_n.T, dkv, name="wkv_bwd", tm=D, tn=512, tk=ML)
    dmem_n = _mm(dkv, w["w_kv_mem"], name="memn_bwd", nt=True, tm=ML, tn=D, tk=512)
    g["w_in"] = _mm(xl.T.astype(BF), dP, name="win_bwd", out_dtype=BF, tm=D, tn=1152, tk=1024)
    if own_chunks is None:
        dx = _mm(dP, w["W"], name="x_bwd", nt=True, tm=1024, tn=D, tk=1152, add=dr, add_scale=ALPHA)
        return dx, g, dmem_n, received, None
    dx, received_own = _mm(dP, w["W"], name="x_bwd_scatter", nt=True, tm=1024, tn=D, tk=1152, add=dr, add_scale=ALPHA,
                           scatter=own_chunks(g))
    return dx, g, dmem_n, received, received_own


_SMALL = (("b_forget", (NL, HC)), ("conv_a_b", (NL, 512)), ("ln_a_g", (NL, 512)), ("ln_a_b", (NL, 512)),
          ("mem_ln_g", (D,)), ("mem_ln_b", (D,)), ("ln_g", (NL, D)), ("ln_b", (NL, D)),
          ("conv_a_w", (NL, KA, 512)), ("conv_b_w", (NL, KB, 512)))


def _pack(parts, rows_mult=8):
    flat = jnp.concatenate([p.reshape(-1).astype(F32) for p in parts])
    n = flat.shape[0]
    rows = -(-n // 128)
    rows = -(-rows // rows_mult) * rows_mult
    return jnp.pad(flat, (0, rows * 128 - n)).reshape(rows, 128)


def _unpack(buf, shapes):
    flat = buf.reshape(-1)
    out, off = [], 0
    for shp in shapes:
        n = 1
        for d in shp:
            n *= d
        out.append(flat[off:off + n].reshape(shp))
        off += n
    return out


def kernel(x, mem, w_in, b_forget, conv_a_w, conv_a_b, ln_a_g, ln_a_b, conv_b_w, w_kv_mem, mem_ln_g, mem_ln_b, p_a, p_b, p_c, p_m, w_out, ln_g, ln_b, loss_target, m_w_in, m_b_forget, m_conv_a_w, m_conv_a_b, m_ln_a_g, m_ln_a_b, m_conv_b_w, m_w_kv_mem, m_mem_ln_g, m_mem_ln_b, m_p_a, m_p_b, m_p_c, m_p_m, m_w_out, m_ln_g, m_ln_b, v_w_in, v_b_forget, v_conv_a_w, v_conv_a_b, v_ln_a_g, v_ln_a_b, v_conv_b_w, v_w_kv_mem, v_mem_ln_g, v_mem_ln_b, v_p_a, v_p_b, v_p_c, v_p_m, v_w_out, v_ln_g, v_ln_b):
    wts = dict(w_in=w_in, b_forget=b_forget, conv_a_w=conv_a_w, conv_a_b=conv_a_b, ln_a_g=ln_a_g, ln_a_b=ln_a_b, conv_b_w=conv_b_w,
               w_kv_mem=w_kv_mem, mem_ln_g=mem_ln_g, mem_ln_b=mem_ln_b, p_a=p_a, p_b=p_b, p_c=p_c, p_m=p_m, w_out=w_out, ln_g=ln_g, ln_b=ln_b)
    mom = dict(w_in=m_w_in, b_forget=m_b_forget, conv_a_w=m_conv_a_w, conv_a_b=m_conv_a_b, ln_a_g=m_ln_a_g, ln_a_b=m_ln_a_b,
               conv_b_w=m_conv_b_w, w_kv_mem=m_w_kv_mem, mem_ln_g=m_mem_ln_g, mem_ln_b=m_mem_ln_b, p_a=m_p_a, p_b=m_p_b, p_c=m_p_c,
               p_m=m_p_m, w_out=m_w_out, ln_g=m_ln_g, ln_b=m_ln_b)
    vel = dict(w_in=v_w_in, b_forget=v_b_forget, conv_a_w=v_conv_a_w, conv_a_b=v_conv_a_b, ln_a_g=v_ln_a_g, ln_a_b=v_ln_a_b,
               conv_b_w=v_conv_b_w, w_kv_mem=v_w_kv_mem, mem_ln_g=v_mem_ln_g, mem_ln_b=v_mem_ln_b, p_a=v_p_a, p_b=v_p_b, p_c=v_p_c,
               p_m=v_p_m, w_out=v_w_out, ln_g=v_ln_g, ln_b=v_ln_b)
    names = ("w_in", "b_forget", "conv_a_w", "conv_a_b", "ln_a_g", "ln_a_b", "conv_b_w", "w_kv_mem", "mem_ln_g", "mem_ln_b",
             "p_a", "p_b", "p_c", "p_m", "w_out", "ln_g", "ln_b")
    mid = ("p_a", "p_b", "p_c", "p_m", "w_out", "w_kv_mem")
    me = 4 * lax.axis_index("x") + 2 * lax.axis_index("y") + lax.axis_index("c")

    row_sharded = ("w_out", "w_kv_mem")
    mid_shapes = [wts[n].shape[1:] for n in mid]
    mid_nrows = [s[0] * s[1] // 128 for s in mid_shapes]

    def mid_pack(d, l):
        return jnp.concatenate([d[n][l].reshape(-1, 128) for n in mid], axis=0)

    def layer_weights(l, g_win, g16, wA, wB):
        w = {"W": _gathered_to_layout(g_win), "wA": wA[l], "wB": wB[l]}
        w["Wf"] = w["W"][:, C_F:]
        off = 0
        for n, shp, nr in zip(mid, mid_shapes, mid_nrows):
            blk = g16[:, off:off + nr].reshape((NDEV,) + shp)
            off += nr
            w[n] = blk.reshape(NDEV * shp[0], shp[1]) if n in row_sharded else blk.transpose(1, 0, 2).reshape(shp[0], NDEV * shp[1])
        w["b_forget"] = jnp.pad(b_forget[l], (0, 128 - HC)).reshape(1, 128)
        for n, a in (("conv_a_b", conv_a_b), ("ln_a_g", ln_a_g), ("ln_a_b", ln_a_b), ("ln_g", ln_g), ("ln_b", ln_b)):
            w[n] = a[l].reshape(1, -1)
        return w

    def grad_chunks(g):
        parts = []
        for n, shp in zip(mid, mid_shapes):
            a = g[n]
            a = a.reshape(NDEV, shp[0], shp[1]) if n in row_sharded else a.reshape(shp[0], NDEV, shp[1]).transpose(1, 0, 2)
            parts.append(a.reshape(NDEV, -1, 128))
        return [_layout_to_shards(g["w_in"]), jnp.concatenate(parts, axis=1).astype(BF)]

    shards = [[w_in[l].astype(BF), mid_pack(wts, l).astype(BF)] for l in range(NL)]
    pk32 = jnp.concatenate([conv_a_w, conv_b_w], axis=1).reshape(NL * (KA + KB), 512 // NDEV)
    g_win, g16, g32 = _all_gather(shards[0] + [pk32], name="gather_first")
    conv = g32.reshape(NDEV, NL, KA + KB, 512 // NDEV).transpose(1, 2, 0, 3).reshape(NL, KA + KB, 512)
    wA = jnp.pad(conv[:, :KA], ((0, 0), (0, 32 - KA), (0, 0)))
    wB = jnp.pad(conv[:, KA:], ((0, 0), (0, 8 - KB), (0, 0)))

    mem_n, mem_hat = _mem_ln_fwd(mem[0], mem_ln_g.reshape(1, D), mem_ln_b.reshape(1, D))
    xl, lw, saved = x[0], [], []
    for l in range(NL):
        lw.append(layer_weights(l, g_win, g16, wA, wB))
        xl, sv, got = _layer_fwd(xl, mem_n, lw[l], gather=shards[l + 1] if l + 1 < NL else ())
        saved.append(sv)
        if got:
            g_win, g16 = got
    dx, loss = _loss_fwd(xl, loss_target[0], ts=_tiles(xl.shape[0])[0])
    loss = lax.psum(loss[0, 0], ("x", "y", "c"))

    g = [None] * NL
    dmem_n = [None] * NL
    recv = [None] * NL
    pending = ()
    for l in reversed(range(NL)):
        dx, g[l], dmem_n[l], got, recv[l] = _layer_bwd(dx, saved[l], mem_n, lw[l], scatter=pending,
                                                       own_chunks=grad_chunks if l == 0 else None)
        if got:
            recv[l + 1] = got
        pending = grad_chunks(g[l]) if l else ()
    gs = {n: jnp.stack([g[l][n].reshape(shp[1:]) for l in range(NL)]) for n, shp in _SMALL if len(shp) > 1}
    gs["mem_ln_g"], gs["mem_ln_b"] = _mem_ln_bwd(dmem_n, mem_hat)
    (r_small,) = _all_gather([_pack([gs[n] for n, _ in _SMALL])], name="gather_small")

    res = {}
    r_win = jnp.concatenate([recv[l][0] for l in range(NL)], axis=1)
    res["w_in"] = [a.reshape(NL, D, SHARD_IN) for a in
                   _adamw(w_in.reshape(NL * D, SHARD_IN), m_w_in.reshape(NL * D, SHARD_IN), v_w_in.reshape(NL * D, SHARD_IN),
                          r_win, name="adamw_w_in", tr=128)]
    pk = lambda d: jnp.concatenate([mid_pack(d, l) for l in range(NL)], axis=0)
    o16 = _adamw(pk(wts), pk(mom), pk(vel), jnp.concatenate([recv[l][1] for l in range(NL)], axis=1), name="adamw_mid", tr=1024)
    for idx, (n, shp, nr) in enumerate(zip(mid, mid_shapes, mid_nrows)):
        off = [l * sum(mid_nrows) + sum(mid_nrows[:idx]) for l in range(NL)]
        res[n] = [jnp.stack([o[f:f + nr].reshape(shp) for f in off]) for o in o16]

    def small_view(d, n):
        a = d[n]
        if n in ("conv_a_w", "conv_b_w"):
            fullw = jnp.zeros(a.shape[:2] + (512,), F32)
            return lax.dynamic_update_slice(fullw, a, (0, 0, me * (512 // NDEV)))
        return a

    spk = lambda d: _pack([small_view(d, n) for n, _ in _SMALL])
    osm = _adamw(spk(wts), spk(mom), spk(vel), r_small, name="adamw_small", tr=1024)
    osm = [_unpack(o, [s for _, s in _SMALL]) for o in osm]
    for idx, (n, _) in enumerate(_SMALL):
        vals = [o[idx] for o in osm]
        if n in ("conv_a_w", "conv_b_w"):
            vals = [lax.dynamic_slice(a, (0, 0, me * (512 // NDEV)), a.shape[:2] + (512 // NDEV,)) for a in vals]
        res[n] = vals

    outs = [loss, dx[None]]
    for k in range(4):
        outs += [res[n][k] for n in names]
    return tuple(outs)
```

```python
import jax
import jax.numpy as jnp
from jax import lax
from jax.experimental import pallas as pl
from jax.experimental.pallas import tpu as pltpu

F32 = jnp.float32
BF = jnp.bfloat16
HIGHEST = lax.Precision.HIGHEST

D = 1024
NL = 4
NDEV = 8
HC, DH = 8, 64
HM = 4
ML = 256
KA, KB = 31, 3
HALO_A, HALO_B = 32, 8
ALPHA = (2.0 * NL) ** 0.25
EPS = 1e-5
SCALE = DH ** -0.5
NEG = -1e30

ADAM_LR, ADAM_B1, ADAM_B2, ADAM_EPS, ADAM_WD, ADAM_STEP = 0.001, 0.9, 0.999, 1e-08, 0.01, 10

C_G = 0
C_AB = 4096
C_Q = 8192
C_M = 8704
C_KV = 9216
C_F = 10240
NP = 10368
_RUNS = ((6152, 10248), (0, 3584), (5128, 5640), (3584, 4096), (5640, 6152), (4096, 5120), (5120, 5128))
IN_COLS = 10248
SHARD_IN = IN_COLS // NDEV

VMEM_LIMIT = 56 * 1024 * 1024

NT_DIMS = (((1,), (1,)), ((), ()))
TN_DIMS = (((0,), (0,)), ((), ()))


def _cp(sem=None):
    return pltpu.CompilerParams(dimension_semantics=sem, vmem_limit_bytes=VMEM_LIMIT)


def _sig(x):
    return 1.0 / (1.0 + jnp.exp(-x))


def _dsilu(x, s):
    return s * (1.0 + x * (1.0 - s))


def _mean_l(x):
    return jnp.mean(x, axis=-1, keepdims=True)


def _sum_r(x):
    return jnp.sum(x, axis=0, keepdims=True)


def _ln_hat(x):
    mu = _mean_l(x)
    xc = x - mu
    rstd = lax.rsqrt(_mean_l(xc * xc) + EPS)
    return xc * rstd, rstd


def _ln_bwd(dxh, xh, rstd):
    return rstd * (dxh - _mean_l(dxh) - xh * _mean_l(dxh * xh))


def _dot(a, b, dims=None):
    if dims is None:
        return jnp.dot(a, b, preferred_element_type=F32)
    return lax.dot_general(a, b, dims, preferred_element_type=F32)


def _taps_by_phase(n_taps):
    return [(b, list(range(b, n_taps, 8))) for b in range(min(8, n_taps))]


CONV_ROWS = 64


def _conv_chunks(ts, width):
    return [(slice(r, r + CONV_ROWS), r, slice(c, c + 128)) for c in range(0, width, 128) for r in range(0, ts, CONV_ROWS)]


def _dwconv(out_ref, win_ref, w_ref, n_taps, first_row, ts, reverse=False, bias_ref=None):
    for rows, r0, cl in _conv_chunks(ts, out_ref.shape[1]):
        acc = None
        for b, taps in _taps_by_phase(n_taps):
            vb = win_ref[pl.ds(first_row + b + r0, CONV_ROWS + 8 * (len(taps) - 1)), cl]
            for a, k in enumerate(taps):
                kw = n_taps - 1 - k if reverse else k
                term = vb[8 * a:8 * a + CONV_ROWS] * w_ref[kw:kw + 1, cl]
                acc = term if acc is None else acc + term
        out_ref[rows, cl] = acc if bias_ref is None else acc + bias_ref[:, cl]


def _dwcorr_acc(dwp_ref, x_ref, win_ref, n_taps, first_row, ts):
    for rows, r0, cl in _conv_chunks(ts, x_ref.shape[1]):
        xc = x_ref[rows, cl]
        for b, taps in _taps_by_phase(n_taps):
            vb = win_ref[pl.ds(first_row + b + r0, CONV_ROWS + 8 * (len(taps) - 1)), cl]
            for a, k in enumerate(taps):
                prod = xc * vb[8 * a:8 * a + CONV_ROWS]
                part = prod[0:8]
                for q in range(1, CONV_ROWS // 8):
                    part = part + prod[8 * q:8 * q + 8]
                dwp_ref[8 * k:8 * k + 8, cl] += part


def _lane_pack(cols, rows):
    lane = lax.broadcasted_iota(jnp.int32, (rows, 128), 1)
    out = jnp.zeros((rows, 128), F32)
    for h, c in enumerate(cols):
        out = jnp.where(lane == h, c, out)
    return out


def _mm(a, b, *, name, nt=False, out_dtype=F32, tm=512, tn=512, tk=512, add=None, add_scale=1.0, scatter=()):
    m, kdim = a.shape
    n = b.shape[0] if nt else b.shape[1]
    tm, tn, tk = min(tm, m), min(tn, n), min(tk, kdim)
    assert m % tm == 0 and n % tn == 0 and kdim % tk == 0, (name, a.shape, b.shape, tm, tn, tk)
    grid = (m // tm, n // tn, kdim // tk)
    nk = grid[2]
    nin = 2 if add is None else 3
    ns = len(scatter)

    def body(*refs):
        a_ref, b_ref = refs[:2]
        add_ref = None if add is None else refs[2]
        o_ref = refs[nin + ns]
        step = [pl.program_id(d) for d in range(3)]
        if ns:
            start, finish_comm = _scatter_copies(refs[nin:nin + ns], refs[nin + ns + 1:nin + 2 * ns + 1], *refs[-3:])
            pl.when((step[0] == 0) & (step[1] == 0) & (step[2] == 0))(start)

        def finish(r):
            if add is not None:
                r = r + add_scale * add_ref[...]
            o_ref[...] = r.astype(out_dtype)

        part = _dot(a_ref[...].astype(BF), b_ref[...].astype(BF), NT_DIMS if nt else None)
        if nk == 1:
            finish(part)
        else:
            acc_ref = refs[nin + 2 * ns + 1]
            k = step[2]

            @pl.when(k == 0)
            def _():
                acc_ref[...] = part

            @pl.when(k > 0)
            def _():
                acc_ref[...] += part

            @pl.when(k == nk - 1)
            def _():
                finish(acc_ref[...])

        if ns:
            pl.when((step[0] == grid[0] - 1) & (step[1] == grid[1] - 1) & (step[2] == nk - 1))(finish_comm)

    anyspec = pl.BlockSpec(memory_space=pl.ANY)
    in_specs = [pl.BlockSpec((tm, tk), lambda i, j, k: (i, k)),
                pl.BlockSpec((tn, tk), lambda i, j, k: (j, k)) if nt else pl.BlockSpec((tk, tn), lambda i, j, k: (k, j))]
    args = [a, b]
    if add is not None:
        in_specs.append(pl.BlockSpec((tm, tn), lambda i, j, k: (i, j)))
        args.append(add)
    res = pl.pallas_call(
        body, name=name, grid=grid,
        out_shape=(jax.ShapeDtypeStruct((m, n), out_dtype),) + _same_shapes(scatter),
        in_specs=in_specs + [anyspec] * ns, out_specs=(pl.BlockSpec((tm, tn), lambda i, j, k: (i, j)),) + (anyspec,) * ns,
        scratch_shapes=([pltpu.VMEM((tm, tn), F32)] if nk > 1 else []) + (_comm_sems(ns) if ns else []),
        compiler_params=_cp(("arbitrary",) * 3 if ns else ("parallel", "parallel", "arbitrary")),
    )(*args, *scatter)
    return (res[0], list(res[1:])) if ns else res[0]


def _mem_ln_fwd(mem, g, b):
    def body(m_ref, g_ref, b_ref, n_ref, h_ref):
        xh, _ = _ln_hat(m_ref[...])
        h_ref[...] = xh
        n_ref[...] = xh * g_ref[...] + b_ref[...]

    shp = jax.ShapeDtypeStruct(mem.shape, F32)
    return pl.pallas_call(body, name="mem_ln_fwd", out_shape=(shp, shp), compiler_params=_cp())(mem, g, b)


def _mem_ln_bwd(dns, mhat):
    def body(*refs):
        d_refs, h_ref, dg_ref, db_ref = refs[:NL], refs[NL], refs[NL + 1], refs[NL + 2]
        dn = d_refs[0][...]
        for r in d_refs[1:]:
            dn = dn + r[...]
        dg_ref[...] = _sum_r(dn * h_ref[...])
        db_ref[...] = _sum_r(dn)

    shp = jax.ShapeDtypeStruct((1, D), F32)
    return pl.pallas_call(body, name="mem_ln_bwd", out_shape=(shp, shp), compiler_params=_cp())(*dns, mhat)


def _pre_fwd(P, wA, bA, gA, betaA, wB, bfg, *, ts):
    S = P.shape[0]
    nt = S // ts
    cb = C_AB // 512

    def cur(j):
        return pl.BlockSpec((ts, 512), lambda i, j=j: (i, cb + j))

    def halo(j, rows):
        return pl.BlockSpec((rows, 512), lambda i, j=j: (jnp.maximum(i * (ts // rows) - 1, 0), cb + j))

    def full(shape):
        return pl.BlockSpec(shape, lambda i: (0, 0))

    def body(au, av, ag, bh, bb, bc, bg, f_ref, au_h, av_h, bh_h, bc_h, wA_r, bA_r, gA_r, betaA_r, wB_r, bf_r,
             za_o, zb_o, u_o, ca_o, vb_o, cb_o, cum_o, winA, winB, carry):
        i = pl.program_id(0)
        nz = (i > 0).astype(F32)

        u = au[...] * _sig(av[...])
        winA[0:HALO_A, :] = au_h[...] * _sig(av_h[...]) * nz
        winA[HALO_A:, :] = u
        _dwconv(ca_o, winA, wA_r, KA, HALO_A - KA + 1, ts, bias_ref=bA_r)
        ca = ca_o[...]
        xh, _ = _ln_hat(ca)
        n = xh * gA_r[...] + betaA_r[...]
        a = n * _sig(n)
        agv = ag[...]
        za_o[...] = (a * agv * _sig(agv)).astype(BF)
        u_o[...] = u

        vb = bc[...] * bh[...]
        winB[0:HALO_B, :] = bc_h[...] * bh_h[...] * nz
        winB[HALO_B:, :] = vb
        _dwconv(cb_o, winB, wB_r, KB, HALO_B - KB + 1, ts)
        accb = cb_o[...]
        bgv = bg[...]
        zb_o[...] = (bb[...] * accb * bgv * _sig(bgv)).astype(BF)
        vb_o[...] = vb

        @pl.when(i == 0)
        def _():
            carry[...] = jnp.zeros_like(carry)

        x = f_ref[...] + bf_r[...]
        logf = jnp.minimum(x, 0.0) - jnp.log1p(jnp.exp(-jnp.abs(x)))
        r = lax.broadcasted_iota(jnp.int32, (ts, ts), 0)
        c = lax.broadcasted_iota(jnp.int32, (ts, ts), 1)
        tri = (r >= c).astype(F32)
        cum = jnp.dot(tri, logf, precision=HIGHEST, preferred_element_type=F32) + carry[...]
        cum_o[...] = cum
        carry[...] = cum[ts - 1:ts, :]

    s512 = lambda dt: jax.ShapeDtypeStruct((S, 512), dt)
    o512 = pl.BlockSpec((ts, 512), lambda i: (i, 0))
    return pl.pallas_call(
        body, name="pre_fwd", grid=(nt,),
        out_shape=(s512(BF), s512(BF), s512(F32), s512(F32), s512(F32), s512(F32), jax.ShapeDtypeStruct((S, 128), F32)),
        in_specs=[cur(0), cur(1), cur(2), cur(3), cur(4), cur(5), cur(6),
                  pl.BlockSpec((ts, 128), lambda i: (i, C_F // 128)),
                  halo(0, HALO_A), halo(1, HALO_A), halo(3, HALO_B), halo(5, HALO_B),
                  full((32, 512)), full((1, 512)), full((1, 512)), full((1, 512)), full((8, 512)), full((1, 128))],
        out_specs=(o512, o512, o512, o512, o512, o512, pl.BlockSpec((ts, 128), lambda i: (i, 0))),
        scratch_shapes=[pltpu.VMEM((ts + HALO_A, 512), F32), pltpu.VMEM((ts + HALO_B, 512), F32), pltpu.VMEM((1, 128), F32)],
        compiler_params=_cp(("arbitrary",)),
    )(P, P, P, P, P, P, P, P, P, P, P, P, wA, bA, gA, betaA, wB, bfg)


RC = 32


def _split3(c):
    c1 = c.astype(BF).astype(F32)
    r = c - c1
    c2 = r.astype(BF).astype(F32)
    return c1, c2, r - c2


def _attn_prep(P, cum, *, ts):
    S = P.shape[0]

    def body(q_ref, kv_ref, cum_ref, qa_o, ka_o, v_o):
        lane = lax.broadcasted_iota(jnp.int32, (ts, DH), 1)
        for h in range(HC):
            sl = slice(DH * h, DH * (h + 1))
            c1, c2, c3 = _split3(cum_ref[:, h:h + 1])
            lo = jnp.where(lane == 0, c1, jnp.where(lane == 1, c2, jnp.where(lane == 2, c3, 0.0)))
            hi = jnp.where(lane == 3, c1, jnp.where(lane == 4, c2, jnp.where(lane == 5, c3, 0.0)))
            qa_o[h, :, 0:DH] = (q_ref[:, sl] * SCALE).astype(BF)
            qa_o[h, :, DH:2 * DH] = (lo + jnp.where((lane >= 3) & (lane < 6), 1.0, 0.0)).astype(BF)
            ka_o[h, :, 0:DH] = kv_ref[:, sl].astype(BF)
            ka_o[h, :, DH:2 * DH] = (jnp.where(lane < 3, 1.0, 0.0) - hi).astype(BF)
            v_o[h] = kv_ref[:, 512 + DH * h:512 + DH * (h + 1)].astype(BF)

    aug = jax.ShapeDtypeStruct((HC, S, 2 * DH), BF)
    return pl.pallas_call(
        body, name="attn_prep", grid=(S // ts,),
        out_shape=(aug, aug, jax.ShapeDtypeStruct((HC, S, DH), BF)),
        in_specs=[pl.BlockSpec((ts, 512), lambda i: (i, C_Q // 512)), pl.BlockSpec((ts, 1024), lambda i: (i, C_KV // 1024)),
                  pl.BlockSpec((ts, 128), lambda i: (i, 0))],
        out_specs=(pl.BlockSpec((HC, ts, 2 * DH), lambda i: (0, i, 0)), pl.BlockSpec((HC, ts, 2 * DH), lambda i: (0, i, 0)),
                   pl.BlockSpec((HC, ts, DH), lambda i: (0, i, 0))),
        compiler_params=_cp(("parallel",)),
    )(P, P, cum)


LW = 128


def _lanes(c):
    return slice(LW * c, LW * (c + 1))


def _diag_slices(rc, n, rows_are_queries):
    out = []
    for c in range(n // LW):
        r0, r1, c0, c1 = rc * RC, rc * RC + RC - 1, LW * c, LW * c + LW - 1
        lo, hi = (c1 <= r0, c0 > r1) if rows_are_queries else (r1 <= c0, r0 > c1)
        if lo:
            out.append("all")
        elif hi:
            out.append("none")
        else:
            r = lax.broadcasted_iota(jnp.int32, (RC, LW), 0) + r0
            cc = lax.broadcasted_iota(jnp.int32, (RC, LW), 1) + c0
            out.append((r >= cc) if rows_are_queries else (cc >= r))
    return out


def _pieces(ref2d, rows, rc, n, masked, rows_are_queries):
    kinds = _diag_slices(rc, n, rows_are_queries) if masked else ["all"] * (n // LW)
    out = []
    for c, kind in enumerate(kinds):
        if isinstance(kind, str):
            out.append(ref2d[rows, _lanes(c)] if kind == "all" else None)
        else:
            out.append(jnp.where(kind, ref2d[rows, _lanes(c)], NEG))
    return out


def _attn_fwd(P, Qa, Ka, V, *, tq, gather=()):
    S = P.shape[0]
    nq = S // tq
    ng = len(gather)

    def body(*refs):
        qa_ref, ka_ref, v_ref, cg_ref = refs[:4]
        o_ref, zc_ref, lse_ref, lsec_ref = refs[4 + ng:8 + ng]
        s_s, p_s, m_s, l_s, acc_s, pm_s, al_s = refs[8 + 2 * ng:15 + 2 * ng]
        i, j = pl.program_id(0), pl.program_id(1)
        if ng:
            start, finish = _gather_copies(refs[4:4 + ng], refs[8 + ng:8 + 2 * ng], *refs[15 + 2 * ng:])
            pl.when((i == 0) & (j == 0))(start)

        @pl.when(j == 0)
        def _():
            m_s[...] = jnp.full_like(m_s, NEG)
            l_s[...] = jnp.zeros_like(l_s)
            acc_s[...] = jnp.zeros_like(acc_s)

        def step(masked):
            s_s[0] = _dot(qa_ref[0], ka_ref[0], NT_DIMS)
            for h in range(HC):
                b = h % 2
                if h + 1 < HC:
                    s_s[1 - b] = _dot(qa_ref[h + 1], ka_ref[h + 1], NT_DIMS)
                for rc in range(tq // RC):
                    rows = slice(rc * RC, (rc + 1) * RC)
                    pm = None
                    for sc in _pieces(s_s.at[b], rows, rc, tq, masked, True):
                        if sc is not None:
                            pm = sc if pm is None else jnp.maximum(pm, sc)
                    pm_s[rows, :] = pm
                m_prev = m_s[h]
                m_new = jnp.maximum(m_prev, jnp.max(pm_s[...], axis=1, keepdims=True))
                alpha = jnp.exp(m_prev - m_new)
                m_s[h] = m_new
                al_s[...] = alpha
                for rc in range(tq // RC):
                    rows = slice(rc * RC, (rc + 1) * RC)
                    mb = m_s[h, rows]
                    ps = None
                    for c, sc in enumerate(_pieces(s_s.at[b], rows, rc, tq, masked, True)):
                        if sc is None:
                            p_s[b, rows, _lanes(c)] = jnp.zeros((RC, LW), BF)
                            continue
                        p = jnp.exp(sc - mb)
                        ps = p if ps is None else ps + p
                        p_s[b, rows, _lanes(c)] = p.astype(BF)
                    l_s[h, rows] = al_s[rows] * l_s[h, rows] + ps
                acc_s[h] = al_s[:, 0:DH] * acc_s[h] + _dot(p_s[b], v_ref[h])

        @pl.when(j < i)
        def _():
            step(False)

        @pl.when(j == i)
        def _():
            step(True)
            lses = []
            for h in range(HC):
                l = jnp.sum(l_s[h], axis=1, keepdims=True)
                o_ref[:, DH * h:DH * (h + 1)] = acc_s[h] / l
                lse = m_s[h] + jnp.log(l)
                lse_ref[h] = lse
                lses.append(lse[:, 0:1])
            lsec_ref[...] = _lane_pack(lses, tq)
            cg = cg_ref[...]
            zc_ref[...] = (o_ref[...] * cg * _sig(cg)).astype(BF)

        if ng:
            pl.when((i == nq - 1) & (j == nq - 1))(finish)

    stat = pltpu.VMEM((HC, tq, LW), F32)
    anyspec = pl.BlockSpec(memory_space=pl.ANY)
    res = pl.pallas_call(
        body, name="attn_fwd_gather" if ng else "attn_fwd", grid=(nq, nq),
        out_shape=(jax.ShapeDtypeStruct((S, 512), F32), jax.ShapeDtypeStruct((S, 512), BF), jax.ShapeDtypeStruct((HC, S, LW), F32),
                   jax.ShapeDtypeStruct((S, LW), F32)) + _gathered_shapes(gather),
        in_specs=[pl.BlockSpec((HC, tq, 2 * DH), lambda i, j: (0, i, 0)),
                  pl.BlockSpec((HC, tq, 2 * DH), lambda i, j: (0, jnp.minimum(i, j), 0)),
                  pl.BlockSpec((HC, tq, DH), lambda i, j: (0, jnp.minimum(i, j), 0)),
                  pl.BlockSpec((tq, 512), lambda i, j: (i, C_AB // 512 + 7))] + [anyspec] * ng,
        out_specs=(pl.BlockSpec((tq, 512), lambda i, j: (i, 0)), pl.BlockSpec((tq, 512), lambda i, j: (i, 0)),
                   pl.BlockSpec((HC, tq, LW), lambda i, j: (0, i, 0)), pl.BlockSpec((tq, LW), lambda i, j: (i, 0))) + (anyspec,) * ng,
        scratch_shapes=[pltpu.VMEM((2, tq, tq), F32), pltpu.VMEM((2, tq, tq), BF), stat, stat, pltpu.VMEM((HC, tq, DH), F32),
                        pltpu.VMEM((tq, LW), F32), pltpu.VMEM((tq, LW), F32)] + (_comm_sems(ng) if ng else []),
        compiler_params=_cp(("arbitrary", "arbitrary") if ng else ("parallel", "arbitrary")),
    )(Qa, Ka, V, P, *gather)
    return res[0], res[1], res[2], res[3], list(res[4:])


def _attn_bwd_dkv(Qa, Ka, V, dO, lseT, dltT, dP, *, tq, scatter=()):
    S = Qa.shape[1]
    nq = S // tq
    ns = len(scatter)

    def body(*refs):
        qa_ref, ka_ref, v_ref, do_ref, lse_ref, dl_ref = refs[:6]
        dkv_o, dck_o = refs[7 + ns:9 + ns]
        s_s, dp_s, p_s, ds_s, dk_s, dv_s, dck_s = refs[9 + 2 * ns:16 + 2 * ns]
        j, i = pl.program_id(0), pl.program_id(1)
        if ns:
            start, finish = _scatter_copies(refs[7:7 + ns], refs[9 + ns:9 + 2 * ns], *refs[16 + 2 * ns:])
            pl.when((i == 0) & (j == 0))(start)

        @pl.when(i == 0)
        def _():
            dk_s[...] = jnp.zeros_like(dk_s)
            dv_s[...] = jnp.zeros_like(dv_s)
            dck_s[...] = jnp.zeros_like(dck_s)

        def mm(h, b):
            s_s[b] = _dot(ka_ref[h], qa_ref[h], NT_DIMS)
            dp_s[b] = _dot(v_ref[h], do_ref[h], NT_DIMS)

        def step(masked):
            mm(0, 0)
            for h in range(HC):
                b = h % 2
                if h + 1 < HC:
                    mm(h + 1, 1 - b)
                for rc in range(tq // RC):
                    rows = slice(rc * RC, (rc + 1) * RC)
                    acc = None
                    for c, sc in enumerate(_pieces(s_s.at[b], rows, rc, tq, masked, False)):
                        if sc is None:
                            p_s[rows, _lanes(c)] = jnp.zeros((RC, LW), BF)
                            ds_s[rows, _lanes(c)] = jnp.zeros((RC, LW), BF)
                            continue
                        p = jnp.exp(sc - lse_ref[h:h + 1, _lanes(c)])
                        ds = p * (dp_s[b, rows, _lanes(c)] - dl_ref[h:h + 1, _lanes(c)])
                        p_s[rows, _lanes(c)] = p.astype(BF)
                        ds_s[rows, _lanes(c)] = ds.astype(BF)
                        acc = ds if acc is None else acc + ds
                    dck_s[h, rows] += acc
                dv_s[h] += _dot(p_s[...], do_ref[h])
                dk_s[h] += _dot(ds_s[...], qa_ref[h])

        @pl.when(i > j)
        def _():
            step(False)

        @pl.when(i == j)
        def _():
            step(True)

        @pl.when(i == nq - 1)
        def _():
            for h in range(HC):
                dkv_o[:, DH * h:DH * (h + 1)] = dk_s[h][:, 0:DH].astype(BF)
                dkv_o[:, 512 + DH * h:512 + DH * (h + 1)] = dv_s[h].astype(BF)
            dck_o[...] = _lane_pack([jnp.sum(dck_s[h], axis=1, keepdims=True) for h in range(HC)], tq)

        if ns:
            pl.when((i == nq - 1) & (j == nq - 1))(finish)

    def qspec(w):
        return pl.BlockSpec((HC, tq, w), lambda j, i: (0, jnp.maximum(i, j), 0))

    def kspec(w):
        return pl.BlockSpec((HC, tq, w), lambda j, i: (0, j, 0))

    rowv = pl.BlockSpec((8, tq), lambda j, i: (0, jnp.maximum(i, j)))
    anyspec = pl.BlockSpec(memory_space=pl.ANY)
    res = pl.pallas_call(
        body, name="attn_bwd_dkv_scatter" if ns else "attn_bwd_dkv", grid=(nq, nq),
        out_shape=(jax.ShapeDtypeStruct(dP.shape, BF), jax.ShapeDtypeStruct((S, 128), F32)) + _same_shapes(scatter),
        in_specs=[qspec(2 * DH), kspec(2 * DH), kspec(DH), qspec(DH), rowv, rowv, anyspec] + [anyspec] * ns,
        out_specs=(pl.BlockSpec((tq, 1024), lambda j, i: (j, C_KV // 1024)), pl.BlockSpec((tq, 128), lambda j, i: (j, 0)))
        + (anyspec,) * ns,
        scratch_shapes=[pltpu.VMEM((2, tq, tq), F32), pltpu.VMEM((2, tq, tq), F32), pltpu.VMEM((tq, tq), BF), pltpu.VMEM((tq, tq), BF),
                        pltpu.VMEM((HC, tq, 2 * DH), F32), pltpu.VMEM((HC, tq, DH), F32), pltpu.VMEM((HC, tq, LW), F32)]
        + (_comm_sems(ns) if ns else []),
        input_output_aliases={6: 0},
        compiler_params=_cp(("arbitrary", "arbitrary") if ns else ("parallel", "arbitrary")),
    )(Qa, Ka, V, dO, lseT, dltT, dP, *scatter)
    return res[0], res[1], list(res[2:])


def _attn_bwd_dq(Qa, Ka, V, dO, lse, dlt, dP, *, tq):
    S = Qa.shape[1]
    nq = S // tq

    def body(qa_ref, ka_ref, v_ref, do_ref, lse_ref, dl_ref, dp_in, dq_o, dcq_o, s_s, dp_s, ds_s, dq_s, dcq_s):
        del dp_in
        i, j = pl.program_id(0), pl.program_id(1)

        @pl.when(j == 0)
        def _():
            dq_s[...] = jnp.zeros_like(dq_s)
            dcq_s[...] = jnp.zeros_like(dcq_s)

        def mm(h, b):
            s_s[b] = _dot(qa_ref[h], ka_ref[h], NT_DIMS)
            dp_s[b] = _dot(do_ref[h], v_ref[h], NT_DIMS)

        def step(masked):
            mm(0, 0)
            for h in range(HC):
                b = h % 2
                if h + 1 < HC:
                    mm(h + 1, 1 - b)
                for rc in range(tq // RC):
                    rows = slice(rc * RC, (rc + 1) * RC)
                    lb = lse_ref[h, rows]
                    db = dl_ref[h, rows]
                    acc = None
                    for c, sc in enumerate(_pieces(s_s.at[b], rows, rc, tq, masked, True)):
                        if sc is None:
                            ds_s[rows, _lanes(c)] = jnp.zeros((RC, LW), BF)
                            continue
                        ds = jnp.exp(sc - lb) * (dp_s[b, rows, _lanes(c)] - db)
                        ds_s[rows, _lanes(c)] = ds.astype(BF)
                        acc = ds if acc is None else acc + ds
                    dcq_s[h, rows] += acc
                dq_s[h] += _dot(ds_s[...], ka_ref[h])

        @pl.when(j < i)
        def _():
            step(False)

        @pl.when(j == i)
        def _():
            step(True)
            for h in range(HC):
                dq_o[:, DH * h:DH * (h + 1)] = (dq_s[h][:, 0:DH] * SCALE).astype(BF)
            dcq_o[...] = _lane_pack([jnp.sum(dcq_s[h], axis=1, keepdims=True) for h in range(HC)], tq)

    def qspec(w):
        return pl.BlockSpec((HC, tq, w), lambda i, j: (0, i, 0))

    def kspec(w):
        return pl.BlockSpec((HC, tq, w), lambda i, j: (0, jnp.minimum(i, j), 0))

    colv = pl.BlockSpec((tq, 128), lambda i, j: (i, 0))
    return pl.pallas_call(
        body, name="attn_bwd_dq", grid=(nq, nq),
        out_shape=(jax.ShapeDtypeStruct(dP.shape, BF), jax.ShapeDtypeStruct((S, 128), F32)),
        in_specs=[qspec(2 * DH), kspec(2 * DH), kspec(DH), qspec(DH), qspec(LW), qspec(LW), pl.BlockSpec(memory_space=pl.ANY)],
        out_specs=(pl.BlockSpec((tq, 512), lambda i, j: (i, C_Q // 512)), colv),
        scratch_shapes=[pltpu.VMEM((2, tq, tq), F32), pltpu.VMEM((2, tq, tq), F32), pltpu.VMEM((tq, tq), BF),
                        pltpu.VMEM((HC, tq, 2 * DH), F32), pltpu.VMEM((HC, tq, LW), F32)],
        input_output_aliases={6: 0},
        compiler_params=_cp(("parallel", "arbitrary")),
    )(Qa, Ka, V, dO, lse, dlt, dP)


def _xattn_probs(qm_ref, kv_ref, h):
    sl = slice(DH * h, DH * (h + 1))
    s = _dot(qm_ref[:, sl].astype(BF), kv_ref[:, sl].astype(BF), NT_DIMS) * SCALE
    p = jnp.exp(s - jnp.max(s, axis=1, keepdims=True))
    return p / jnp.sum(p, axis=1, keepdims=True)


def _xattn_fwd(P, kv, *, ts):
    S = P.shape[0]

    def body(qm_ref, kv_ref, zm_o, o_s):
        for h in range(HM):
            sl = slice(DH * h, DH * (h + 1))
            p = _xattn_probs(qm_ref, kv_ref, h)
            o_s[:, sl] = _dot(p.astype(BF), kv_ref[:, ML + DH * h:ML + DH * (h + 1)].astype(BF))
        mg = qm_ref[:, 256:512]
        zm_o[...] = (o_s[...] * mg * _sig(mg)).astype(BF)

    return pl.pallas_call(
        body, name="xattn_fwd", grid=(S // ts,),
        out_shape=jax.ShapeDtypeStruct((S, 256), BF),
        in_specs=[pl.BlockSpec((ts, 512), lambda i: (i, C_M // 512)), pl.BlockSpec((ML, 512), lambda i: (0, 0))],
        out_specs=pl.BlockSpec((ts, 256), lambda i: (i, 0)),
        scratch_shapes=[pltpu.VMEM((ts, 256), F32)],
        compiler_params=_cp(("parallel",)),
    )(P, kv)


def _xattn_bwd(P, kv, dzm, dP, *, ts):
    S = P.shape[0]

    def body(qm_ref, kv_ref, dz_ref, dp_in, dqm_o, dkv_o):
        del dp_in
        i = pl.program_id(0)

        @pl.when(i == 0)
        def _():
            dkv_o[...] = jnp.zeros_like(dkv_o)

        mg = qm_ref[:, 256:512]
        sg = _sig(mg)
        for h in range(HM):
            sl = slice(DH * h, DH * (h + 1))
            vsl = slice(ML + DH * h, ML + DH * (h + 1))
            p = _xattn_probs(qm_ref, kv_ref, h)
            pb = p.astype(BF)
            vh = kv_ref[:, vsl].astype(BF)
            o = _dot(pb, vh)
            dz = dz_ref[:, sl]
            do = dz * mg[:, sl] * sg[:, sl]
            dqm_o[:, 256 + DH * h:256 + DH * (h + 1)] = (dz * o * _dsilu(mg[:, sl], sg[:, sl])).astype(BF)
            dob = do.astype(BF)
            dpv = _dot(dob, vh, NT_DIMS)
            ds = p * (dpv - jnp.sum(do * o, axis=1, keepdims=True))
            dsb = ds.astype(BF)
            dqm_o[:, sl] = (_dot(dsb, kv_ref[:, sl].astype(BF)) * SCALE).astype(BF)
            dkv_o[:, sl] += _dot(dsb, qm_ref[:, sl].astype(BF), TN_DIMS) * SCALE
            dkv_o[:, vsl] += _dot(pb, dob, TN_DIMS)

    return pl.pallas_call(
        body, name="xattn_bwd", grid=(S // ts,),
        out_shape=(jax.ShapeDtypeStruct(dP.shape, BF), jax.ShapeDtypeStruct((ML, 512), F32)),
        in_specs=[pl.BlockSpec((ts, 512), lambda i: (i, C_M // 512)), pl.BlockSpec((ML, 512), lambda i: (0, 0)),
                  pl.BlockSpec((ts, 256), lambda i: (i, 0)), pl.BlockSpec(memory_space=pl.ANY)],
        out_specs=(pl.BlockSpec((ts, 512), lambda i: (i, C_M // 512)), pl.BlockSpec((ML, 512), lambda i: (0, 0))),
        input_output_aliases={3: 0},
        compiler_params=_cp(("arbitrary",)),
    )(P, kv, dzm, dP)


def _merge_fwd(za, zb, zc, zm, P, x, pa, pb, pc, pm, wo, lng, lnb, *, ts):
    S = x.shape[0]

    def body(za_r, zb_r, zc_r, zm_r, g_r, x_r, pa_r, pb_r, pc_r, pm_r, wo_r, lng_r, lnb_r,
             mg_o, ya_o, yb_o, yc_o, ym_o, xn_o, xh_o, rs_o):
        merged = jnp.zeros((ts, D), F32)
        for t, (z_r, p_r, y_o) in enumerate(((za_r, pa_r, ya_o), (zb_r, pb_r, yb_o), (zc_r, pc_r, yc_o), (zm_r, pm_r, ym_o))):
            y = _dot(z_r[...], p_r[...])
            merged = merged + _sig(g_r[:, D * t:D * (t + 1)]) * y
            y_o[...] = y.astype(BF)
        mb = merged.astype(BF)
        mg_o[...] = mb
        r = ALPHA * x_r[...] + _dot(mb, wo_r[...])
        xh, rstd = _ln_hat(r)
        xh_o[...] = xh
        rs_o[...] = rstd
        xn_o[...] = xh * lng_r[...] + lnb_r[...]

    def rows(w):
        return pl.BlockSpec((ts, w), lambda i: (i, 0))

    def full(a):
        return pl.BlockSpec(a.shape, lambda i: (0, 0))

    sd = lambda dt: jax.ShapeDtypeStruct((S, D), dt)
    return pl.pallas_call(
        body, name="merge_fwd", grid=(S // ts,),
        out_shape=(sd(BF), sd(BF), sd(BF), sd(BF), sd(BF), sd(F32), sd(F32), jax.ShapeDtypeStruct((S, 1), F32)),
        in_specs=[rows(512), rows(512), rows(512), rows(256), pl.BlockSpec((ts, 4 * D), lambda i: (i, 0)), rows(D),
                  full(pa), full(pb), full(pc), full(pm), full(wo), full(lng), full(lnb)],
        out_specs=(rows(D),) * 7 + (rows(1),),
        compiler_params=_cp(("parallel",)),
    )(za, zb, zc, zm, P, x, pa, pb, pc, pm, wo, lng, lnb)


def _loss_fwd(y, tgt, *, ts):
    S = y.shape[0]

    def body(y_r, t_r, dy_o, l_o):
        @pl.when(pl.program_id(0) == 0)
        def _():
            l_o[...] = jnp.zeros_like(l_o)

        e = y_r[...] - t_r[...]
        dy_o[...] = e / D
        l_o[...] += 0.5 * jnp.sum(_sum_r(e * e), axis=1, keepdims=True) / D

    rows = pl.BlockSpec((ts, D), lambda i: (i, 0))
    return pl.pallas_call(
        body, name="loss", grid=(S // ts,),
        out_shape=(jax.ShapeDtypeStruct((S, D), F32), jax.ShapeDtypeStruct((1, 1), F32)),
        in_specs=[rows, rows], out_specs=(rows, pl.BlockSpec((1, 1), lambda i: (0, 0))),
        compiler_params=_cp(("arbitrary",)),
    )(y, tgt)


def _out_bwd(dxn, xh, rstd, merged, wo, lng, *, ts):
    S = dxn.shape[0]

    def body(dxn_r, xh_r, rs_r, mg_r, wo_r, lng_r, dr_o, dm_o, dwo_o, dlng_o, dlnb_o):
        @pl.when(pl.program_id(0) == 0)
        def _():
            dwo_o[...] = jnp.zeros_like(dwo_o)
            dlng_o[...] = jnp.zeros_like(dlng_o)
            dlnb_o[...] = jnp.zeros_like(dlnb_o)

        dxn = dxn_r[...]
        xh = xh_r[...]
        dr = _ln_bwd(dxn * lng_r[...], xh, rs_r[...])
        dr_o[...] = dr
        drb = dr.astype(BF)
        dm_o[...] = _dot(drb, wo_r[...], NT_DIMS)
        dwo_o[...] += _dot(mg_r[...], drb, TN_DIMS)
        dlng_o[...] += _sum_r(dxn * xh)
        dlnb_o[...] += _sum_r(dxn)

    rows = pl.BlockSpec((ts, D), lambda i: (i, 0))
    full = lambda shape: pl.BlockSpec(shape, lambda i: (0, 0))
    sd = jax.ShapeDtypeStruct((S, D), F32)
    vec = jax.ShapeDtypeStruct((1, D), F32)
    return pl.pallas_call(
        body, name="out_bwd", grid=(S // ts,),
        out_shape=(sd, sd, jax.ShapeDtypeStruct((D, D), F32), vec, vec),
        in_specs=[rows, rows, pl.BlockSpec((ts, 1), lambda i: (i, 0)), rows, full((D, D)), full((1, D))],
        out_specs=(rows, rows, full((D, D)), full((1, D)), full((1, D))),
        compiler_params=_cp(("arbitrary",)),
    )(dxn, xh, rstd, merged, wo, lng)


def _merge_bwd(dm, P, ya, yb, yc, ym, za, zb, zc, zm, pa, pb, pc, pm, *, ts):
    S = dm.shape[0]

    def body(dm_r, g_r, ya_r, yb_r, yc_r, ym_r, za_r, zb_r, zc_r, zm_r, pa_r, pb_r, pc_r, pm_r,
             dg_o, dza_o, dzb_o, dzc_o, dzm_o, dpa_o, dpb_o, dpc_o, dpm_o):
        @pl.when(pl.program_id(0) == 0)
        def _():
            for o in (dpa_o, dpb_o, dpc_o, dpm_o):
                o[...] = jnp.zeros_like(o)

        dm = dm_r[...]
        for t, (y_r, z_r, p_r, dz_o, dp_o) in enumerate(((ya_r, za_r, pa_r, dza_o, dpa_o), (yb_r, zb_r, pb_r, dzb_o, dpb_o),
                                                        (yc_r, zc_r, pc_r, dzc_o, dpc_o), (ym_r, zm_r, pm_r, dzm_o, dpm_o))):
            gate = _sig(g_r[:, D * t:D * (t + 1)])
            dg_o[:, D * t:D * (t + 1)] = (dm * y_r[...].astype(F32) * gate * (1.0 - gate)).astype(BF)
            dyb = (dm * gate).astype(BF)
            dz_o[...] = _dot(dyb, p_r[...], NT_DIMS)
            dp_o[...] += _dot(z_r[...], dyb, TN_DIMS)

    def rows(w):
        return pl.BlockSpec((ts, w), lambda i: (i, 0))

    def full(a):
        return pl.BlockSpec(a.shape, lambda i: (0, 0))

    return pl.pallas_call(
        body, name="merge_bwd", grid=(S // ts,),
        out_shape=(jax.ShapeDtypeStruct((S, NP), BF),
                   jax.ShapeDtypeStruct((S, 512), F32), jax.ShapeDtypeStruct((S, 512), F32),
                   jax.ShapeDtypeStruct((S, 512), F32), jax.ShapeDtypeStruct((S, 256), F32),
                   jax.ShapeDtypeStruct(pa.shape, F32), jax.ShapeDtypeStruct(pb.shape, F32),
                   jax.ShapeDtypeStruct(pc.shape, F32), jax.ShapeDtypeStruct(pm.shape, F32)),
        in_specs=[rows(D), pl.BlockSpec((ts, 4 * D), lambda i: (i, 0)), rows(D), rows(D), rows(D), rows(D),
                  rows(512), rows(512), rows(512), rows(256), full(pa), full(pb), full(pc), full(pm)],
        out_specs=(pl.BlockSpec((ts, 4 * D), lambda i: (i, 0)), rows(512), rows(512), rows(512), rows(256),
                   full(pa), full(pb), full(pc), full(pm)),
        compiler_params=_cp(("arbitrary",)),
    )(dm, P, ya, yb, yc, ym, za, zb, zc, zm, pa, pb, pc, pm)


def _branch_bwd(P, ca, cb, u, vb, dza, dzb, dzc, oc, wA, gA, betaA, wB, dP, *, ts):
    S = P.shape[0]
    nt = S // ts

    def rev(i):
        return nt - 1 - i

    def rows(w):
        return pl.BlockSpec((ts, w), lambda i: (rev(i), 0))

    def halo(rows_):
        return pl.BlockSpec((rows_, 512), lambda i: (jnp.maximum(rev(i) * (ts // rows_) - 1, 0), 0))

    def full(shape):
        return pl.BlockSpec(shape, lambda i: (0, 0))

    def body(pg, ca_r, cb_r, u_r, vb_r, uh_r, vh_r, dza_r, dzb_r, dzc_r, oc_r, wA_r, gA_r, betaA_r, wB_r, dp_in,
             dpg_o, do_o, dl_o, dlc_o, dwA_o, dbA_o, dgA_o, dbetaA_o, dwB_o, dwinA, uwin, haloA, dwinB, vwin, haloB, cv_s, dwpA, dwpB):
        del dp_in
        i = pl.program_id(0)
        nz = (rev(i) > 0).astype(F32)

        @pl.when(i == 0)
        def _():
            for o in (dwA_o, dbA_o, dgA_o, dbetaA_o, dwB_o, haloA, haloB, dwpA, dwpB):
                o[...] = jnp.zeros_like(o)

        def col(j):
            return pg[:, 512 * j:512 * (j + 1)]

        def put(j, val):
            dpg_o[:, 512 * j:512 * (j + 1)] = val.astype(BF)

        a_gate = col(2)
        xh, rstd = _ln_hat(ca_r[...])
        gA_v = gA_r[...]
        n = xh * gA_v + betaA_r[...]
        sn = _sig(n)
        a = n * sn
        sg = _sig(a_gate)
        dza = dza_r[...]
        put(2, dza * a * _dsilu(a_gate, sg))
        dn = dza * a_gate * sg * _dsilu(n, sn)
        dgA_o[...] += _sum_r(dn * xh)
        dbetaA_o[...] += _sum_r(dn)
        dca = _ln_bwd(dn * gA_v, xh, rstd)
        dbA_o[...] += _sum_r(dca)
        dwinA[0:ts, :] = dca
        dwinA[ts:, :] = haloA[...]
        haloA[...] = dca[0:HALO_A, :]
        uwin[0:HALO_A, :] = uh_r[...] * nz
        uwin[HALO_A:, :] = u_r[...]
        _dwcorr_acc(dwpA, dwinA.at[pl.ds(0, ts)], uwin, KA, HALO_A - KA + 1, ts)
        _dwconv(cv_s, dwinA, wA_r, KA, 0, ts, reverse=True)
        du = cv_s[...]
        sv = _sig(col(1))
        put(0, du * sv)
        put(1, du * col(0) * sv * (1.0 - sv))

        b_gate = col(6)
        sgb = _sig(b_gate)
        cbv = cb_r[...]
        b_b = col(4)
        dzb = dzb_r[...]
        put(6, dzb * b_b * cbv * _dsilu(b_gate, sgb))
        dhb = dzb * b_gate * sgb
        put(4, dhb * cbv)
        dcb = dhb * b_b
        dwinB[0:ts, :] = dcb
        dwinB[ts:, :] = haloB[...]
        haloB[...] = dcb[0:HALO_B, :]
        vwin[0:HALO_B, :] = vh_r[...] * nz
        vwin[HALO_B:, :] = vb_r[...]
        _dwcorr_acc(dwpB, dwinB.at[pl.ds(0, ts)], vwin, KB, HALO_B - KB + 1, ts)
        _dwconv(cv_s, dwinB, wB_r, KB, 0, ts, reverse=True)
        dv = cv_s[...]
        put(5, dv * col(3))
        put(3, dv * col(5))

        c_gate = col(7)
        sgc = _sig(c_gate)
        dzc = dzc_r[...]
        ocv = oc_r[...]
        put(7, dzc * ocv * _dsilu(c_gate, sgc))
        do = dzc * c_gate * sgc
        for h in range(HC):
            do_o[h] = do[:, DH * h:DH * (h + 1)].astype(BF)
        dd = do * ocv
        dls = [jnp.sum(dd[:, DH * h:DH * (h + 1)], axis=1, keepdims=True) for h in range(HC)]
        for h in range(HC):
            dl_o[h] = jnp.broadcast_to(dls[h], (ts, LW))
        dlc_o[...] = _lane_pack(dls, ts)

        @pl.when(i == nt - 1)
        def _():
            for k in range(KA):
                dwA_o[k:k + 1, :] = _sum_r(dwpA[8 * k:8 * k + 8, :])
            for k in range(KB):
                dwB_o[k:k + 1, :] = _sum_r(dwpB[8 * k:8 * k + 8, :])

    v512 = jax.ShapeDtypeStruct((1, 512), F32)
    return pl.pallas_call(
        body, name="branch_bwd", grid=(nt,),
        out_shape=(jax.ShapeDtypeStruct(dP.shape, BF), jax.ShapeDtypeStruct((HC, S, DH), BF), jax.ShapeDtypeStruct((HC, S, LW), F32),
                   jax.ShapeDtypeStruct((S, LW), F32),
                   jax.ShapeDtypeStruct((32, 512), F32), v512, v512, v512, jax.ShapeDtypeStruct((8, 512), F32)),
        in_specs=[pl.BlockSpec((ts, 4096), lambda i: (rev(i), C_AB // 4096)),
                  rows(512), rows(512), rows(512), rows(512), halo(HALO_A), halo(HALO_B),
                  rows(512), rows(512), rows(512), rows(512),
                  full((32, 512)), full((1, 512)), full((1, 512)), full((8, 512)), pl.BlockSpec(memory_space=pl.ANY)],
        out_specs=(pl.BlockSpec((ts, 4096), lambda i: (rev(i), C_AB // 4096)),
                   pl.BlockSpec((HC, ts, DH), lambda i: (0, rev(i), 0)), pl.BlockSpec((HC, ts, LW), lambda i: (0, rev(i), 0)),
                   rows(LW), full((32, 512)), full((1, 512)), full((1, 512)), full((1, 512)), full((8, 512))),
        scratch_shapes=[pltpu.VMEM((ts + HALO_A, 512), F32), pltpu.VMEM((ts + HALO_A, 512), F32), pltpu.VMEM((HALO_A, 512), F32),
                        pltpu.VMEM((ts + HALO_B, 512), F32), pltpu.VMEM((ts + HALO_B, 512), F32), pltpu.VMEM((HALO_B, 512), F32),
                        pltpu.VMEM((ts, 512), F32), pltpu.VMEM((8 * 32, 512), F32), pltpu.VMEM((8 * 8, 512), F32)],
        input_output_aliases={15: 0},
        compiler_params=_cp(("arbitrary",)),
    )(P, ca, cb, u, vb, u, vb, dza, dzb, dzc, oc, wA, gA, betaA, wB, dP)


def _cum_bwd(P, dcum, bfg, dP, *, ts):
    S = P.shape[0]
    nt = S // ts

    def body(f_ref, dc_ref, bf_r, dp_in, df_o, dbf_o, carry):
        del dp_in
        i = pl.program_id(0)

        @pl.when(i == 0)
        def _():
            carry[...] = jnp.zeros_like(carry)
            dbf_o[...] = jnp.zeros_like(dbf_o)

        r = lax.broadcasted_iota(jnp.int32, (ts, ts), 0)
        c = lax.broadcasted_iota(jnp.int32, (ts, ts), 1)
        tri = (r <= c).astype(F32)
        dlogf = jnp.dot(tri, dc_ref[...], precision=HIGHEST, preferred_element_type=F32) + carry[...]
        carry[...] = dlogf[0:1, :]
        x = f_ref[...] + bf_r[...]
        lane = lax.broadcasted_iota(jnp.int32, (ts, 128), 1)
        df = jnp.where(lane < HC, dlogf * _sig(-x), 0.0)
        df_o[...] = df.astype(BF)
        dbf_o[...] += _sum_r(df)

    blk = pl.BlockSpec((ts, 128), lambda i: (nt - 1 - i, C_F // 128))
    return pl.pallas_call(
        body, name="cum_bwd", grid=(nt,),
        out_shape=(jax.ShapeDtypeStruct(dP.shape, BF), jax.ShapeDtypeStruct((1, 128), F32)),
        in_specs=[blk, pl.BlockSpec((ts, 128), lambda i: (nt - 1 - i, 0)), pl.BlockSpec((1, 128), lambda i: (0, 0)),
                  pl.BlockSpec(memory_space=pl.ANY)],
        out_specs=(blk, pl.BlockSpec((1, 128), lambda i: (0, 0))),
        scratch_shapes=[pltpu.VMEM((1, 128), F32)],
        input_output_aliases={3: 0},
        compiler_params=_cp(("arbitrary",)),
    )(P, dcum, bfg, dP)


def _adamw(w, m, v, gparts, *, name, tr):
    rws, cols = w.shape
    tr = min(tr, rws)
    assert rws % tr == 0 and gparts.shape == (NDEV, rws, cols), (name, w.shape, gparts.shape)
    c1 = 1.0 - ADAM_B1 ** ADAM_STEP
    c2 = 1.0 - ADAM_B2 ** ADAM_STEP

    def body(w_r, m_r, v_r, g_r, g_o, d_o, m_o, v_o):
        g = g_r[0].astype(F32)
        for p in range(1, NDEV):
            g = g + g_r[p].astype(F32)
        mn = ADAM_B1 * m_r[...] + (1.0 - ADAM_B1) * g
        vn = ADAM_B2 * v_r[...] + (1.0 - ADAM_B2) * (g * g)
        g_o[...] = g
        m_o[...] = mn
        v_o[...] = vn
        d_o[...] = -ADAM_LR * ((mn / c1) / (jnp.sqrt(vn / c2) + ADAM_EPS) + ADAM_WD * w_r[...])

    blk = pl.BlockSpec((tr, cols), lambda i: (i, 0))
    shp = jax.ShapeDtypeStruct((rws, cols), F32)
    return pl.pallas_call(
        body, name=name, grid=(rws // tr,), out_shape=(shp,) * 4,
        in_specs=[blk, blk, blk, pl.BlockSpec((NDEV, tr, cols), lambda i: (0, i, 0))],
        out_specs=(blk,) * 4, compiler_params=_cp(("parallel",)),
    )(w, m, v, gparts)


def _slot(p):
    return 4 * p[0] + 2 * p[1] + p[2]


def _comm_sems(na):
    return [pltpu.SemaphoreType.DMA((na, 7)), pltpu.SemaphoreType.DMA((na, 7)), pltpu.SemaphoreType.DMA((na,))]


def _gather_copies(ins, outs, send_sems, recv_sems, local_sems):
    na = len(ins)
    x, y, c = lax.axis_index("x"), lax.axis_index("y"), lax.axis_index("c")
    me, sib = (x, y, c), (x, y, 1 - c)
    chips = [(1 - x, y), (x, 1 - y), (1 - x, 1 - y)]

    def cp(a, k, block, to, src=None):
        dst = outs[a].at[_slot(block)]
        return pltpu.make_async_remote_copy(src_ref=dst if src is None else src, dst_ref=dst,
                                            send_sem=send_sems.at[a, k], recv_sem=recv_sems.at[a, k],
                                            device_id=to, device_id_type=pl.DeviceIdType.MESH)

    def mine(a):
        return pltpu.make_async_copy(ins[a], outs[a].at[_slot(me)], local_sems.at[a])

    def first(a):
        return [cp(a, 0, me, sib, src=ins[a])] + [cp(a, 1 + j, me, (*chip, c), src=ins[a]) for j, chip in enumerate(chips)]

    def start():
        for a in range(na):
            mine(a).start()
            for f in first(a):
                f.start()

    def finish():
        for j, chip in enumerate(chips):
            for a in range(na):
                cp(a, 1 + j, (*chip, c), me).wait_recv()
                cp(a, 4 + j, (*chip, c), sib).start()
        for a in range(na):
            cp(a, 0, sib, me).wait_recv()
            for j, chip in enumerate(chips):
                cp(a, 4 + j, (*chip, 1 - c), me).wait_recv()
        for a in range(na):
            for f in first(a):
                f.wait_send()
            for j, chip in enumerate(chips):
                cp(a, 4 + j, (*chip, c), sib).wait_send()
            mine(a).wait()

    return start, finish


def _scatter_copies(ins, outs, send_sems, recv_sems, local_sems):
    na = len(ins)
    x, y, c = lax.axis_index("x"), lax.axis_index("y"), lax.axis_index("c")
    me = (x, y, c)
    peers = [(x ^ ((k >> 2) & 1), y ^ ((k >> 1) & 1), c ^ (k & 1)) for k in range(1, NDEV)]

    def cp(a, k, peer):
        return pltpu.make_async_remote_copy(src_ref=ins[a].at[_slot(peer)], dst_ref=outs[a].at[_slot(me)],
                                            send_sem=send_sems.at[a, k], recv_sem=recv_sems.at[a, k],
                                            device_id=peer, device_id_type=pl.DeviceIdType.MESH)

    def landed(a, k, peer):
        dst = outs[a].at[_slot(peer)]
        return pltpu.make_async_remote_copy(src_ref=dst, dst_ref=dst, send_sem=send_sems.at[a, k], recv_sem=recv_sems.at[a, k],
                                            device_id=peer, device_id_type=pl.DeviceIdType.MESH)

    def mine(a):
        return pltpu.make_async_copy(ins[a].at[_slot(me)], outs[a].at[_slot(me)], local_sems.at[a])

    def start():
        for a in range(na):
            mine(a).start()
            for k, peer in enumerate(peers):
                cp(a, k, peer).start()

    def finish():
        for a in range(na):
            for k, peer in enumerate(peers):
                landed(a, k, peer).wait_recv()
        for a in range(na):
            for k, peer in enumerate(peers):
                cp(a, k, peer).wait_send()
            mine(a).wait()

    return start, finish


def _comm_call(arrs, copies, out_shapes, *, name):
    na = len(arrs)

    def body(*refs):
        start, finish = copies(refs[:na], refs[na:2 * na], *refs[2 * na:])
        start()
        finish()

    anyspec = pl.BlockSpec(memory_space=pl.ANY)
    return pl.pallas_call(body, name=name, out_shape=out_shapes, in_specs=[anyspec] * na, out_specs=(anyspec,) * na,
                          scratch_shapes=_comm_sems(na))(*arrs)


def _gathered_shapes(arrs):
    return tuple(jax.ShapeDtypeStruct((NDEV,) + a.shape, a.dtype) for a in arrs)


def _same_shapes(arrs):
    return tuple(jax.ShapeDtypeStruct(a.shape, a.dtype) for a in arrs)


def _all_gather(arrs, *, name):
    return _comm_call(arrs, _gather_copies, _gathered_shapes(arrs), name=name)


def _gathered_to_layout(g4):
    parts = []
    for a, b in _RUNS:
        for d in range(a // SHARD_IN, (b - 1) // SHARD_IN + 1):
            lo, hi = max(a, d * SHARD_IN), min(b, (d + 1) * SHARD_IN)
            parts.append(g4[d, ..., lo - d * SHARD_IN:hi - d * SHARD_IN])
    parts.append(jnp.zeros(g4.shape[1:-1] + (NP - IN_COLS,), g4.dtype))
    return jnp.concatenate(parts, axis=-1)


def _layout_to_shards(w):
    offs, off = {}, 0
    for a, b in _RUNS:
        offs[a] = (b, off)
        off += b - a
    shards = []
    for d in range(NDEV):
        parts = []
        for a in sorted(offs):
            b, off = offs[a]
            lo, hi = max(a, d * SHARD_IN), min(b, (d + 1) * SHARD_IN)
            if lo < hi:
                parts.append(w[..., off + lo - a:off + hi - a])
        shards.append(jnp.concatenate(parts, axis=-1))
    return jnp.stack(shards)


def _tiles(S):
    ts = min(256, S)
    tsb = min(128, S)
    tq = min(512, S)
    return ts, tsb, tq


def _layer_fwd(x, mem_n, w, gather=()):
    ts, _, tq = _tiles(x.shape[0])
    P = _mm(x, w["W"], name="proj_fwd", tm=1024, tn=1152, tk=D)
    kv = _mm(mem_n, w["w_kv_mem"], name="kv_fwd", tm=ML, tn=512, tk=D)
    za, zb, u, ca, vb, cb, cum = _pre_fwd(P, w["wA"], w["conv_a_b"], w["ln_a_g"], w["ln_a_b"], w["wB"], w["b_forget"], ts=ts)
    Qa, Ka, V = _attn_prep(P, cum, ts=ts)
    oc, zc, lse, lse_c, gathered = _attn_fwd(P, Qa, Ka, V, tq=tq, gather=gather)
    zm = _xattn_fwd(P, kv, ts=tq)
    merged, ya, yb, yc, ym, xn, xh, rstd = _merge_fwd(za, zb, zc, zm, P, x, w["p_a"], w["p_b"], w["p_c"], w["p_m"], w["w_out"],
                                                      w["ln_g"], w["ln_b"], ts=ts)
    saved = (x, P, kv, za, zb, zc, zm, u, ca, vb, cb, Qa, Ka, V, oc, lse, lse_c, merged, ya, yb, yc, ym, xh, rstd)
    return xn, saved, gathered


def _layer_bwd(dx, saved, mem_n, w, scatter=(), own_chunks=None):
    (xl, P, kv, za, zb, zc, zm, u, ca, vb, cb, Qa, Ka, V, oc, lse, lse_c, merged, ya, yb, yc, ym, xh, rstd) = saved
    ts, tsb, tq = _tiles(xl.shape[0])
    g = {}
    dr, dm, g["w_out"], g["ln_g"], g["ln_b"] = _out_bwd(dx, xh, rstd, merged, w["w_out"], w["ln_g"], ts=ts)
    dP, dza, dzb, dzc, dzm, g["p_a"], g["p_b"], g["p_c"], g["p_m"] = _merge_bwd(
        dm, P, ya, yb, yc, ym, za, zb, zc, zm, w["p_a"], w["p_b"], w["p_c"], w["p_m"], ts=tsb)
    dP, do, dlt, dlt_c, dwA, g["conv_a_b"], g["ln_a_g"], g["ln_a_b"], dwB = _branch_bwd(
        P, ca, cb, u, vb, dza, dzb, dzc, oc, w["wA"], w["ln_a_g"], w["ln_a_b"], w["wB"], dP, ts=ts)
    g["conv_a_w"], g["conv_b_w"] = dwA[:KA], dwB[:KB]
    dP, dck, received = _attn_bwd_dkv(Qa, Ka, V, do, lse_c[:, :HC].T, dlt_c[:, :HC].T, dP, tq=tq, scatter=scatter)
    dP, dcq = _attn_bwd_dq(Qa, Ka, V, do, lse, dlt, dP, tq=tq)
    dP, dbf = _cum_bwd(P, dcq - dck, w["b_forget"], dP, ts=ts)
    g["b_forget"] = dbf[0, :HC]
    dP, dkv = _xattn_bwd(P, kv, dzm, dP, ts=tq)
    g["w_kv_mem"] = _mm(mem_n.T, dkv, name="wkv_bwd", tm=D, tn=512, tk=ML)
    dmem_n = _mm(dkv, w["w_kv_mem"], name="memn_bwd", nt=True, tm=ML, tn=D, tk=512)
    g["w_in"] = _mm(xl.T.astype(BF), dP, name="win_bwd", out_dtype=BF, tm=D, tn=1152, tk=1024)
    if own_chunks is None:
        dx = _mm(dP, w["W"], name="x_bwd", nt=True, tm=1024, tn=D, tk=1152, add=dr, add_scale=ALPHA)
        return dx, g, dmem_n, received, None
    dx, received_own = _mm(dP, w["W"], name="x_bwd_scatter", nt=True, tm=1024, tn=D, tk=1152, add=dr, add_scale=ALPHA,
                           scatter=own_chunks(g))
    return dx, g, dmem_n, received, received_own


_SMALL = (("b_forget", (NL, HC)), ("conv_a_b", (NL, 512)), ("ln_a_g", (NL, 512)), ("ln_a_b", (NL, 512)),
          ("mem_ln_g", (D,)), ("mem_ln_b", (D,)), ("ln_g", (NL, D)), ("ln_b", (NL, D)),
          ("conv_a_w", (NL, KA, 512)), ("conv_b_w", (NL, KB, 512)))


def _pack(parts, rows_mult=8):
    flat = jnp.concatenate([p.reshape(-1).astype(F32) for p in parts])
    n = flat.shape[0]
    rows = -(-n // 128)
    rows = -(-rows // rows_mult) * rows_mult
    return jnp.pad(flat, (0, rows * 128 - n)).reshape(rows, 128)


def _unpack(buf, shapes):
    flat = buf.reshape(-1)
    out, off = [], 0
    for shp in shapes:
        n = 1
        for d in shp:
            n *= d
        out.append(flat[off:off + n].reshape(shp))
        off += n
    return out


def kernel(x, mem, w_in, b_forget, conv_a_w, conv_a_b, ln_a_g, ln_a_b, conv_b_w, w_kv_mem, mem_ln_g, mem_ln_b, p_a, p_b, p_c, p_m, w_out, ln_g, ln_b, loss_target, m_w_in, m_b_forget, m_conv_a_w, m_conv_a_b, m_ln_a_g, m_ln_a_b, m_conv_b_w, m_w_kv_mem, m_mem_ln_g, m_mem_ln_b, m_p_a, m_p_b, m_p_c, m_p_m, m_w_out, m_ln_g, m_ln_b, v_w_in, v_b_forget, v_conv_a_w, v_conv_a_b, v_ln_a_g, v_ln_a_b, v_conv_b_w, v_w_kv_mem, v_mem_ln_g, v_mem_ln_b, v_p_a, v_p_b, v_p_c, v_p_m, v_w_out, v_ln_g, v_ln_b):
    wts = dict(w_in=w_in, b_forget=b_forget, conv_a_w=conv_a_w, conv_a_b=conv_a_b, ln_a_g=ln_a_g, ln_a_b=ln_a_b, conv_b_w=conv_b_w,
               w_kv_mem=w_kv_mem, mem_ln_g=mem_ln_g, mem_ln_b=mem_ln_b, p_a=p_a, p_b=p_b, p_c=p_c, p_m=p_m, w_out=w_out, ln_g=ln_g, ln_b=ln_b)
    mom = dict(w_in=m_w_in, b_forget=m_b_forget, conv_a_w=m_conv_a_w, conv_a_b=m_conv_a_b, ln_a_g=m_ln_a_g, ln_a_b=m_ln_a_b,
               conv_b_w=m_conv_b_w, w_kv_mem=m_w_kv_mem, mem_ln_g=m_mem_ln_g, mem_ln_b=m_mem_ln_b, p_a=m_p_a, p_b=m_p_b, p_c=m_p_c,
               p_m=m_p_m, w_out=m_w_out, ln_g=m_ln_g, ln_b=m_ln_b)
    vel = dict(w_in=v_w_in, b_forget=v_b_forget, conv_a_w=v_conv_a_w, conv_a_b=v_conv_a_b, ln_a_g=v_ln_a_g, ln_a_b=v_ln_a_b,
               conv_b_w=v_conv_b_w, w_kv_mem=v_w_kv_mem, mem_ln_g=v_mem_ln_g, mem_ln_b=v_mem_ln_b, p_a=v_p_a, p_b=v_p_b, p_c=v_p_c,
               p_m=v_p_m, w_out=v_w_out, ln_g=v_ln_g, ln_b=v_ln_b)
    names = ("w_in", "b_forget", "conv_a_w", "conv_a_b", "ln_a_g", "ln_a_b", "conv_b_w", "w_kv_mem", "mem_ln_g", "mem_ln_b",
             "p_a", "p_b", "p_c", "p_m", "w_out", "ln_g", "ln_b")
    mid = ("p_a", "p_b", "p_c", "p_m", "w_out", "w_kv_mem")
    me = 4 * lax.axis_index("x") + 2 * lax.axis_index("y") + lax.axis_index("c")

    row_sharded = ("w_out", "w_kv_mem")
    mid_shapes = [wts[n].shape[1:] for n in mid]
    mid_nrows = [s[0] * s[1] // 128 for s in mid_shapes]

    def mid_pack(d, l):
        return jnp.concatenate([d[n][l].reshape(-1, 128) for n in mid], axis=0)

    def layer_weights(l, g_win, g16, wA, wB):
        w = {"W": _gathered_to_layout(g_win), "wA": wA[l], "wB": wB[l]}
        off = 0
        for n, shp, nr in zip(mid, mid_shapes, mid_nrows):
            blk = g16[:, off:off + nr].reshape((NDEV,) + shp)
            off += nr
            w[n] = blk.reshape(NDEV * shp[0], shp[1]) if n in row_sharded else blk.transpose(1, 0, 2).reshape(shp[0], NDEV * shp[1])
        w["b_forget"] = jnp.pad(b_forget[l], (0, 128 - HC)).reshape(1, 128)
        for n, a in (("conv_a_b", conv_a_b), ("ln_a_g", ln_a_g), ("ln_a_b", ln_a_b), ("ln_g", ln_g), ("ln_b", ln_b)):
            w[n] = a[l].reshape(1, -1)
        return w

    def grad_chunks(g):
        parts = []
        for n, shp in zip(mid, mid_shapes):
            a = g[n]
            a = a.reshape(NDEV, shp[0], shp[1]) if n in row_sharded else a.reshape(shp[0], NDEV, shp[1]).transpose(1, 0, 2)
            parts.append(a.reshape(NDEV, -1, 128))
        return [_layout_to_shards(g["w_in"]), jnp.concatenate(parts, axis=1).astype(BF)]

    shards = [[w_in[l].astype(BF), mid_pack(wts, l).astype(BF)] for l in range(NL)]
    pk32 = jnp.concatenate([conv_a_w, conv_b_w], axis=1).reshape(NL * (KA + KB), 512 // NDEV)
    g_win, g16, g32 = _all_gather(shards[0] + [pk32], name="gather_first")
    conv = g32.reshape(NDEV, NL, KA + KB, 512 // NDEV).transpose(1, 2, 0, 3).reshape(NL, KA + KB, 512)
    wA = jnp.pad(conv[:, :KA], ((0, 0), (0, 32 - KA), (0, 0)))
    wB = jnp.pad(conv[:, KA:], ((0, 0), (0, 8 - KB), (0, 0)))

    mem_n, mem_hat = _mem_ln_fwd(mem[0], mem_ln_g.reshape(1, D), mem_ln_b.reshape(1, D))
    xl, lw, saved = x[0], [], []
    for l in range(NL):
        lw.append(layer_weights(l, g_win, g16, wA, wB))
        xl, sv, got = _layer_fwd(xl, mem_n, lw[l], gather=shards[l + 1] if l + 1 < NL else ())
        saved.append(sv)
        if got:
            g_win, g16 = got
    dx, loss = _loss_fwd(xl, loss_target[0], ts=_tiles(xl.shape[0])[0])
    loss = lax.psum(loss[0, 0], ("x", "y", "c"))

    g = [None] * NL
    dmem_n = [None] * NL
    recv = [None] * NL
    pending = ()
    for l in reversed(range(NL)):
        dx, g[l], dmem_n[l], got, recv[l] = _layer_bwd(dx, saved[l], mem_n, lw[l], scatter=pending,
                                                       own_chunks=grad_chunks if l == 0 else None)
        if got:
            recv[l + 1] = got
        pending = grad_chunks(g[l]) if l else ()
    gs = {n: jnp.stack([g[l][n].reshape(shp[1:]) for l in range(NL)]) for n, shp in _SMALL if len(shp) > 1}
    gs["mem_ln_g"], gs["mem_ln_b"] = _mem_ln_bwd(dmem_n, mem_hat)
    (r_small,) = _all_gather([_pack([gs[n] for n, _ in _SMALL])], name="gather_small")

    res = {}
    r_win = jnp.concatenate([recv[l][0] for l in range(NL)], axis=1)
    res["w_in"] = [a.reshape(NL, D, SHARD_IN) for a in
                   _adamw(w_in.reshape(NL * D, SHARD_IN), m_w_in.reshape(NL * D, SHARD_IN), v_w_in.reshape(NL * D, SHARD_IN),
                          r_win, name="adamw_w_in", tr=128)]
    pk = lambda d: jnp.concatenate([mid_pack(d, l) for l in range(NL)], axis=0)
    o16 = _adamw(pk(wts), pk(mom), pk(vel), jnp.concatenate([recv[l][1] for l in range(NL)], axis=1), name="adamw_mid", tr=1024)
    for idx, (n, shp, nr) in enumerate(zip(mid, mid_shapes, mid_nrows)):
        off = [l * sum(mid_nrows) + sum(mid_nrows[:idx]) for l in range(NL)]
        res[n] = [jnp.stack([o[f:f + nr].reshape(shp) for f in off]) for o in o16]

    def small_view(d, n):
        a = d[n]
        if n in ("conv_a_w", "conv_b_w"):
            fullw = jnp.zeros(a.shape[:2] + (512,), F32)
            return lax.dynamic_update_slice(fullw, a, (0, 0, me * (512 // NDEV)))
        return a

    spk = lambda d: _pack([small_view(d, n) for n, _ in _SMALL])
    osm = _adamw(spk(wts), spk(mom), spk(vel), r_small, name="adamw_small", tr=1024)
    osm = [_unpack(o, [s for _, s in _SMALL]) for o in osm]
    for idx, (n, _) in enumerate(_SMALL):
        vals = [o[idx] for o in osm]
        if n in ("conv_a_w", "conv_b_w"):
            vals = [lax.dynamic_slice(a, (0, 0, me * (512 // NDEV)), a.shape[:2] + (512 // NDEV,)) for a in vals]
        res[n] = vals

    outs = [loss, dx[None]]
    for k in range(4):
        outs += [res[n][k] for n in names]
    return tuple(outs)
```

```python
import jax
import jax.numpy as jnp
from jax import lax
from jax.experimental import pallas as pl
from jax.experimental.pallas import tpu as pltpu

F32 = jnp.float32
BF = jnp.bfloat16
HIGHEST = lax.Precision.HIGHEST

D = 1024
NL = 4
NDEV = 8
HC, DH = 8, 64
HM = 4
ML = 256
KA, KB = 31, 3
HALO_A, HALO_B = 32, 8
ALPHA = (2.0 * NL) ** 0.25
EPS = 1e-5
SCALE = DH ** -0.5
NEG = -1e30

ADAM_LR, ADAM_B1, ADAM_B2, ADAM_EPS, ADAM_WD, ADAM_STEP = 0.001, 0.9, 0.999, 1e-08, 0.01, 10

C_G = 0
C_AB = 4096
C_Q = 8192
C_M = 8704
C_KV = 9216
C_F = 10240
NP = 10368
_RUNS = ((6152, 10248), (0, 3584), (5128, 5640), (3584, 4096), (5640, 6152), (4096, 5120), (5120, 5128))
IN_COLS = 10248
SHARD_IN = IN_COLS // NDEV

VMEM_LIMIT = 56 * 1024 * 1024

NT_DIMS = (((1,), (1,)), ((), ()))
TN_DIMS = (((0,), (0,)), ((), ()))


def _cp(sem=None):
    return pltpu.CompilerParams(dimension_semantics=sem, vmem_limit_bytes=VMEM_LIMIT)


def _sig(x):
    return 1.0 / (1.0 + jnp.exp(-x))


def _dsilu(x, s):
    return s * (1.0 + x * (1.0 - s))


def _mean_l(x):
    return jnp.mean(x, axis=-1, keepdims=True)


def _sum_r(x):
    return jnp.sum(x, axis=0, keepdims=True)


def _ln_hat(x):
    mu = _mean_l(x)
    xc = x - mu
    rstd = lax.rsqrt(_mean_l(xc * xc) + EPS)
    return xc * rstd, rstd


def _ln_bwd(dxh, xh, rstd):
    return rstd * (dxh - _mean_l(dxh) - xh * _mean_l(dxh * xh))


def _dot(a, b, dims=None):
    if dims is None:
        return jnp.dot(a, b, preferred_element_type=F32)
    return lax.dot_general(a, b, dims, preferred_element_type=F32)


def _taps_by_phase(n_taps):
    return [(b, list(range(b, n_taps, 8))) for b in range(min(8, n_taps))]


CONV_ROWS = 64


def _conv_chunks(ts, width):
    return [(slice(r, r + CONV_ROWS), r, slice(c, c + 128)) for c in range(0, width, 128) for r in range(0, ts, CONV_ROWS)]


def _dwconv(out_ref, win_ref, w_ref, n_taps, first_row, ts, reverse=False, bias_ref=None):
    for rows, r0, cl in _conv_chunks(ts, out_ref.shape[1]):
        acc = None
        for b, taps in _taps_by_phase(n_taps):
            vb = win_ref[pl.ds(first_row + b + r0, CONV_ROWS + 8 * (len(taps) - 1)), cl]
            for a, k in enumerate(taps):
                kw = n_taps - 1 - k if reverse else k
                term = vb[8 * a:8 * a + CONV_ROWS] * w_ref[kw:kw + 1, cl]
                acc = term if acc is None else acc + term
        out_ref[rows, cl] = acc if bias_ref is None else acc + bias_ref[:, cl]


def _dwcorr_acc(dwp_ref, x_ref, win_ref, n_taps, first_row, ts):
    for rows, r0, cl in _conv_chunks(ts, x_ref.shape[1]):
        xc = x_ref[rows, cl]
        for b, taps in _taps_by_phase(n_taps):
            vb = win_ref[pl.ds(first_row + b + r0, CONV_ROWS + 8 * (len(taps) - 1)), cl]
            for a, k in enumerate(taps):
                prod = xc * vb[8 * a:8 * a + CONV_ROWS]
                part = prod[0:8]
                for q in range(1, CONV_ROWS // 8):
                    part = part + prod[8 * q:8 * q + 8]
                dwp_ref[8 * k:8 * k + 8, cl] += part


def _lane_pack(cols, rows):
    lane = lax.broadcasted_iota(jnp.int32, (rows, 128), 1)
    out = jnp.zeros((rows, 128), F32)
    for h, c in enumerate(cols):
        out = jnp.where(lane == h, c, out)
    return out


def _mm(a, b, *, name, nt=False, out_dtype=F32, tm=512, tn=512, tk=512, add=None, add_scale=1.0, scatter=()):
    m, kdim = a.shape
    n = b.shape[0] if nt else b.shape[1]
    tm, tn, tk = min(tm, m), min(tn, n), min(tk, kdim)
    assert m % tm == 0 and n % tn == 0 and kdim % tk == 0, (name, a.shape, b.shape, tm, tn, tk)
    grid = (m // tm, n // tn, kdim // tk)
    nk = grid[2]
    nin = 2 if add is None else 3
    ns = len(scatter)

    def body(*refs):
        a_ref, b_ref = refs[:2]
        add_ref = None if add is None else refs[2]
        o_ref = refs[nin + ns]
        step = [pl.program_id(d) for d in range(3)]
        if ns:
            start, finish_comm = _scatter_copies(refs[nin:nin + ns], refs[nin + ns + 1:nin + 2 * ns + 1], *refs[-3:])
            pl.when((step[0] == 0) & (step[1] == 0) & (step[2] == 0))(start)

        def finish(r):
            if add is not None:
                r = r + add_scale * add_ref[...]
            o_ref[...] = r.astype(out_dtype)

        part = _dot(a_ref[...].astype(BF), b_ref[...].astype(BF), NT_DIMS if nt else None)
        if nk == 1:
            finish(part)
        else:
            acc_ref = refs[nin + 2 * ns + 1]
            k = step[2]

            @pl.when(k == 0)
            def _():
                acc_ref[...] = part

            @pl.when(k > 0)
            def _():
                acc_ref[...] += part

            @pl.when(k == nk - 1)
            def _():
                finish(acc_ref[...])

        if ns:
            pl.when((step[0] == grid[0] - 1) & (step[1] == grid[1] - 1) & (step[2] == nk - 1))(finish_comm)

    anyspec = pl.BlockSpec(memory_space=pl.ANY)
    in_specs = [pl.BlockSpec((tm, tk), lambda i, j, k: (i, k)),
                pl.BlockSpec((tn, tk), lambda i, j, k: (j, k)) if nt else pl.BlockSpec((tk, tn), lambda i, j, k: (k, j))]
    args = [a, b]
    if add is not None:
        in_specs.append(pl.BlockSpec((tm, tn), lambda i, j, k: (i, j)))
        args.append(add)
    res = pl.pallas_call(
        body, name=name, grid=grid,
        out_shape=(jax.ShapeDtypeStruct((m, n), out_dtype),) + _same_shapes(scatter),
        in_specs=in_specs + [anyspec] * ns, out_specs=(pl.BlockSpec((tm, tn), lambda i, j, k: (i, j)),) + (anyspec,) * ns,
        scratch_shapes=([pltpu.VMEM((tm, tn), F32)] if nk > 1 else []) + (_comm_sems(ns) if ns else []),
        compiler_params=_cp(("arbitrary",) * 3 if ns else ("parallel", "parallel", "arbitrary")),
    )(*args, *scatter)
    return (res[0], list(res[1:])) if ns else res[0]


def _mem_ln_fwd(mem, g, b):
    def body(m_ref, g_ref, b_ref, n_ref, h_ref):
        xh, _ = _ln_hat(m_ref[...])
        h_ref[...] = xh
        n_ref[...] = xh * g_ref[...] + b_ref[...]

    shp = jax.ShapeDtypeStruct(mem.shape, F32)
    return pl.pallas_call(body, name="mem_ln_fwd", out_shape=(shp, shp), compiler_params=_cp())(mem, g, b)


def _mem_ln_bwd(dns, mhat):
    def body(*refs):
        d_refs, h_ref, dg_ref, db_ref = refs[:NL], refs[NL], refs[NL + 1], refs[NL + 2]
        dn = d_refs[0][...]
        for r in d_refs[1:]:
            dn = dn + r[...]
        dg_ref[...] = _sum_r(dn * h_ref[...])
        db_ref[...] = _sum_r(dn)

    shp = jax.ShapeDtypeStruct((1, D), F32)
    return pl.pallas_call(body, name="mem_ln_bwd", out_shape=(shp, shp), compiler_params=_cp())(*dns, mhat)


def _pre_fwd(P, wA, bA, gA, betaA, wB, bfg, *, ts):
    S = P.shape[0]
    nt = S // ts
    cb = C_AB // 512

    def cur(j):
        return pl.BlockSpec((ts, 512), lambda i, j=j: (i, cb + j))

    def halo(j, rows):
        return pl.BlockSpec((rows, 512), lambda i, j=j: (jnp.maximum(i * (ts // rows) - 1, 0), cb + j))

    def full(shape):
        return pl.BlockSpec(shape, lambda i: (0, 0))

    def body(au, av, ag, bh, bb, bc, bg, f_ref, au_h, av_h, bh_h, bc_h, wA_r, bA_r, gA_r, betaA_r, wB_r, bf_r,
             za_o, zb_o, u_o, ca_o, vb_o, cb_o, cum_o, winA, winB, carry):
        i = pl.program_id(0)
        nz = (i > 0).astype(F32)

        u = au[...] * _sig(av[...])
        winA[0:HALO_A, :] = au_h[...] * _sig(av_h[...]) * nz
        winA[HALO_A:, :] = u
        _dwconv(ca_o, winA, wA_r, KA, HALO_A - KA + 1, ts, bias_ref=bA_r)
        ca = ca_o[...]
        xh, _ = _ln_hat(ca)
        n = xh * gA_r[...] + betaA_r[...]
        a = n * _sig(n)
        agv = ag[...]
        za_o[...] = (a * agv * _sig(agv)).astype(BF)
        u_o[...] = u

        vb = bc[...] * bh[...]
        winB[0:HALO_B, :] = bc_h[...] * bh_h[...] * nz
        winB[HALO_B:, :] = vb
        _dwconv(cb_o, winB, wB_r, KB, HALO_B - KB + 1, ts)
        accb = cb_o[...]
        bgv = bg[...]
        zb_o[...] = (bb[...] * accb * bgv * _sig(bgv)).astype(BF)
        vb_o[...] = vb

        @pl.when(i == 0)
        def _():
            carry[...] = jnp.zeros_like(carry)

        x = f_ref[...] + bf_r[...]
        logf = jnp.minimum(x, 0.0) - jnp.log1p(jnp.exp(-jnp.abs(x)))
        r = lax.broadcasted_iota(jnp.int32, (ts, ts), 0)
        c = lax.broadcasted_iota(jnp.int32, (ts, ts), 1)
        tri = (r >= c).astype(F32)
        cum = jnp.dot(tri, logf, precision=HIGHEST, preferred_element_type=F32) + carry[...]
        cum_o[...] = cum
        carry[...] = cum[ts - 1:ts, :]

    s512 = lambda dt: jax.ShapeDtypeStruct((S, 512), dt)
    o512 = pl.BlockSpec((ts, 512), lambda i: (i, 0))
    return pl.pallas_call(
        body, name="pre_fwd", grid=(nt,),
        out_shape=(s512(BF), s512(BF), s512(F32), s512(F32), s512(F32), s512(F32), jax.ShapeDtypeStruct((S, 128), F32)),
        in_specs=[cur(0), cur(1), cur(2), cur(3), cur(4), cur(5), cur(6),
                  pl.BlockSpec((ts, 128), lambda i: (i, C_F // 128)),
                  halo(0, HALO_A), halo(1, HALO_A), halo(3, HALO_B), halo(5, HALO_B),
                  full((32, 512)), full((1, 512)), full((1, 512)), full((1, 512)), full((8, 512)), full((1, 128))],
        out_specs=(o512, o512, o512, o512, o512, o512, pl.BlockSpec((ts, 128), lambda i: (i, 0))),
        scratch_shapes=[pltpu.VMEM((ts + HALO_A, 512), F32), pltpu.VMEM((ts + HALO_B, 512), F32), pltpu.VMEM((1, 128), F32)],
        compiler_params=_cp(("arbitrary",)),
    )(P, P, P, P, P, P, P, P, P, P, P, P, wA, bA, gA, betaA, wB, bfg)


RC = 32


def _split3(c):
    c1 = c.astype(BF).astype(F32)
    r = c - c1
    c2 = r.astype(BF).astype(F32)
    return c1, c2, r - c2


def _attn_prep(P, cum, *, ts):
    S = P.shape[0]

    def body(q_ref, kv_ref, cum_ref, qa_o, ka_o, v_o):
        lane = lax.broadcasted_iota(jnp.int32, (ts, DH), 1)
        for h in range(HC):
            sl = slice(DH * h, DH * (h + 1))
            c1, c2, c3 = _split3(cum_ref[:, h:h + 1])
            lo = jnp.where(lane == 0, c1, jnp.where(lane == 1, c2, jnp.where(lane == 2, c3, 0.0)))
            hi = jnp.where(lane == 3, c1, jnp.where(lane == 4, c2, jnp.where(lane == 5, c3, 0.0)))
            qa_o[h, :, 0:DH] = (q_ref[:, sl] * SCALE).astype(BF)
            qa_o[h, :, DH:2 * DH] = (lo + jnp.where((lane >= 3) & (lane < 6), 1.0, 0.0)).astype(BF)
            ka_o[h, :, 0:DH] = kv_ref[:, sl].astype(BF)
            ka_o[h, :, DH:2 * DH] = (jnp.where(lane < 3, 1.0, 0.0) - hi).astype(BF)
            v_o[h] = kv_ref[:, 512 + DH * h:512 + DH * (h + 1)].astype(BF)

    aug = jax.ShapeDtypeStruct((HC, S, 2 * DH), BF)
    return pl.pallas_call(
        body, name="attn_prep", grid=(S // ts,),
        out_shape=(aug, aug, jax.ShapeDtypeStruct((HC, S, DH), BF)),
        in_specs=[pl.BlockSpec((ts, 512), lambda i: (i, C_Q // 512)), pl.BlockSpec((ts, 1024), lambda i: (i, C_KV // 1024)),
                  pl.BlockSpec((ts, 128), lambda i: (i, 0))],
        out_specs=(pl.BlockSpec((HC, ts, 2 * DH), lambda i: (0, i, 0)), pl.BlockSpec((HC, ts, 2 * DH), lambda i: (0, i, 0)),
                   pl.BlockSpec((HC, ts, DH), lambda i: (0, i, 0))),
        compiler_params=_cp(("parallel",)),
    )(P, P, cum)


LW = 128


def _lanes(c):
    return slice(LW * c, LW * (c + 1))


def _diag_slices(rc, n, rows_are_queries):
    out = []
    for c in range(n // LW):
        r0, r1, c0, c1 = rc * RC, rc * RC + RC - 1, LW * c, LW * c + LW - 1
        lo, hi = (c1 <= r0, c0 > r1) if rows_are_queries else (r1 <= c0, r0 > c1)
        if lo:
            out.append("all")
        elif hi:
            out.append("none")
        else:
            r = lax.broadcasted_iota(jnp.int32, (RC, LW), 0) + r0
            cc = lax.broadcasted_iota(jnp.int32, (RC, LW), 1) + c0
            out.append((r >= cc) if rows_are_queries else (cc >= r))
    return out


def _pieces(ref2d, rows, rc, n, masked, rows_are_queries):
    kinds = _diag_slices(rc, n, rows_are_queries) if masked else ["all"] * (n // LW)
    out = []
    for c, kind in enumerate(kinds):
        if isinstance(kind, str):
            out.append(ref2d[rows, _lanes(c)] if kind == "all" else None)
        else:
            out.append(jnp.where(kind, ref2d[rows, _lanes(c)], NEG))
    return out


def _attn_fwd(P, Qa, Ka, V, *, tq, gather=()):
    S = P.shape[0]
    nq = S // tq
    ng = len(gather)

    def body(*refs):
        qa_ref, ka_ref, v_ref, cg_ref = refs[:4]
        o_ref, zc_ref, lse_ref, lsec_ref = refs[4 + ng:8 + ng]
        s_s, p_s, m_s, l_s, acc_s, pm_s, al_s = refs[8 + 2 * ng:15 + 2 * ng]
        i, j = pl.program_id(0), pl.program_id(1)
        if ng:
            start, finish = _gather_copies(refs[4:4 + ng], refs[8 + ng:8 + 2 * ng], *refs[15 + 2 * ng:])
            pl.when((i == 0) & (j == 0))(start)

        @pl.when(j == 0)
        def _():
            m_s[...] = jnp.full_like(m_s, NEG)
            l_s[...] = jnp.zeros_like(l_s)
            acc_s[...] = jnp.zeros_like(acc_s)

        def step(masked):
            s_s[0] = _dot(qa_ref[0], ka_ref[0], NT_DIMS)
            for h in range(HC):
                b = h % 2
                if h + 1 < HC:
                    s_s[1 - b] = _dot(qa_ref[h + 1], ka_ref[h + 1], NT_DIMS)
                for rc in range(tq // RC):
                    rows = slice(rc * RC, (rc + 1) * RC)
                    pm = None
                    for sc in _pieces(s_s.at[b], rows, rc, tq, masked, True):
                        if sc is not None:
                            pm = sc if pm is None else jnp.maximum(pm, sc)
                    pm_s[rows, :] = pm
                m_prev = m_s[h]
                m_new = jnp.maximum(m_prev, jnp.max(pm_s[...], axis=1, keepdims=True))
                alpha = jnp.exp(m_prev - m_new)
                m_s[h] = m_new
                al_s[...] = alpha
                for rc in range(tq // RC):
                    rows = slice(rc * RC, (rc + 1) * RC)
                    mb = m_s[h, rows]
                    ps = None
                    for c, sc in enumerate(_pieces(s_s.at[b], rows, rc, tq, masked, True)):
                        if sc is None:
                            p_s[b, rows, _lanes(c)] = jnp.zeros((RC, LW), BF)
                            continue
                        p = jnp.exp(sc - mb)
                        ps = p if ps is None else ps + p
                        p_s[b, rows, _lanes(c)] = p.astype(BF)
                    l_s[h, rows] = al_s[rows] * l_s[h, rows] + ps
                acc_s[h] = al_s[:, 0:DH] * acc_s[h] + _dot(p_s[b], v_ref[h])

        @pl.when(j < i)
        def _():
            step(False)

        @pl.when(j == i)
        def _():
            step(True)
            lses = []
            for h in range(HC):
                l = jnp.sum(l_s[h], axis=1, keepdims=True)
                o_ref[:, DH * h:DH * (h + 1)] = acc_s[h] / l
                lse = m_s[h] + jnp.log(l)
                lse_ref[h] = lse
                lses.append(lse)
            lsec_ref[...] = _lane_pack(lses, tq)
            cg = cg_ref[...]
            zc_ref[...] = (o_ref[...] * cg * _sig(cg)).astype(BF)

        if ng:
            pl.when((i == nq - 1) & (j == nq - 1))(finish)

    stat = pltpu.VMEM((HC, tq, LW), F32)
    anyspec = pl.BlockSpec(memory_space=pl.ANY)
    res = pl.pallas_call(
        body, name="attn_fwd_gather" if ng else "attn_fwd", grid=(nq, nq),
        out_shape=(jax.ShapeDtypeStruct((S, 512), F32), jax.ShapeDtypeStruct((S, 512), BF), jax.ShapeDtypeStruct((HC, S, LW), F32),
                   jax.ShapeDtypeStruct((S, LW), F32)) + _gathered_shapes(gather),
        in_specs=[pl.BlockSpec((HC, tq, 2 * DH), lambda i, j: (0, i, 0)),
                  pl.BlockSpec((HC, tq, 2 * DH), lambda i, j: (0, jnp.minimum(i, j), 0)),
                  pl.BlockSpec((HC, tq, DH), lambda i, j: (0, jnp.minimum(i, j), 0)),
                  pl.BlockSpec((tq, 512), lambda i, j: (i, C_AB // 512 + 7))] + [anyspec] * ng,
        out_specs=(pl.BlockSpec((tq, 512), lambda i, j: (i, 0)), pl.BlockSpec((tq, 512), lambda i, j: (i, 0)),
                   pl.BlockSpec((HC, tq, LW), lambda i, j: (0, i, 0)), pl.BlockSpec((tq, LW), lambda i, j: (i, 0))) + (anyspec,) * ng,
        scratch_shapes=[pltpu.VMEM((2, tq, tq), F32), pltpu.VMEM((2, tq, tq), BF), stat, stat, pltpu.VMEM((HC, tq, DH), F32),
                        pltpu.VMEM((tq, LW), F32), pltpu.VMEM((tq, LW), F32)] + (_comm_sems(ng) if ng else []),
        compiler_params=_cp(("arbitrary", "arbitrary") if ng else ("parallel", "arbitrary")),
    )(Qa, Ka, V, P, *gather)
    return res[0], res[1], res[2], res[3], list(res[4:])


def _attn_bwd_dkv(Qa, Ka, V, dO, lseT, dltT, dP, *, tq, scatter=()):
    S = Qa.shape[1]
    nq = S // tq
    ns = len(scatter)

    def body(*refs):
        qa_ref, ka_ref, v_ref, do_ref, lse_ref, dl_ref = refs[:6]
        dkv_o, dck_o = refs[7 + ns:9 + ns]
        s_s, dp_s, p_s, ds_s, dk_s, dv_s, dck_s = refs[9 + 2 * ns:16 + 2 * ns]
        j, i = pl.program_id(0), pl.program_id(1)
        if ns:
            start, finish = _scatter_copies(refs[7:7 + ns], refs[9 + ns:9 + 2 * ns], *refs[16 + 2 * ns:])
            pl.when((i == 0) & (j == 0))(start)

        @pl.when(i == 0)
        def _():
            dk_s[...] = jnp.zeros_like(dk_s)
            dv_s[...] = jnp.zeros_like(dv_s)
            dck_s[...] = jnp.zeros_like(dck_s)

        def mm(h, b):
            s_s[b] = _dot(ka_ref[h], qa_ref[h], NT_DIMS)
            dp_s[b] = _dot(v_ref[h], do_ref[h], NT_DIMS)

        def step(masked):
            mm(0, 0)
            for h in range(HC):
                b = h % 2
                if h + 1 < HC:
                    mm(h + 1, 1 - b)
                for rc in range(tq // RC):
                    rows = slice(rc * RC, (rc + 1) * RC)
                    acc = None
                    for c, sc in enumerate(_pieces(s_s.at[b], rows, rc, tq, masked, False)):
                        if sc is None:
                            p_s[rows, _lanes(c)] = jnp.zeros((RC, LW), BF)
                            ds_s[rows, _lanes(c)] = jnp.zeros((RC, LW), BF)
                            continue
                        p = jnp.exp(sc - lse_ref[h:h + 1, _lanes(c)])
                        ds = p * (dp_s[b, rows, _lanes(c)] - dl_ref[h:h + 1, _lanes(c)])
                        p_s[rows, _lanes(c)] = p.astype(BF)
                        ds_s[rows, _lanes(c)] = ds.astype(BF)
                        acc = ds if acc is None else acc + ds
                    dck_s[h, rows] += acc
                dv_s[h] += _dot(p_s[...], do_ref[h])
                dk_s[h] += _dot(ds_s[...], qa_ref[h])

        @pl.when(i > j)
        def _():
            step(False)

        @pl.when(i == j)
        def _():
            step(True)

        @pl.when(i == nq - 1)
        def _():
            for h in range(HC):
                dkv_o[:, DH * h:DH * (h + 1)] = dk_s[h][:, 0:DH].astype(BF)
                dkv_o[:, 512 + DH * h:512 + DH * (h + 1)] = dv_s[h].astype(BF)
            dck_o[...] = _lane_pack([jnp.sum(dck_s[h], axis=1, keepdims=True) for h in range(HC)], tq)

        if ns:
            pl.when((i == nq - 1) & (j == nq - 1))(finish)

    def qspec(w):
        return pl.BlockSpec((HC, tq, w), lambda j, i: (0, jnp.maximum(i, j), 0))

    def kspec(w):
        return pl.BlockSpec((HC, tq, w), lambda j, i: (0, j, 0))

    rowv = pl.BlockSpec((8, tq), lambda j, i: (0, jnp.maximum(i, j)))
    anyspec = pl.BlockSpec(memory_space=pl.ANY)
    res = pl.pallas_call(
        body, name="attn_bwd_dkv_scatter" if ns else "attn_bwd_dkv", grid=(nq, nq),
        out_shape=(jax.ShapeDtypeStruct(dP.shape, BF), jax.ShapeDtypeStruct((S, 128), F32)) + _same_shapes(scatter),
        in_specs=[qspec(2 * DH), kspec(2 * DH), kspec(DH), qspec(DH), rowv, rowv, anyspec] + [anyspec] * ns,
        out_specs=(pl.BlockSpec((tq, 1024), lambda j, i: (j, C_KV // 1024)), pl.BlockSpec((tq, 128), lambda j, i: (j, 0)))
        + (anyspec,) * ns,
        scratch_shapes=[pltpu.VMEM((2, tq, tq), F32), pltpu.VMEM((2, tq, tq), F32), pltpu.VMEM((tq, tq), BF), pltpu.VMEM((tq, tq), BF),
                        pltpu.VMEM((HC, tq, 2 * DH), F32), pltpu.VMEM((HC, tq, DH), F32), pltpu.VMEM((HC, tq, LW), F32)]
        + (_comm_sems(ns) if ns else []),
        input_output_aliases={6: 0},
        compiler_params=_cp(("arbitrary", "arbitrary") if ns else ("parallel", "arbitrary")),
    )(Qa, Ka, V, dO, lseT, dltT, dP, *scatter)
    return res[0], res[1], list(res[2:])


def _attn_bwd_dq(Qa, Ka, V, dO, lse, dlt, dP, *, tq):
    S = Qa.shape[1]
    nq = S // tq

    def body(qa_ref, ka_ref, v_ref, do_ref, lse_ref, dl_ref, dp_in, dq_o, dcq_o, s_s, dp_s, ds_s, dq_s, dcq_s):
        del dp_in
        i, j = pl.program_id(0), pl.program_id(1)

        @pl.when(j == 0)
        def _():
            dq_s[...] = jnp.zeros_like(dq_s)
            dcq_s[...] = jnp.zeros_like(dcq_s)

        def mm(h, b):
            s_s[b] = _dot(qa_ref[h], ka_ref[h], NT_DIMS)
            dp_s[b] = _dot(do_ref[h], v_ref[h], NT_DIMS)

        def step(masked):
            mm(0, 0)
            for h in range(HC):
                b = h % 2
                if h + 1 < HC:
                    mm(h + 1, 1 - b)
                for rc in range(tq // RC):
                    rows = slice(rc * RC, (rc + 1) * RC)
                    lb = lse_ref[h, rows]
                    db = dl_ref[h, rows]
                    acc = None
                    for c, sc in enumerate(_pieces(s_s.at[b], rows, rc, tq, masked, True)):
                        if sc is None:
                            ds_s[rows, _lanes(c)] = jnp.zeros((RC, LW), BF)
                            continue
                        ds = jnp.exp(sc - lb) * (dp_s[b, rows, _lanes(c)] - db)
                        ds_s[rows, _lanes(c)] = ds.astype(BF)
                        acc = ds if acc is None else acc + ds
                    dcq_s[h, rows] += acc
                dq_s[h] += _dot(ds_s[...], ka_ref[h])

        @pl.when(j < i)
        def _():
            step(False)

        @pl.when(j == i)
        def _():
            step(True)
            for h in range(HC):
                dq_o[:, DH * h:DH * (h + 1)] = (dq_s[h][:, 0:DH] * SCALE).astype(BF)
            dcq_o[...] = _lane_pack([jnp.sum(dcq_s[h], axis=1, keepdims=True) for h in range(HC)], tq)

    def qspec(w):
        return pl.BlockSpec((HC, tq, w), lambda i, j: (0, i, 0))

    def kspec(w):
        return pl.BlockSpec((HC, tq, w), lambda i, j: (0, jnp.minimum(i, j), 0))

    colv = pl.BlockSpec((tq, 128), lambda i, j: (i, 0))
    return pl.pallas_call(
        body, name="attn_bwd_dq", grid=(nq, nq),
        out_shape=(jax.ShapeDtypeStruct(dP.shape, BF), jax.ShapeDtypeStruct((S, 128), F32)),
        in_specs=[qspec(2 * DH), kspec(2 * DH), kspec(DH), qspec(DH), qspec(LW), qspec(LW), pl.BlockSpec(memory_space=pl.ANY)],
        out_specs=(pl.BlockSpec((tq, 512), lambda i, j: (i, C_Q // 512)), colv),
        scratch_shapes=[pltpu.VMEM((2, tq, tq), F32), pltpu.VMEM((2, tq, tq), F32), pltpu.VMEM((tq, tq), BF),
                        pltpu.VMEM((HC, tq, 2 * DH), F32), pltpu.VMEM((HC, tq, LW), F32)],
        input_output_aliases={6: 0},
        compiler_params=_cp(("parallel", "arbitrary")),
    )(Qa, Ka, V, dO, lse, dlt, dP)


def _xattn_probs(qm_ref, kv_ref, h):
    sl = slice(DH * h, DH * (h + 1))
    s = _dot(qm_ref[:, sl].astype(BF), kv_ref[:, sl].astype(BF), NT_DIMS) * SCALE
    p = jnp.exp(s - jnp.max(s, axis=1, keepdims=True))
    return p / jnp.sum(p, axis=1, keepdims=True)


def _xattn_fwd(P, kv, *, ts):
    S = P.shape[0]

    def body(qm_ref, kv_ref, zm_o, o_s):
        for h in range(HM):
            sl = slice(DH * h, DH * (h + 1))
            p = _xattn_probs(qm_ref, kv_ref, h)
            o_s[:, sl] = _dot(p.astype(BF), kv_ref[:, ML + DH * h:ML + DH * (h + 1)].astype(BF))
        mg = qm_ref[:, 256:512]
        zm_o[...] = (o_s[...] * mg * _sig(mg)).astype(BF)

    return pl.pallas_call(
        body, name="xattn_fwd", grid=(S // ts,),
        out_shape=jax.ShapeDtypeStruct((S, 256), BF),
        in_specs=[pl.BlockSpec((ts, 512), lambda i: (i, C_M // 512)), pl.BlockSpec((ML, 512), lambda i: (0, 0))],
        out_specs=pl.BlockSpec((ts, 256), lambda i: (i, 0)),
        scratch_shapes=[pltpu.VMEM((ts, 256), F32)],
        compiler_params=_cp(("parallel",)),
    )(P, kv)


def _xattn_bwd(P, kv, dzm, dP, *, ts):
    S = P.shape[0]

    def body(qm_ref, kv_ref, dz_ref, dp_in, dqm_o, dkv_o):
        del dp_in
        i = pl.program_id(0)

        @pl.when(i == 0)
        def _():
            dkv_o[...] = jnp.zeros_like(dkv_o)

        mg = qm_ref[:, 256:512]
        sg = _sig(mg)
        for h in range(HM):
            sl = slice(DH * h, DH * (h + 1))
            vsl = slice(ML + DH * h, ML + DH * (h + 1))
            p = _xattn_probs(qm_ref, kv_ref, h)
            pb = p.astype(BF)
            vh = kv_ref[:, vsl].astype(BF)
            o = _dot(pb, vh)
            dz = dz_ref[:, sl]
            do = dz * mg[:, sl] * sg[:, sl]
            dqm_o[:, 256 + DH * h:256 + DH * (h + 1)] = (dz * o * _dsilu(mg[:, sl], sg[:, sl])).astype(BF)
            dob = do.astype(BF)
            dpv = _dot(dob, vh, NT_DIMS)
            ds = p * (dpv - jnp.sum(do * o, axis=1, keepdims=True))
            dsb = ds.astype(BF)
            dqm_o[:, sl] = (_dot(dsb, kv_ref[:, sl].astype(BF)) * SCALE).astype(BF)
            dkv_o[:, sl] += _dot(dsb, qm_ref[:, sl].astype(BF), TN_DIMS) * SCALE
            dkv_o[:, vsl] += _dot(pb, dob, TN_DIMS)

    return pl.pallas_call(
        body, name="xattn_bwd", grid=(S // ts,),
        out_shape=(jax.ShapeDtypeStruct(dP.shape, BF), jax.ShapeDtypeStruct((ML, 512), F32)),
        in_specs=[pl.BlockSpec((ts, 512), lambda i: (i, C_M // 512)), pl.BlockSpec((ML, 512), lambda i: (0, 0)),
                  pl.BlockSpec((ts, 256), lambda i: (i, 0)), pl.BlockSpec(memory_space=pl.ANY)],
        out_specs=(pl.BlockSpec((ts, 512), lambda i: (i, C_M // 512)), pl.BlockSpec((ML, 512), lambda i: (0, 0))),
        input_output_aliases={3: 0},
        compiler_params=_cp(("arbitrary",)),
    )(P, kv, dzm, dP)


def _merge_fwd(za, zb, zc, zm, P, x, pa, pb, pc, pm, wo, lng, lnb, *, ts):
    S = x.shape[0]

    def body(za_r, zb_r, zc_r, zm_r, g_r, x_r, pa_r, pb_r, pc_r, pm_r, wo_r, lng_r, lnb_r,
             mg_o, ya_o, yb_o, yc_o, ym_o, xn_o, xh_o, rs_o):
        merged = jnp.zeros((ts, D), F32)
        for t, (z_r, p_r, y_o) in enumerate(((za_r, pa_r, ya_o), (zb_r, pb_r, yb_o), (zc_r, pc_r, yc_o), (zm_r, pm_r, ym_o))):
            y = _dot(z_r[...], p_r[...])
            merged = merged + _sig(g_r[:, D * t:D * (t + 1)]) * y
            y_o[...] = y.astype(BF)
        mb = merged.astype(BF)
        mg_o[...] = mb
        r = ALPHA * x_r[...] + _dot(mb, wo_r[...])
        xh, rstd = _ln_hat(r)
        xh_o[...] = xh
        rs_o[...] = rstd
        xn_o[...] = xh * lng_r[...] + lnb_r[...]

    def rows(w):
        return pl.BlockSpec((ts, w), lambda i: (i, 0))

    def full(a):
        return pl.BlockSpec(a.shape, lambda i: (0, 0))

    sd = lambda dt: jax.ShapeDtypeStruct((S, D), dt)
    return pl.pallas_call(
        body, name="merge_fwd", grid=(S // ts,),
        out_shape=(sd(BF), sd(BF), sd(BF), sd(BF), sd(BF), sd(F32), sd(F32), jax.ShapeDtypeStruct((S, 1), F32)),
        in_specs=[rows(512), rows(512), rows(512), rows(256), pl.BlockSpec((ts, 4 * D), lambda i: (i, 0)), rows(D),
                  full(pa), full(pb), full(pc), full(pm), full(wo), full(lng), full(lnb)],
        out_specs=(rows(D),) * 7 + (rows(1),),
        compiler_params=_cp(("parallel",)),
    )(za, zb, zc, zm, P, x, pa, pb, pc, pm, wo, lng, lnb)


def _loss_fwd(y, tgt, *, ts):
    S = y.shape[0]

    def body(y_r, t_r, dy_o, l_o):
        @pl.when(pl.program_id(0) == 0)
        def _():
            l_o[...] = jnp.zeros_like(l_o)

        e = y_r[...] - t_r[...]
        dy_o[...] = e / D
        l_o[...] += 0.5 * jnp.sum(_sum_r(e * e), axis=1, keepdims=True) / D

    rows = pl.BlockSpec((ts, D), lambda i: (i, 0))
    return pl.pallas_call(
        body, name="loss", grid=(S // ts,),
        out_shape=(jax.ShapeDtypeStruct((S, D), F32), jax.ShapeDtypeStruct((1, 1), F32)),
        in_specs=[rows, rows], out_specs=(rows, pl.BlockSpec((1, 1), lambda i: (0, 0))),
        compiler_params=_cp(("arbitrary",)),
    )(y, tgt)


def _out_bwd(dxn, xh, rstd, merged, wo, lng, *, ts):
    S = dxn.shape[0]

    def body(dxn_r, xh_r, rs_r, mg_r, wo_r, lng_r, dr_o, dm_o, dwo_o, dlng_o, dlnb_o):
        @pl.when(pl.program_id(0) == 0)
        def _():
            dwo_o[...] = jnp.zeros_like(dwo_o)
            dlng_o[...] = jnp.zeros_like(dlng_o)
            dlnb_o[...] = jnp.zeros_like(dlnb_o)

        dxn = dxn_r[...]
        xh = xh_r[...]
        dr = _ln_bwd(dxn * lng_r[...], xh, rs_r[...])
        dr_o[...] = dr
        drb = dr.astype(BF)
        dm_o[...] = _dot(drb, wo_r[...], NT_DIMS)
        dwo_o[...] += _dot(mg_r[...], drb, TN_DIMS)
        dlng_o[...] += _sum_r(dxn * xh)
        dlnb_o[...] += _sum_r(dxn)

    rows = pl.BlockSpec((ts, D), lambda i: (i, 0))
    full = lambda shape: pl.BlockSpec(shape, lambda i: (0, 0))
    sd = jax.ShapeDtypeStruct((S, D), F32)
    vec = jax.ShapeDtypeStruct((1, D), F32)
    return pl.pallas_call(
        body, name="out_bwd", grid=(S // ts,),
        out_shape=(sd, sd, jax.ShapeDtypeStruct((D, D), F32), vec, vec),
        in_specs=[rows, rows, pl.BlockSpec((ts, 1), lambda i: (i, 0)), rows, full((D, D)), full((1, D))],
        out_specs=(rows, rows, full((D, D)), full((1, D)), full((1, D))),
        compiler_params=_cp(("arbitrary",)),
    )(dxn, xh, rstd, merged, wo, lng)


def _merge_bwd(dm, P, ya, yb, yc, ym, za, zb, zc, zm, pa, pb, pc, pm, *, ts):
    S = dm.shape[0]

    def body(dm_r, g_r, ya_r, yb_r, yc_r, ym_r, za_r, zb_r, zc_r, zm_r, pa_r, pb_r, pc_r, pm_r,
             dg_o, dza_o, dzb_o, dzc_o, dzm_o, dpa_o, dpb_o, dpc_o, dpm_o):
        @pl.when(pl.program_id(0) == 0)
        def _():
            for o in (dpa_o, dpb_o, dpc_o, dpm_o):
                o[...] = jnp.zeros_like(o)

        dm = dm_r[...]
        for t, (y_r, z_r, p_r, dz_o, dp_o) in enumerate(((ya_r, za_r, pa_r, dza_o, dpa_o), (yb_r, zb_r, pb_r, dzb_o, dpb_o),
                                                        (yc_r, zc_r, pc_r, dzc_o, dpc_o), (ym_r, zm_r, pm_r, dzm_o, dpm_o))):
            gate = _sig(g_r[:, D * t:D * (t + 1)])
            dg_o[:, D * t:D * (t + 1)] = (dm * y_r[...].astype(F32) * gate * (1.0 - gate)).astype(BF)
            dyb = (dm * gate).astype(BF)
            dz_o[...] = _dot(dyb, p_r[...], NT_DIMS)
            dp_o[...] += _dot(z_r[...], dyb, TN_DIMS)

    def rows(w):
        return pl.BlockSpec((ts, w), lambda i: (i, 0))

    def full(a):
        return pl.BlockSpec(a.shape, lambda i: (0, 0))

    return pl.pallas_call(
        body, name="merge_bwd", grid=(S // ts,),
        out_shape=(jax.ShapeDtypeStruct((S, NP), BF),
                   jax.ShapeDtypeStruct((S, 512), F32), jax.ShapeDtypeStruct((S, 512), F32),
                   jax.ShapeDtypeStruct((S, 512), F32), jax.ShapeDtypeStruct((S, 256), F32),
                   jax.ShapeDtypeStruct(pa.shape, F32), jax.ShapeDtypeStruct(pb.shape, F32),
                   jax.ShapeDtypeStruct(pc.shape, F32), jax.ShapeDtypeStruct(pm.shape, F32)),
        in_specs=[rows(D), pl.BlockSpec((ts, 4 * D), lambda i: (i, 0)), rows(D), rows(D), rows(D), rows(D),
                  rows(512), rows(512), rows(512), rows(256), full(pa), full(pb), full(pc), full(pm)],
        out_specs=(pl.BlockSpec((ts, 4 * D), lambda i: (i, 0)), rows(512), rows(512), rows(512), rows(256),
                   full(pa), full(pb), full(pc), full(pm)),
        compiler_params=_cp(("arbitrary",)),
    )(dm, P, ya, yb, yc, ym, za, zb, zc, zm, pa, pb, pc, pm)


def _branch_bwd(P, ca, cb, u, vb, dza, dzb, dzc, oc, wA, gA, betaA, wB, dP, *, ts):
    S = P.shape[0]
    nt = S // ts

    def rev(i):
        return nt - 1 - i

    def rows(w):
        return pl.BlockSpec((ts, w), lambda i: (rev(i), 0))

    def halo(rows_):
        return pl.BlockSpec((rows_, 512), lambda i: (jnp.maximum(rev(i) * (ts // rows_) - 1, 0), 0))

    def full(shape):
        return pl.BlockSpec(shape, lambda i: (0, 0))

    def body(pg, ca_r, cb_r, u_r, vb_r, uh_r, vh_r, dza_r, dzb_r, dzc_r, oc_r, wA_r, gA_r, betaA_r, wB_r, dp_in,
             dpg_o, do_o, dl_o, dlc_o, dwA_o, dbA_o, dgA_o, dbetaA_o, dwB_o, dwinA, uwin, haloA, dwinB, vwin, haloB, cv_s, dwpA, dwpB):
        del dp_in
        i = pl.program_id(0)
        nz = (rev(i) > 0).astype(F32)

        @pl.when(i == 0)
        def _():
            for o in (dwA_o, dbA_o, dgA_o, dbetaA_o, dwB_o, haloA, haloB, dwpA, dwpB):
                o[...] = jnp.zeros_like(o)

        def col(j):
            return pg[:, 512 * j:512 * (j + 1)]

        def put(j, val):
            dpg_o[:, 512 * j:512 * (j + 1)] = val.astype(BF)

        a_gate = col(2)
        xh, rstd = _ln_hat(ca_r[...])
        gA_v = gA_r[...]
        n = xh * gA_v + betaA_r[...]
        sn = _sig(n)
        a = n * sn
        sg = _sig(a_gate)
        dza = dza_r[...]
        put(2, dza * a * _dsilu(a_gate, sg))
        dn = dza * a_gate * sg * _dsilu(n, sn)
        dgA_o[...] += _sum_r(dn * xh)
        dbetaA_o[...] += _sum_r(dn)
        dca = _ln_bwd(dn * gA_v, xh, rstd)
        dbA_o[...] += _sum_r(dca)
        dwinA[0:ts, :] = dca
        dwinA[ts:, :] = haloA[...]
        haloA[...] = dca[0:HALO_A, :]
        uwin[0:HALO_A, :] = uh_r[...] * nz
        uwin[HALO_A:, :] = u_r[...]
        _dwcorr_acc(dwpA, dwinA.at[pl.ds(0, ts)], uwin, KA, HALO_A - KA + 1, ts)
        _dwconv(cv_s, dwinA, wA_r, KA, 0, ts, reverse=True)
        du = cv_s[...]
        sv = _sig(col(1))
        put(0, du * sv)
        put(1, du * col(0) * sv * (1.0 - sv))

        b_gate = col(6)
        sgb = _sig(b_gate)
        cbv = cb_r[...]
        b_b = col(4)
        dzb = dzb_r[...]
        put(6, dzb * b_b * cbv * _dsilu(b_gate, sgb))
        dhb = dzb * b_gate * sgb
        put(4, dhb * cbv)
        dcb = dhb * b_b
        dwinB[0:ts, :] = dcb
        dwinB[ts:, :] = haloB[...]
        haloB[...] = dcb[0:HALO_B, :]
        vwin[0:HALO_B, :] = vh_r[...] * nz
        vwin[HALO_B:, :] = vb_r[...]
        _dwcorr_acc(dwpB, dwinB.at[pl.ds(0, ts)], vwin, KB, HALO_B - KB + 1, ts)
        _dwconv(cv_s, dwinB, wB_r, KB, 0, ts, reverse=True)
        dv = cv_s[...]
        put(5, dv * col(3))
        put(3, dv * col(5))

        c_gate = col(7)
        sgc = _sig(c_gate)
        dzc = dzc_r[...]
        ocv = oc_r[...]
        put(7, dzc * ocv * _dsilu(c_gate, sgc))
        do = dzc * c_gate * sgc
        for h in range(HC):
            do_o[h] = do[:, DH * h:DH * (h + 1)].astype(BF)
        dd = do * ocv
        dls = [jnp.sum(dd[:, DH * h:DH * (h + 1)], axis=1, keepdims=True) for h in range(HC)]
        for h in range(HC):
            dl_o[h] = jnp.broadcast_to(dls[h], (ts, LW))
        dlc_o[...] = _lane_pack(dls, ts)

        @pl.when(i == nt - 1)
        def _():
            for k in range(KA):
                dwA_o[k:k + 1, :] = _sum_r(dwpA[8 * k:8 * k + 8, :])
            for k in range(KB):
                dwB_o[k:k + 1, :] = _sum_r(dwpB[8 * k:8 * k + 8, :])

    v512 = jax.ShapeDtypeStruct((1, 512), F32)
    return pl.pallas_call(
        body, name="branch_bwd", grid=(nt,),
        out_shape=(jax.ShapeDtypeStruct(dP.shape, BF), jax.ShapeDtypeStruct((HC, S, DH), BF), jax.ShapeDtypeStruct((HC, S, LW), F32),
                   jax.ShapeDtypeStruct((S, LW), F32),
                   jax.ShapeDtypeStruct((32, 512), F32), v512, v512, v512, jax.ShapeDtypeStruct((8, 512), F32)),
        in_specs=[pl.BlockSpec((ts, 4096), lambda i: (rev(i), C_AB // 4096)),
                  rows(512), rows(512), rows(512), rows(512), halo(HALO_A), halo(HALO_B),
                  rows(512), rows(512), rows(512), rows(512),
                  full((32, 512)), full((1, 512)), full((1, 512)), full((8, 512)), pl.BlockSpec(memory_space=pl.ANY)],
        out_specs=(pl.BlockSpec((ts, 4096), lambda i: (rev(i), C_AB // 4096)),
                   pl.BlockSpec((HC, ts, DH), lambda i: (0, rev(i), 0)), pl.BlockSpec((HC, ts, LW), lambda i: (0, rev(i), 0)),
                   rows(LW), full((32, 512)), full((1, 512)), full((1, 512)), full((1, 512)), full((8, 512))),
        scratch_shapes=[pltpu.VMEM((ts + HALO_A, 512), F32), pltpu.VMEM((ts + HALO_A, 512), F32), pltpu.VMEM((HALO_A, 512), F32),
                        pltpu.VMEM((ts + HALO_B, 512), F32), pltpu.VMEM((ts + HALO_B, 512), F32), pltpu.VMEM((HALO_B, 512), F32),
                        pltpu.VMEM((ts, 512), F32), pltpu.VMEM((8 * 32, 512), F32), pltpu.VMEM((8 * 8, 512), F32)],
        input_output_aliases={15: 0},
        compiler_params=_cp(("arbitrary",)),
    )(P, ca, cb, u, vb, u, vb, dza, dzb, dzc, oc, wA, gA, betaA, wB, dP)


def _cum_bwd(P, dcum, bfg, dP, *, ts):
    S = P.shape[0]
    nt = S // ts

    def body(f_ref, dc_ref, bf_r, dp_in, df_o, dbf_o, carry):
        del dp_in
        i = pl.program_id(0)

        @pl.when(i == 0)
        def _():
            carry[...] = jnp.zeros_like(carry)
            dbf_o[...] = jnp.zeros_like(dbf_o)

        r = lax.broadcasted_iota(jnp.int32, (ts, ts), 0)
        c = lax.broadcasted_iota(jnp.int32, (ts, ts), 1)
        tri = (r <= c).astype(F32)
        dlogf = jnp.dot(tri, dc_ref[...], precision=HIGHEST, preferred_element_type=F32) + carry[...]
        carry[...] = dlogf[0:1, :]
        x = f_ref[...] + bf_r[...]
        lane = lax.broadcasted_iota(jnp.int32, (ts, 128), 1)
        df = jnp.where(lane < HC, dlogf * _sig(-x), 0.0)
        df_o[...] = df.astype(BF)
        dbf_o[...] += _sum_r(df)

    blk = pl.BlockSpec((ts, 128), lambda i: (nt - 1 - i, C_F // 128))
    return pl.pallas_call(
        body, name="cum_bwd", grid=(nt,),
        out_shape=(jax.ShapeDtypeStruct(dP.shape, BF), jax.ShapeDtypeStruct((1, 128), F32)),
        in_specs=[blk, pl.BlockSpec((ts, 128), lambda i: (nt - 1 - i, 0)), pl.BlockSpec((1, 128), lambda i: (0, 0)),
                  pl.BlockSpec(memory_space=pl.ANY)],
        out_specs=(blk, pl.BlockSpec((1, 128), lambda i: (0, 0))),
        scratch_shapes=[pltpu.VMEM((1, 128), F32)],
        input_output_aliases={3: 0},
        compiler_params=_cp(("arbitrary",)),
    )(P, dcum, bfg, dP)


def _adamw(w, m, v, gparts, *, name, tr):
    rws, cols = w.shape
    tr = min(tr, rws)
    assert rws % tr == 0 and gparts.shape == (NDEV, rws, cols), (name, w.shape, gparts.shape)
    c1 = 1.0 - ADAM_B1 ** ADAM_STEP
    c2 = 1.0 - ADAM_B2 ** ADAM_STEP

    def body(w_r, m_r, v_r, g_r, g_o, d_o, m_o, v_o):
        g = g_r[0].astype(F32)
        for p in range(1, NDEV):
            g = g + g_r[p].astype(F32)
        mn = ADAM_B1 * m_r[...] + (1.0 - ADAM_B1) * g
        vn = ADAM_B2 * v_r[...] + (1.0 - ADAM_B2) * (g * g)
        g_o[...] = g
        m_o[...] = mn
        v_o[...] = vn
        d_o[...] = -ADAM_LR * ((mn / c1) / (jnp.sqrt(vn / c2) + ADAM_EPS) + ADAM_WD * w_r[...])

    blk = pl.BlockSpec((tr, cols), lambda i: (i, 0))
    shp = jax.ShapeDtypeStruct((rws, cols), F32)
    return pl.pallas_call(
        body, name=name, grid=(rws // tr,), out_shape=(shp,) * 4,
        in_specs=[blk, blk, blk, pl.BlockSpec((NDEV, tr, cols), lambda i: (0, i, 0))],
        out_specs=(blk,) * 4, compiler_params=_cp(("parallel",)),
    )(w, m, v, gparts)


def _slot(p):
    return 4 * p[0] + 2 * p[1] + p[2]


def _comm_sems(na):
    return [pltpu.SemaphoreType.DMA((na, 7)), pltpu.SemaphoreType.DMA((na, 7)), pltpu.SemaphoreType.DMA((na,))]


def _gather_copies(ins, outs, send_sems, recv_sems, local_sems):
    na = len(ins)
    x, y, c = lax.axis_index("x"), lax.axis_index("y"), lax.axis_index("c")
    me, sib = (x, y, c), (x, y, 1 - c)
    chips = [(1 - x, y), (x, 1 - y), (1 - x, 1 - y)]

    def cp(a, k, block, to, src=None):
        dst = outs[a].at[_slot(block)]
        return pltpu.make_async_remote_copy(src_ref=dst if src is None else src, dst_ref=dst,
                                            send_sem=send_sems.at[a, k], recv_sem=recv_sems.at[a, k],
                                            device_id=to, device_id_type=pl.DeviceIdType.MESH)

    def mine(a):
        return pltpu.make_async_copy(ins[a], outs[a].at[_slot(me)], local_sems.at[a])

    def first(a):
        return [cp(a, 0, me, sib, src=ins[a])] + [cp(a, 1 + j, me, (*chip, c), src=ins[a]) for j, chip in enumerate(chips)]

    def start():
        for a in range(na):
            mine(a).start()
            for f in first(a):
                f.start()

    def finish():
        for j, chip in enumerate(chips):
            for a in range(na):
                cp(a, 1 + j, (*chip, c), me).wait_recv()
                cp(a, 4 + j, (*chip, c), sib).start()
        for a in range(na):
            cp(a, 0, sib, me).wait_recv()
            for j, chip in enumerate(chips):
                cp(a, 4 + j, (*chip, 1 - c), me).wait_recv()
        for a in range(na):
            for f in first(a):
                f.wait_send()
            for j, chip in enumerate(chips):
                cp(a, 4 + j, (*chip, c), sib).wait_send()
            mine(a).wait()

    return start, finish


def _scatter_copies(ins, outs, send_sems, recv_sems, local_sems):
    na = len(ins)
    x, y, c = lax.axis_index("x"), lax.axis_index("y"), lax.axis_index("c")
    me = (x, y, c)
    peers = [(x ^ ((k >> 2) & 1), y ^ ((k >> 1) & 1), c ^ (k & 1)) for k in range(1, NDEV)]

    def cp(a, k, peer):
        return pltpu.make_async_remote_copy(src_ref=ins[a].at[_slot(peer)], dst_ref=outs[a].at[_slot(me)],
                                            send_sem=send_sems.at[a, k], recv_sem=recv_sems.at[a, k],
                                            device_id=peer, device_id_type=pl.DeviceIdType.MESH)

    def landed(a, k, peer):
        dst = outs[a].at[_slot(peer)]
        return pltpu.make_async_remote_copy(src_ref=dst, dst_ref=dst, send_sem=send_sems.at[a, k], recv_sem=recv_sems.at[a, k],
                                            device_id=peer, device_id_type=pl.DeviceIdType.MESH)

    def mine(a):
        return pltpu.make_async_copy(ins[a].at[_slot(me)], outs[a].at[_slot(me)], local_sems.at[a])

    def start():
        for a in range(na):
            mine(a).start()
            for k, peer in enumerate(peers):
                cp(a, k, peer).start()

    def finish():
        for a in range(na):
            for k, peer in enumerate(peers):
                landed(a, k, peer).wait_recv()
        for a in range(na):
            for k, peer in enumerate(peers):
                cp(a, k, peer).wait_send()
            mine(a).wait()

    return start, finish


def _comm_call(arrs, copies, out_shapes, *, name):
    na = len(arrs)

    def body(*refs):
        start, finish = copies(refs[:na], refs[na:2 * na], *refs[2 * na:])
        start()
        finish()

    anyspec = pl.BlockSpec(memory_space=pl.ANY)
    return pl.pallas_call(body, name=name, out_shape=out_shapes, in_specs=[anyspec] * na, out_specs=(anyspec,) * na,
                          scratch_shapes=_comm_sems(na))(*arrs)


def _gathered_shapes(arrs):
    return tuple(jax.ShapeDtypeStruct((NDEV,) + a.shape, a.dtype) for a in arrs)


def _same_shapes(arrs):
    return tuple(jax.ShapeDtypeStruct(a.shape, a.dtype) for a in arrs)


def _all_gather(arrs, *, name):
    return _comm_call(arrs, _gather_copies, _gathered_shapes(arrs), name=name)


def _gathered_to_layout(g4):
    parts = []
    for a, b in _RUNS:
        for d in range(a // SHARD_IN, (b - 1) // SHARD_IN + 1):
            lo, hi = max(a, d * SHARD_IN), min(b, (d + 1) * SHARD_IN)
            parts.append(g4[d, ..., lo - d * SHARD_IN:hi - d * SHARD_IN])
    parts.append(jnp.zeros(g4.shape[1:-1] + (NP - IN_COLS,), g4.dtype))
    return jnp.concatenate(parts, axis=-1)


def _layout_to_shards(w):
    offs, off = {}, 0
    for a, b in _RUNS:
        offs[a] = (b, off)
        off += b - a
    shards = []
    for d in range(NDEV):
        parts = []
        for a in sorted(offs):
            b, off = offs[a]
            lo, hi = max(a, d * SHARD_IN), min(b, (d + 1) * SHARD_IN)
            if lo < hi:
                parts.append(w[..., off + lo - a:off + hi - a])
        shards.append(jnp.concatenate(parts, axis=-1))
    return jnp.stack(shards)


def _tiles(S):
    ts = min(256, S)
    tsb = min(128, S)
    tq = min(512, S)
    return ts, tsb, tq


def _layer_fwd(x, mem_n, w, gather=()):
    ts, _, tq = _tiles(x.shape[0])
    P = _mm(x, w["W"], name="proj_fwd", tm=1024, tn=1152, tk=D)
    kv = _mm(mem_n, w["w_kv_mem"], name="kv_fwd", tm=ML, tn=512, tk=D)
    za, zb, u, ca, vb, cb, cum = _pre_fwd(P, w["wA"], w["conv_a_b"], w["ln_a_g"], w["ln_a_b"], w["wB"], w["b_forget"], ts=ts)
    Qa, Ka, V = _attn_prep(P, cum, ts=ts)
    oc, zc, lse, lse_c, gathered = _attn_fwd(P, Qa, Ka, V, tq=tq, gather=gather)
    zm = _xattn_fwd(P, kv, ts=tq)
    merged, ya, yb, yc, ym, xn, xh, rstd = _merge_fwd(za, zb, zc, zm, P, x, w["p_a"], w["p_b"], w["p_c"], w["p_m"], w["w_out"],
                                                      w["ln_g"], w["ln_b"], ts=ts)
    saved = (x, P, kv, za, zb, zc, zm, u, ca, vb, cb, Qa, Ka, V, oc, lse, lse_c, merged, ya, yb, yc, ym, xh, rstd)
    return xn, saved, gathered


def _layer_bwd(dx, saved, mem_n, w, scatter=(), own_chunks=None):
    (xl, P, kv, za, zb, zc, zm, u, ca, vb, cb, Qa, Ka, V, oc, lse, lse_c, merged, ya, yb, yc, ym, xh, rstd) = saved
    ts, tsb, tq = _tiles(xl.shape[0])
    g = {}
    dr, dm, g["w_out"], g["ln_g"], g["ln_b"] = _out_bwd(dx, xh, rstd, merged, w["w_out"], w["ln_g"], ts=ts)
    dP, dza, dzb, dzc, dzm, g["p_a"], g["p_b"], g["p_c"], g["p_m"] = _merge_bwd(
        dm, P, ya, yb, yc, ym, za, zb, zc, zm, w["p_a"], w["p_b"], w["p_c"], w["p_m"], ts=tsb)
    dP, do, dlt, dlt_c, dwA, g["conv_a_b"], g["ln_a_g"], g["ln_a_b"], dwB = _branch_bwd(
        P, ca, cb, u, vb, dza, dzb, dzc, oc, w["wA"], w["ln_a_g"], w["ln_a_b"], w["wB"], dP, ts=ts)
    g["conv_a_w"], g["conv_b_w"] = dwA[:KA], dwB[:KB]
    dP, dck, received = _attn_bwd_dkv(Qa, Ka, V, do, lse_c[:, :HC].T, dlt_c[:, :HC].T, dP, tq=tq, scatter=scatter)
    dP, dcq = _attn_bwd_dq(Qa, Ka, V, do, lse, dlt, dP, tq=tq)
    dP, dbf = _cum_bwd(P, dcq - dck, w["b_forget"], dP, ts=ts)
    g["b_forget"] = dbf[0, :HC]
    dP, dkv = _xattn_bwd(P, kv, dzm, dP, ts=tq)
    g["w_kv_mem"] = _mm(mem_n.T, dkv, name="wkv_bwd", tm=D, tn=512, tk=ML)
    dmem_n = _mm(dkv, w["w_kv_mem"], name="memn_bwd", nt=True, tm=ML, tn=D, tk=512)
    g["w_in"] = _mm(xl.T.astype(BF), dP, name="win_bwd", out_dtype=BF, tm=D, tn=1152, tk=1024)
    if own_chunks is None:
        dx = _mm(dP, w["W"], name="x_bwd", nt=True, tm=1024, tn=D, tk=1152, add=dr, add_scale=ALPHA)
        return dx, g, dmem_n, received, None
    dx, received_own = _mm(dP, w["W"], name="x_bwd_scatter", nt=True, tm=1024, tn=D, tk=1152, add=dr, add_scale=ALPHA,
                           scatter=own_chunks(g))
    return dx, g, dmem_n, received, received_own


_SMALL = (("b_forget", (NL, HC)), ("conv_a_b", (NL, 512)), ("ln_a_g", (NL, 512)), ("ln_a_b", (NL, 512)),
          ("mem_ln_g", (D,)), ("mem_ln_b", (D,)), ("ln_g", (NL, D)), ("ln_b", (NL, D)),
          ("conv_a_w", (NL, KA, 512)), ("conv_b_w", (NL, KB, 512)))


def _pack(parts, rows_mult=8):
    flat = jnp.concatenate([p.reshape(-1).astype(F32) for p in parts])
    n = flat.shape[0]
    rows = -(-n // 128)
    rows = -(-rows // rows_mult) * rows_mult
    return jnp.pad(flat, (0, rows * 128 - n)).reshape(rows, 128)


def _unpack(buf, shapes):
    flat = buf.reshape(-1)
    out, off = [], 0
    for shp in shapes:
        n = 1
        for d in shp:
            n *= d
        out.append(flat[off:off + n].reshape(shp))
        off += n
    return out


def kernel(x, mem, w_in, b_forget, conv_a_w, conv_a_b, ln_a_g, ln_a_b, conv_b_w, w_kv_mem, mem_ln_g, mem_ln_b, p_a, p_b, p_c, p_m, w_out, ln_g, ln_b, loss_target, m_w_in, m_b_forget, m_conv_a_w, m_conv_a_b, m_ln_a_g, m_ln_a_b, m_conv_b_w, m_w_kv_mem, m_mem_ln_g, m_mem_ln_b, m_p_a, m_p_b, m_p_c, m_p_m, m_w_out, m_ln_g, m_ln_b, v_w_in, v_b_forget, v_conv_a_w, v_conv_a_b, v_ln_a_g, v_ln_a_b, v_conv_b_w, v_w_kv_mem, v_mem_ln_g, v_mem_ln_b, v_p_a, v_p_b, v_p_c, v_p_m, v_w_out, v_ln_g, v_ln_b):
    wts = dict(w_in=w_in, b_forget=b_forget, conv_a_w=conv_a_w, conv_a_b=conv_a_b, ln_a_g=ln_a_g, ln_a_b=ln_a_b, conv_b_w=conv_b_w,
               w_kv_mem=w_kv_mem, mem_ln_g=mem_ln_g, mem_ln_b=mem_ln_b, p_a=p_a, p_b=p_b, p_c=p_c, p_m=p_m, w_out=w_out, ln_g=ln_g, ln_b=ln_b)
    mom = dict(w_in=m_w_in, b_forget=m_b_forget, conv_a_w=m_conv_a_w, conv_a_b=m_conv_a_b, ln_a_g=m_ln_a_g, ln_a_b=m_ln_a_b,
               conv_b_w=m_conv_b_w, w_kv_mem=m_w_kv_mem, mem_ln_g=m_mem_ln_g, mem_ln_b=m_mem_ln_b, p_a=m_p_a, p_b=m_p_b, p_c=m_p_c,
               p_m=m_p_m, w_out=m_w_out, ln_g=m_ln_g, ln_b=m_ln_b)
    vel = dict(w_in=v_w_in, b_forget=v_b_forget, conv_a_w=v_conv_a_w, conv_a_b=v_conv_a_b, ln_a_g=v_ln_a_g, ln_a_b=v_ln_a_b,
               conv_b_w=v_conv_b_w, w_kv_mem=v_w_kv_mem, mem_ln_g=v_mem_ln_g, mem_ln_b=v_mem_ln_b, p_a=v_p_a, p_b=v_p_b, p_c=v_p_c,
               p_m=v_p_m, w_out=v_w_out, ln_g=v_ln_g, ln_b=v_ln_b)
    names = ("w_in", "b_forget", "conv_a_w", "conv_a_b", "ln_a_g", "ln_a_b", "conv_b_w", "w_kv_mem", "mem_ln_g", "mem_ln_b",
             "p_a", "p_b", "p_c", "p_m", "w_out", "ln_g", "ln_b")
    mid = ("p_a", "p_b", "p_c", "p_m", "w_out", "w_kv_mem")
    me = 4 * lax.axis_index("x") + 2 * lax.axis_index("y") + lax.axis_index("c")

    row_sharded = ("w_out", "w_kv_mem")
    mid_shapes = [wts[n].shape[1:] for n in mid]
    mid_nrows = [s[0] * s[1] // 128 for s in mid_shapes]

    def mid_pack(d, l):
        return jnp.concatenate([d[n][l].reshape(-1, 128) for n in mid], axis=0)

    def layer_weights(l, g_win, g16, wA, wB):
        w = {"W": _gathered_to_layout(g_win), "wA": wA[l], "wB": wB[l]}
        off = 0
        for n, shp, nr in zip(mid, mid_shapes, mid_nrows):
            blk = g16[:, off:off + nr].reshape((NDEV,) + shp)
            off += nr
            w[n] = blk.reshape(NDEV * shp[0], shp[1]) if n in row_sharded else blk.transpose(1, 0, 2).reshape(shp[0], NDEV * shp[1])
        w["b_forget"] = jnp.pad(b_forget[l], (0, 128 - HC)).reshape(1, 128)
        for n, a in (("conv_a_b", conv_a_b), ("ln_a_g", ln_a_g), ("ln_a_b", ln_a_b), ("ln_g", ln_g), ("ln_b", ln_b)):
            w[n] = a[l].reshape(1, -1)
        return w

    def grad_chunks(g):
        parts = []
        for n, shp in zip(mid, mid_shapes):
            a = g[n]
            a = a.reshape(NDEV, shp[0], shp[1]) if n in row_sharded else a.reshape(shp[0], NDEV, shp[1]).transpose(1, 0, 2)
            parts.append(a.reshape(NDEV, -1, 128))
        return [_layout_to_shards(g["w_in"]), jnp.concatenate(parts, axis=1).astype(BF)]

    shards = [[w_in[l].astype(BF), mid_pack(wts, l).astype(BF)] for l in range(NL)]
    pk32 = jnp.concatenate([conv_a_w, conv_b_w], axis=1).reshape(NL * (KA + KB), 512 // NDEV)
    g_win, g16, g32 = _all_gather(shards[0] + [pk32], name="gather_first")
    conv = g32.reshape(NDEV, NL, KA + KB, 512 // NDEV).transpose(1, 2, 0, 3).reshape(NL, KA + KB, 512)
    wA = jnp.pad(conv[:, :KA], ((0, 0), (0, 32 - KA), (0, 0)))
    wB = jnp.pad(conv[:, KA:], ((0, 0), (0, 8 - KB), (0, 0)))

    mem_n, mem_hat = _mem_ln_fwd(mem[0], mem_ln_g.reshape(1, D), mem_ln_b.reshape(1, D))
    xl, lw, saved = x[0], [], []
    for l in range(NL):
        lw.append(layer_weights(l, g_win, g16, wA, wB))
        xl, sv, got = _layer_fwd(xl, mem_n, lw[l], gather=shards[l + 1] if l + 1 < NL else ())
        saved.append(sv)
        if got:
            g_win, g16 = got
    dx, loss = _loss_fwd(xl, loss_target[0], ts=_tiles(xl.shape[0])[0])
    loss = lax.psum(loss[0, 0], ("x", "y", "c"))

    g = [None] * NL
    dmem_n = [None] * NL
    recv = [None] * NL
    pending = ()
    for l in reversed(range(NL)):
        dx, g[l], dmem_n[l], got, recv[l] = _layer_bwd(dx, saved[l], mem_n, lw[l], scatter=pending,
                                                       own_chunks=grad_chunks if l == 0 else None)
        if got:
            recv[l + 1] = got
        pending = grad_chunks(g[l]) if l else ()
    gs = {n: jnp.stack([g[l][n].reshape(shp[1:]) for l in range(NL)]) for n, shp in _SMALL if len(shp) > 1}
    gs["mem_ln_g"], gs["mem_ln_b"] = _mem_ln_bwd(dmem_n, mem_hat)
    (r_small,) = _all_gather([_pack([gs[n] for n, _ in _SMALL])], name="gather_small")

    res = {}
    r_win = jnp.concatenate([recv[l][0] for l in range(NL)], axis=1)
    res["w_in"] = [a.reshape(NL, D, SHARD_IN) for a in
                   _adamw(w_in.reshape(NL * D, SHARD_IN), m_w_in.reshape(NL * D, SHARD_IN), v_w_in.reshape(NL * D, SHARD_IN),
                          r_win, name="adamw_w_in", tr=128)]
    pk = lambda d: jnp.concatenate([mid_pack(d, l) for l in range(NL)], axis=0)
    o16 = _adamw(pk(wts), pk(mom), pk(vel), jnp.concatenate([recv[l][1] for l in range(NL)], axis=1), name="adamw_mid", tr=1024)
    for idx, (n, shp, nr) in enumerate(zip(mid, mid_shapes, mid_nrows)):
        off = [l * sum(mid_nrows) + sum(mid_nrows[:idx]) for l in range(NL)]
        res[n] = [jnp.stack([o[f:f + nr].reshape(shp) for f in off]) for o in o16]

    def small_view(d, n):
        a = d[n]
        if n in ("conv_a_w", "conv_b_w"):
            fullw = jnp.zeros(a.shape[:2] + (512,), F32)
            return lax.dynamic_update_slice(fullw, a, (0, 0, me * (512 // NDEV)))
        return a

    spk = lambda d: _pack([small_view(d, n) for n, _ in _SMALL])
    osm = _adamw(spk(wts), spk(mom), spk(vel), r_small, name="adamw_small", tr=1024)
    osm = [_unpack(o, [s for _, s in _SMALL]) for o in osm]
    for idx, (n, _) in enumerate(_SMALL):
        vals = [o[idx] for o in osm]
        if n in ("conv_a_w", "conv_b_w"):
            vals = [lax.dynamic_slice(a, (0, 0, me * (512 // NDEV)), a.shape[:2] + (512 // NDEV,)) for a in vals]
        res[n] = vals

    outs = [loss, dx[None]]
    for k in range(4):
        outs += [res[n][k] for n in names]
    return tuple(outs)
```
